```python
import math
import jax
import jax.numpy as jnp
from jax import lax
import numpy as np

D_MODEL = 1024
BATCH = 2
SEQ = 16384
DEPTH = 4

GRID_W = 64
CTX_LEN = 256
NA_HEAD_DIM = 64
NA_HEADS = D_MODEL // NA_HEAD_DIM
NA_WIDTH = NA_HEADS * NA_HEAD_DIM
NA_ROWS = 8
NA_COLS = 16
NA_QBLOCK = 128
SSM_HEAD_DIM = 64
SSM_INNER = D_MODEL
SSM_HEADS = SSM_INNER // SSM_HEAD_DIM
SSM_GROUPS = 4
SSM_STATE = 128
SSM_CONV = 5
SSM_CHUNK = 128
SSM_CONV_DIM = SSM_INNER + 2 * SSM_GROUPS * SSM_STATE
EVEN_IN = 3 * NA_WIDTH + SSM_INNER + SSM_CONV_DIM + 2 * SSM_HEADS
EVEN_MIX = NA_WIDTH + SSM_INNER
SC_WIDTH = D_MODEL
SC_CONV = 3
FFN_DIM = 256 * ((8 * D_MODEL // 3 + 255) // 256)
N_EXPERTS = 8
TOP_K = 2
EXPERT_DIM = 7 * D_MODEL // 2
MOE_BLOCK = 128
N_EVEN = (DEPTH + 1) // 2
N_ODD = DEPTH // 2
DEEPNORM_ALPHA = (2 * DEPTH) ** 0.25
DEEPNORM_BETA = (8 * DEPTH) ** -0.25
LN_EPS = 1e-5
RMS_EPS = 1e-5

kernel_name = 'hybrid_natten_ssd_shortconv_moe'


def _layernorm(x, g, b):
    xf = x.astype(jnp.float32)
    mu = jnp.mean(xf, axis=-1, keepdims=True)
    var = jnp.mean(jnp.square(xf - mu), axis=-1, keepdims=True)
    return ((xf - mu) * lax.rsqrt(var + LN_EPS) * g + b).astype(x.dtype)


def _modulate(h, shift, scale):
    return h * (1 + scale) + shift


def _dwconv(x, w):
    k = w.shape[0]
    return lax.conv_general_dilated(
        x, w[:, None, :].astype(x.dtype), window_strides=(1,), padding=[(k // 2, k // 2)],
        dimension_numbers=('NWC', 'WIO', 'NWC'), feature_group_count=x.shape[-1])


def _neighbourhood_tables(n_tok):
    rows = n_tok // GRID_W
    kr = min(NA_ROWS, rows)
    r = np.arange(rows)
    col = np.arange(GRID_W)
    r0 = np.clip(r - kr // 2, 0, rows - kr)
    c0 = np.clip(col - NA_COLS // 2, 0, GRID_W - NA_COLS)
    key_r = r0[:, None] + np.arange(kr)
    key_c = c0[:, None] + np.arange(NA_COLS)
    idx = key_r[:, None, :, None] * GRID_W + key_c[None, :, None, :]
    dr = key_r - r[:, None] + (NA_ROWS - 1)
    dc = key_c - col[:, None] + (NA_COLS - 1)
    bidx = dr[:, None, :, None] * (2 * NA_COLS - 1) + dc[None, :, None, :]
    n_keys = kr * NA_COLS
    return (jnp.asarray(idx.reshape(n_tok, n_keys), jnp.int32),
            jnp.asarray(bidx.reshape(n_tok, n_keys), jnp.int32))


def _neighbourhood_attention(q, k, v, k_ctx, v_ctx, rpb, nbr_idx, bias_idx):
    b, s, h, dh = q.shape
    n_keys = nbr_idx.shape[1]
    nblk = s // NA_QBLOCK
    scale = dh ** -0.5
    rpb_flat = rpb.reshape(h, -1)
    q_blocks = jnp.moveaxis(q.reshape(b, nblk, NA_QBLOCK, h, dh), 1, 0)
    i_blocks = nbr_idx.reshape(nblk, NA_QBLOCK, n_keys)
    b_blocks = bias_idx.reshape(nblk, NA_QBLOCK, n_keys)

    def block(args):
        q_blk, i_blk, b_blk = args
        k_blk = k[:, i_blk]
        v_blk = v[:, i_blk]
        s_loc = jnp.einsum('bqhd,bqkhd->bhqk', q_blk, k_blk) * scale + rpb_flat[:, b_blk]
        s_ctx = jnp.einsum('bqhd,bchd->bhqc', q_blk, k_ctx) * scale
        p = jax.nn.softmax(jnp.concatenate([s_loc, s_ctx], axis=-1).astype(jnp.float32), axis=-1)
        p = p.astype(v.dtype)
        return (jnp.einsum('bhqk,bqkhd->bqhd', p[..., :n_keys], v_blk)
                + jnp.einsum('bhqc,bchd->bqhd', p[..., n_keys:], v_ctx))

    o = lax.map(block, (q_blocks, i_blocks, b_blocks))
    return jnp.moveaxis(o, 0, 1).reshape(b, s, h * dh)


def _context_attention(q, k, v):
    s = jnp.einsum('bqhd,bkhd->bhqk', q, k) * q.shape[-1] ** -0.5
    p = jax.nn.softmax(s.astype(jnp.float32), axis=-1).astype(v.dtype)
    return jnp.einsum('bhqk,bkhd->bqhd', p, v).reshape(q.shape[0], q.shape[1], -1)


def _ssd_scan(x, dt, a, bm, cm, h0):
    b, l, h, p = x.shape
    g, n = bm.shape[2], bm.shape[3]
    hg = h // g
    nc = l // SSM_CHUNK

    def chunks(t):
        return jnp.moveaxis(t.reshape((b, nc, SSM_CHUNK) + t.shape[2:]), 1, 0)

    xs = chunks((x.astype(jnp.float32) * dt[..., None]).reshape(b, l, g, hg, p))
    las = chunks((dt * a).reshape(b, l, g, hg))
    bs = chunks(bm.astype(jnp.float32))
    cs = chunks(cm.astype(jnp.float32))
    tril = jnp.tril(jnp.ones((SSM_CHUNK, SSM_CHUNK), dtype=bool))[None, :, :, None, None]

    def step(state, inp):
        xc, lac, bc, cc = inp
        cum = jnp.cumsum(lac, axis=1)
        seg = cum[:, :, None] - cum[:, None, :]
        decay = jnp.exp(jnp.where(tril, seg, -jnp.inf))
        cb = jnp.einsum('bign,bjgn->bijg', cc, bc)
        y = jnp.einsum('bijg,bijgh,bjghp->bighp', cb, decay, xc)
        y = y + jnp.einsum('bign,bghpn->bighp', cc, state) * jnp.exp(cum)[..., None]
        last = cum[:, -1]
        w = jnp.exp(last[:, None] - cum)
        state = (state * jnp.exp(last)[..., None, None]
                 + jnp.einsum('bjgn,bjgh,bjghp->bghpn', bc, w, xc))
        return state, y

    h_fin, ys = lax.scan(step, h0, (xs, las, bs, cs))
    y = jnp.moveaxis(ys, 0, 1).reshape(b, l, h, p)
    return y.astype(x.dtype), h_fin


def _gated_group_rmsnorm(y, z, w):
    u = (y * jax.nn.silu(z)).astype(jnp.float32)
    ug = u.reshape(u.shape[:-1] + (SSM_GROUPS, -1))
    ug = ug * lax.rsqrt(jnp.mean(jnp.square(ug), axis=-1, keepdims=True) + RMS_EPS)
    return (ug.reshape(u.shape) * w).astype(y.dtype)


def _even_mixer(u_lat, u_ctx, w_in, rpb, conv_w, conv_b, dt_bias, a_log, d_skip, norm_w, w_out,
                nbr_idx, bias_idx, with_ctx_out):
    offs = [NA_WIDTH, 2 * NA_WIDTH, 3 * NA_WIDTH, 3 * NA_WIDTH + SSM_INNER,
            3 * NA_WIDTH + SSM_INNER + SSM_CONV_DIM]

    def split(pr):
        b, l, _ = pr.shape
        q, k, v, z, xbc, dt_raw = jnp.split(pr, offs, axis=-1)
        heads = lambda t: t.reshape(b, l, NA_HEADS, NA_HEAD_DIM)
        return heads(q), heads(k), heads(v), z, xbc, dt_raw

    def ssm_inputs(xbc, dt_raw):
        b, l, _ = xbc.shape
        xbc = jax.nn.silu(_dwconv(xbc, conv_w) + conv_b)
        xs, bm, cm = jnp.split(xbc, [SSM_INNER, SSM_INNER + SSM_GROUPS * SSM_STATE], axis=-1)
        xs = xs.reshape(b, l, SSM_HEADS, SSM_HEAD_DIM)
        bm = bm.reshape(b, l, SSM_GROUPS, SSM_STATE)
        cm = cm.reshape(b, l, SSM_GROUPS, SSM_STATE)
        dt = jax.nn.softplus(dt_raw.astype(jnp.float32).reshape(b, l, 2, SSM_HEADS)
                             + dt_bias.astype(jnp.float32))
        return xs, bm, cm, dt

    flip = lambda t: t[:, ::-1]

    q_l, k_l, v_l, z_l, xbc_l, dtr_l = split(u_lat @ w_in)
    q_c, k_c, v_c, z_c, xbc_c, dtr_c = split(u_ctx @ w_in)

    attn_l = _neighbourhood_attention(q_l, k_l, v_l, k_c, v_c, rpb, nbr_idx, bias_idx)

    a = -jnp.exp(a_log.astype(jnp.float32))
    x_l, b_l, c_l, dt_l = ssm_inputs(xbc_l, dtr_l)
    x_c, b_c, c_c, dt_c = ssm_inputs(xbc_c, dtr_c)
    h0 = jnp.zeros((x_c.shape[0], SSM_GROUPS, SSM_HEADS // SSM_GROUPS, SSM_HEAD_DIM, SSM_STATE),
                   jnp.float32)
    yc_f, hc_f = _ssd_scan(x_c, dt_c[:, :, 0], a[0], b_c, c_c, h0)
    yl_f, _ = _ssd_scan(x_l, dt_l[:, :, 0], a[0], b_l, c_l, hc_f)
    yc_b, hc_b = _ssd_scan(flip(x_c), flip(dt_c[:, :, 1]), a[1], flip(b_c), flip(c_c), h0)
    yl_b, _ = _ssd_scan(flip(x_l), flip(dt_l[:, :, 1]), a[1], flip(b_l), flip(c_l), hc_b)

    def ssm_out(xs, y_f, y_b_rev, z):
        b, l = xs.shape[0], xs.shape[1]
        y = y_f + flip(y_b_rev) + xs * d_skip[:, None]
        return _gated_group_rmsnorm(y.reshape(b, l, SSM_INNER), z, norm_w)

    y_lat = jnp.concatenate([attn_l, ssm_out(x_l, yl_f, yl_b, z_l)], axis=-1) @ w_out
    if not with_ctx_out:
        return y_lat, None
    attn_c = _context_attention(q_c, k_c, v_c)
    y_ctx = jnp.concatenate([attn_c, ssm_out(x_c, yc_f, yc_b, z_c)], axis=-1) @ w_out
    return y_lat, y_ctx


def _short_conv_mixer(u, w_in, conv_w, w_out):
    gate_b, gate_c, h = jnp.split(u @ w_in, 3, axis=-1)
    return (gate_b * _dwconv(gate_c * h, conv_w)) @ w_out


def _swiglu(h, w1, w3, w2):
    return (jax.nn.silu(h @ w1) * (h @ w3)) @ w2


def _moe_swiglu(h, w_router, w1, w3, w2):
    shp = h.shape
    xt = h.reshape(-1, shp[-1])
    t = xt.shape[0]
    logits = (xt @ w_router).astype(jnp.float32)
    top_v, top_e = lax.top_k(logits, TOP_K)
    gates = jax.nn.softmax(top_v, axis=-1)
    n_assign = t * TOP_K
    flat_e = top_e.reshape(-1).astype(jnp.int32)
    order = jnp.argsort(flat_e)
    sorted_e = flat_e[order]
    counts = jnp.bincount(flat_e, length=N_EXPERTS).astype(jnp.int32)
    padded = (counts + MOE_BLOCK - 1) // MOE_BLOCK * MOE_BLOCK
    starts = jnp.cumsum(counts) - counts
    pad_ends = jnp.cumsum(padded)
    pad_starts = pad_ends - padded
    dest_sorted = (pad_starts[sorted_e] + jnp.arange(n_assign, dtype=jnp.int32)
                   - starts[sorted_e]).astype(jnp.int32)
    n_blocks = -(-n_assign // MOE_BLOCK) + N_EXPERTS
    n_rows = n_blocks * MOE_BLOCK
    row_token = jnp.zeros((n_rows,), jnp.int32).at[dest_sorted].set(
        (order // TOP_K).astype(jnp.int32))
    block_expert = jnp.minimum(
        jnp.searchsorted(pad_ends, jnp.arange(n_blocks, dtype=jnp.int32) * MOE_BLOCK, side='right'),
        N_EXPERTS - 1)
    rows_in = xt[row_token].reshape(n_blocks, MOE_BLOCK, shp[-1])

    def expert_block(args):
        xb, e = args
        return (jax.nn.silu(xb @ w1[e]) * (xb @ w3[e])) @ w2[e]

    rows_out = lax.map(expert_block, (rows_in, block_expert)).reshape(n_rows, shp[-1])
    dest = jnp.zeros((n_assign,), jnp.int32).at[order].set(dest_sorted)
    y = jnp.einsum('tkd,tk->td', rows_out[dest].reshape(t, TOP_K, shp[-1]), gates.astype(h.dtype))
    return y.reshape(shp)


def setup_inputs(seed: int = 0) -> dict:
    key = jax.random.key(seed)
    ks = jax.random.split(key, 32)
    f32 = jnp.float32
    d = D_MODEL
    beta = DEEPNORM_BETA

    def nrm(k, shape, scale):
        return jax.random.normal(k, shape, f32) * scale

    dt0 = jnp.exp(jax.random.uniform(ks[12], (N_EVEN, 2, SSM_HEADS), f32,
                                     math.log(1e-3), math.log(1e-1)))
    return {
        'x': nrm(ks[0], (BATCH, SEQ, d), 1.0),
        'c': nrm(ks[1], (BATCH, d), 1.0),
        'ctx': nrm(ks[2], (BATCH, CTX_LEN, d), 1.0),
        'c_ctx': nrm(ks[3], (d,), 1.0),
        'ada_w': nrm(ks[4], (DEPTH, d, 6 * d), 0.5 * d ** -0.5),
        'ada_b': nrm(ks[5], (DEPTH, 6 * d), 0.02),
        'ln_g': 1.0 + nrm(ks[6], (DEPTH, 2, d), 0.02),
        'ln_b': nrm(ks[7], (DEPTH, 2, d), 0.02),
        'even_w_in': nrm(ks[8], (N_EVEN, d, EVEN_IN), d ** -0.5),
        'na_rpb': nrm(ks[9], (N_EVEN, NA_HEADS, 2 * NA_ROWS - 1, 2 * NA_COLS - 1), 0.1),
        'ssm_conv_w': nrm(ks[10], (N_EVEN, SSM_CONV, SSM_CONV_DIM), SSM_CONV ** -0.5),
        'ssm_conv_b': nrm(ks[11], (N_EVEN, SSM_CONV_DIM), 0.02),
        'ssm_dt_bias': dt0 + jnp.log(-jnp.expm1(-dt0)),
        'ssm_a_log': jnp.log(jax.random.uniform(ks[13], (N_EVEN, 2, SSM_HEADS), f32, 1.0, 16.0)),
        'ssm_d': 1.0 + nrm(ks[14], (N_EVEN, SSM_HEADS), 0.1),
        'ssm_norm_w': 1.0 + nrm(ks[15], (N_EVEN, SSM_INNER), 0.02),
        'even_w_out': nrm(ks[16], (N_EVEN, EVEN_MIX, d), beta * EVEN_MIX ** -0.5),
        'ffn_w1': nrm(ks[17], (N_EVEN, d, FFN_DIM), d ** -0.5),
        'ffn_w3': nrm(ks[18], (N_EVEN, d, FFN_DIM), d ** -0.5),
        'ffn_w2': nrm(ks[19], (N_EVEN, FFN_DIM, d), beta * FFN_DIM ** -0.5),
        'sc_w_in': nrm(ks[20], (N_ODD, d, 3 * SC_WIDTH), d ** -0.5),
        'sc_conv_w': nrm(ks[21], (N_ODD, SC_CONV, SC_WIDTH), SC_CONV ** -0.5),
        'sc_w_out': nrm(ks[22], (N_ODD, SC_WIDTH, d), beta * SC_WIDTH ** -0.5),
        'moe_router': nrm(ks[23], (N_ODD, d, N_EXPERTS), d ** -0.5),
        'moe_w1': nrm(ks[24], (N_ODD, N_EXPERTS, d, EXPERT_DIM), d ** -0.5),
        'moe_w3': nrm(ks[25], (N_ODD, N_EXPERTS, d, EXPERT_DIM), d ** -0.5),
        'moe_w2': nrm(ks[26], (N_ODD, N_EXPERTS, EXPERT_DIM, d), beta * EXPERT_DIM ** -0.5),
    }


def reference(x, c, ctx, c_ctx, ada_w, ada_b, ln_g, ln_b,
              even_w_in, na_rpb, ssm_conv_w, ssm_conv_b, ssm_dt_bias, ssm_a_log, ssm_d,
              ssm_norm_w, even_w_out, ffn_w1, ffn_w3, ffn_w2,
              sc_w_in, sc_conv_w, sc_w_out, moe_router, moe_w1, moe_w3, moe_w2):
    nbr_idx, bias_idx = _neighbourhood_tables(x.shape[1])
    silu_c = jax.nn.silu(c)
    silu_cc = jax.nn.silu(c_ctx)
    h_lat, h_ctx = x, ctx
    for i in range(DEPTH):
        j = i // 2
        even = i % 2 == 0
        ctx_live = any(m % 2 == 0 for m in range(i + 1, DEPTH))
        mods_l = jnp.split((silu_c @ ada_w[i] + ada_b[i])[:, None, :], 6, axis=-1)
        mods_c = jnp.split(silu_cc @ ada_w[i] + ada_b[i], 6, axis=-1)

        u_lat = _modulate(h_lat, mods_l[0], mods_l[1])
        u_ctx = _modulate(h_ctx, mods_c[0], mods_c[1])
        if even:
            y_lat, y_ctx = _even_mixer(u_lat, u_ctx, even_w_in[j], na_rpb[j], ssm_conv_w[j],
                                       ssm_conv_b[j], ssm_dt_bias[j], ssm_a_log[j], ssm_d[j],
                                       ssm_norm_w[j], even_w_out[j], nbr_idx, bias_idx, ctx_live)
        else:
            y_lat = _short_conv_mixer(u_lat, sc_w_in[j], sc_conv_w[j], sc_w_out[j])
            y_ctx = (_short_conv_mixer(u_ctx, sc_w_in[j], sc_conv_w[j], sc_w_out[j])
                     if ctx_live else None)
        h_lat = _layernorm(DEEPNORM_ALPHA * h_lat + mods_l[2] * y_lat, ln_g[i, 0], ln_b[i, 0])
        if ctx_live:
            h_ctx = _layernorm(DEEPNORM_ALPHA * h_ctx + mods_c[2] * y_ctx, ln_g[i, 0], ln_b[i, 0])

        def channel(t):
            if even:
                return _swiglu(t, ffn_w1[j], ffn_w3[j], ffn_w2[j])
            return _moe_swiglu(t, moe_router[j], moe_w1[j], moe_w3[j], moe_w2[j])

        f_lat = channel(_modulate(h_lat, mods_l[3], mods_l[4]))
        h_lat = _layernorm(DEEPNORM_ALPHA * h_lat + mods_l[5] * f_lat, ln_g[i, 1], ln_b[i, 1])
        if ctx_live:
            f_ctx = channel(_modulate(h_ctx, mods_c[3], mods_c[4]))
            h_ctx = _layernorm(DEEPNORM_ALPHA * h_ctx + mods_c[5] * f_ctx, ln_g[i, 1], ln_b[i, 1])
    return h_lat
```

```python
import functools

import numpy as np
import jax
import jax.numpy as jnp
from jax import lax
from jax.experimental import pallas as pl
from jax.experimental.pallas import tpu as pltpu

F32 = jnp.float32
BF16 = jnp.bfloat16
HIGHEST = lax.Precision.HIGHEST

GRID_W = 64
NA_HEAD_DIM = 64
NA_ROWS = 8
NA_COLS = 16
SSM_HEAD_DIM = 64
SSM_HEADS = 16
SSM_GROUPS = 4
SSM_STATE = 128
SSM_CONV = 5
SSM_CHUNK = 128
SC_CONV = 3
N_EXPERTS = 8
TOP_K = 2
LN_EPS = 1e-5
RMS_EPS = 1e-5

LANES = 128
BF16_SUBLANES = 16
NEG = -1e30
VMEM_LIMIT = 56 * 1024 * 1024

TM_LAT = 1024
TM_MLP = 512
NA_QR = 2
MOE_TILE = 256
FFN_CHUNK = 256
MOE_CHUNK = 512


def _cparams(sem, vmem=None):
    return pltpu.CompilerParams(dimension_semantics=sem, vmem_limit_bytes=vmem or VMEM_LIMIT)


def _silu(x):
    return x * jax.nn.sigmoid(x)


def _softplus(x):
    return jnp.maximum(x, 0.0) + jnp.log(1.0 + jnp.exp(-jnp.abs(x)))


def _ln_residual(h, y, gate, g, b, alpha):
    v = alpha * h + gate * y
    mu = jnp.mean(v, axis=-1, keepdims=True)
    d = v - mu
    var = jnp.mean(d * d, axis=-1, keepdims=True)
    return d * lax.rsqrt(var + LN_EPS) * g + b


def _mod_row(ref, rid):
    return ref[0, pl.ds(rid, 1), :]


def _mods_body(c_ref, w_ref, b_ref, o_ref):
    s = _silu(c_ref[...])
    o_ref[0] = jnp.dot(s, w_ref[0], precision=HIGHEST, preferred_element_type=F32) + b_ref[0]


def _mods_call(cvec, ada_w, ada_b):
    depth, d, n = ada_w.shape
    tn = 1024
    return pl.pallas_call(
        _mods_body,
        grid=(depth, n // tn),
        in_specs=[pl.BlockSpec((8, d), lambda l, j: (0, 0)),
                  pl.BlockSpec((1, d, tn), lambda l, j: (l, 0, j)),
                  pl.BlockSpec((1, 1, tn), lambda l, j: (l, 0, j))],
        out_specs=pl.BlockSpec((1, 8, tn), lambda l, j: (l, 0, j)),
        out_shape=jax.ShapeDtypeStruct((depth, 8, n), F32),
        compiler_params=_cparams(("parallel", "parallel")),
        name="mods",
    )(cvec, ada_w, ada_b.reshape(depth, 1, n))


def _modmm_body(h_ref, sh_ref, sc_ref, w_ref, o_ref, u_scr, *, row_base, tiles_per_row):
    i = pl.program_id(0)
    j = pl.program_id(1)

    @pl.when(j == 0)
    def _():
        rid = row_base + i // tiles_per_row
        u_scr[...] = (h_ref[...] * (1.0 + _mod_row(sc_ref, rid)) + _mod_row(sh_ref, rid)).astype(BF16)

    o_ref[...] = jnp.dot(u_scr[...], w_ref[...], preferred_element_type=F32).astype(o_ref.dtype)


def _modmm_call(h, mods, layer, k_shift, k_scale, w, out_dtype, *, tm, tn, row_base, tiles_per_row, name):
    t, d = h.shape
    n = w.shape[1]
    body = functools.partial(_modmm_body, row_base=row_base, tiles_per_row=tiles_per_row)
    return pl.pallas_call(
        body,
        grid=(t // tm, n // tn),
        in_specs=[pl.BlockSpec((tm, d), lambda i, j: (i, 0)),
                  pl.BlockSpec((1, 8, d), lambda i, j: (layer, 0, k_shift)),
                  pl.BlockSpec((1, 8, d), lambda i, j: (layer, 0, k_scale)),
                  pl.BlockSpec((d, tn), lambda i, j: (0, j))],
        out_specs=pl.BlockSpec((tm, tn), lambda i, j: (i, j)),
        out_shape=jax.ShapeDtypeStruct((t, n), out_dtype),
        scratch_shapes=[pltpu.VMEM((tm, d), BF16)],
        compiler_params=_cparams(("parallel", "arbitrary")),
        name=name,
    )(h, mods, mods, w)


def _rpb_onehot():
    qc = np.arange(GRID_W)[:, None]
    kc = np.arange(GRID_W)[None, :]
    c0 = np.clip(qc - NA_COLS // 2, 0, GRID_W - NA_COLS)
    inside = (kc >= c0) & (kc < c0 + NA_COLS)
    dc = kc - qc + NA_COLS - 1
    oh = np.zeros((LANES, GRID_W, GRID_W), np.float32)
    for d in range(2 * NA_COLS - 1):
        oh[d] = ((dc == d) & inside).astype(np.float32)
    mask = np.where(inside, 0.0, NEG).astype(np.float32)
    return oh.reshape(LANES, GRID_W * GRID_W), mask.reshape(1, GRID_W * GRID_W)


def _rpb_body(r_ref, oh_ref, m_ref, o_ref):
    o_ref[...] = jnp.dot(r_ref[...], oh_ref[...], precision=HIGHEST,
                         preferred_element_type=F32) + m_ref[...]


def _rpb_table(rpb):
    h = rpb.shape[0]
    nd = 2 * NA_ROWS - 1
    oh, mask = _rpb_onehot()
    r2 = jnp.pad(rpb.reshape(h * nd, 2 * NA_COLS - 1), ((0, 0), (0, LANES - 2 * NA_COLS + 1)))
    e = pl.pallas_call(
        _rpb_body,
        out_shape=jax.ShapeDtypeStruct((h * nd, GRID_W * GRID_W), F32),
        name="rpb_table",
    )(r2, jnp.asarray(oh), jnp.asarray(mask))
    e = e.reshape(h, nd, GRID_W, GRID_W)
    e = jnp.pad(e, ((0, 0), (2, 3), (0, 0), (0, 0)), constant_values=NEG)
    return jnp.concatenate([e[:, :-1], e[:, 1:]], axis=-1)


def _natten_body(q_ref, k_ref, v_ref, kc_ref, vc_ref, e_ref, o_ref, s_scr, *, qr, rows, lc):
    win = qr + NA_ROWS
    nq = qr * GRID_W
    nk = win * GRID_W
    rb = pl.program_id(2)
    r_first = rb * qr
    ws = jnp.clip(r_first - NA_ROWS // 2, 0, rows - win)
    kstart = pl.multiple_of(ws * GRID_W, GRID_W)
    kwin = k_ref[pl.ds(kstart, nk), :]
    vwin = v_ref[pl.ds(kstart, nk), :]
    kc = kc_ref[...]
    vc = vc_ref[...]
    q = q_ref[...]
    lane = lax.broadcasted_iota(jnp.int32, (1, LANES), 1)
    lo = lane < NA_HEAD_DIM
    scale = NA_HEAD_DIM ** -0.5
    outs = []
    for a in range(2):
        sel = lo if a == 0 else jnp.logical_not(lo)
        qa = jnp.where(sel, q, jnp.zeros_like(q)) * jnp.asarray(scale, BF16)
        s_loc = lax.dot_general(qa, kwin, (((1,), (1,)), ((), ())), preferred_element_type=F32)
        s_ctx = lax.dot_general(qa, kc, (((1,), (1,)), ((), ())), preferred_element_type=F32)
        for aq in range(qr):
            r = r_first + aq
            r0 = jnp.clip(r - NA_ROWS // 2, 0, rows - NA_ROWS)
            for j in range(win // 2):
                kr = ws + 2 * j
                d = kr - r + (NA_ROWS - 1)
                v_lo = jnp.logical_and(kr >= r0, kr < r0 + NA_ROWS).astype(jnp.int32)
                v_hi = jnp.logical_and(kr + 1 >= r0, kr + 1 < r0 + NA_ROWS).astype(jnp.int32)
                valid = jnp.where(lo, v_lo, v_hi) > 0
                blk = s_loc[aq * GRID_W:(aq + 1) * GRID_W, j * LANES:(j + 1) * LANES]
                s_scr[aq * GRID_W:(aq + 1) * GRID_W, j * LANES:(j + 1) * LANES] = jnp.where(
                    valid, blk + e_ref[a, d + 2], NEG)
        s_scr[:, nk:nk + lc] = s_ctx
        s = s_scr[...]
        m = jnp.max(s, axis=-1, keepdims=True)
        p = jnp.exp(s - m)
        l = jnp.sum(p, axis=-1, keepdims=True)
        pb = p.astype(BF16)
        o = (jnp.dot(pb[:, :nk], vwin, preferred_element_type=F32)
             + jnp.dot(pb[:, nk:], vc, preferred_element_type=F32))
        outs.append(o / l)
    o_ref[...] = jnp.where(lo, outs[0], outs[1]).astype(o_ref.dtype)


def _natten_call(proj_l, proj_c, e2, batch, seq, lc):
    d = e2.shape[0] * NA_HEAD_DIM
    npair = d // LANES
    rows = seq // GRID_W
    qr = NA_QR
    nrb = rows // qr
    nq = qr * GRID_W
    win = qr + NA_ROWS
    body = functools.partial(_natten_body, qr=qr, rows=rows, lc=lc)
    return pl.pallas_call(
        body,
        grid=(batch, npair, nrb),
        in_specs=[pl.BlockSpec((nq, LANES), lambda b, p, r: (b * nrb + r, p)),
                  pl.BlockSpec((seq, LANES), lambda b, p, r: (b, npair + p)),
                  pl.BlockSpec((seq, LANES), lambda b, p, r: (b, 2 * npair + p)),
                  pl.BlockSpec((lc, LANES), lambda b, p, r: (b, npair + p)),
                  pl.BlockSpec((lc, LANES), lambda b, p, r: (b, 2 * npair + p)),
                  pl.BlockSpec((2, e2.shape[1], GRID_W, LANES), lambda b, p, r: (p, 0, 0, 0))],
        out_specs=pl.BlockSpec((nq, LANES), lambda b, p, r: (b * nrb + r, p)),
        out_shape=jax.ShapeDtypeStruct((batch * seq, d), BF16),
        scratch_shapes=[pltpu.VMEM((nq, win * GRID_W + lc), F32)],
        compiler_params=_cparams(("parallel", "parallel", "arbitrary")),
        name="natten",
    )(proj_l, proj_l, proj_l, proj_c, proj_c, e2)


def _ctxattn_body(q_ref, k_ref, v_ref, o_ref):
    q = q_ref[...]
    k = k_ref[...]
    v = v_ref[...]
    lane = lax.broadcasted_iota(jnp.int32, (1, LANES), 1)
    lo = lane < NA_HEAD_DIM
    scale = NA_HEAD_DIM ** -0.5
    outs = []
    for a in range(2):
        sel = lo if a == 0 else jnp.logical_not(lo)
        qa = jnp.where(sel, q, jnp.zeros_like(q)) * jnp.asarray(scale, BF16)
        s = lax.dot_general(qa, k, (((1,), (1,)), ((), ())), preferred_element_type=F32)
        m = jnp.max(s, axis=-1, keepdims=True)
        p = jnp.exp(s - m)
        l = jnp.sum(p, axis=-1, keepdims=True)
        outs.append(jnp.dot(p.astype(BF16), v, preferred_element_type=F32) / l)
    o_ref[...] = jnp.where(lo, outs[0], outs[1]).astype(o_ref.dtype)


def _ctxattn_call(proj_c, batch, lc, d):
    npair = d // LANES
    return pl.pallas_call(
        _ctxattn_body,
        grid=(batch, npair),
        in_specs=[pl.BlockSpec((lc, LANES), lambda b, p: (b, p)),
                  pl.BlockSpec((lc, LANES), lambda b, p: (b, npair + p)),
                  pl.BlockSpec((lc, LANES), lambda b, p: (b, 2 * npair + p))],
        out_specs=pl.BlockSpec((lc, LANES), lambda b, p: (b, p)),
        out_shape=jax.ShapeDtypeStruct((batch * lc, d), BF16),
        compiler_params=_cparams(("parallel", "parallel")),
        name="ctx_attn",
    )(proj_c, proj_c, proj_c)


def _dwconv_silu_body(xp_ref, x_ref, xn_ref, w_ref, b_ref, o_ref, *, tiles_per_seq):
    i = pl.program_id(0)
    k = w_ref.shape[0]
    half = k // 2
    tb = x_ref.shape[0]
    hr = xp_ref.shape[0]
    first = (i % tiles_per_seq) == 0
    last = (i % tiles_per_seq) == tiles_per_seq - 1
    prev = jnp.where(first, 0.0, xp_ref[...].astype(F32))
    nxt = jnp.where(last, 0.0, xn_ref[...].astype(F32))
    ext = jnp.concatenate([prev, x_ref[...].astype(F32), nxt], axis=0)
    acc = jnp.zeros((tb, x_ref.shape[1]), F32) + b_ref[...]
    for t in range(k):
        off = hr - half + t
        acc = acc + ext[off:off + tb, :] * w_ref[pl.ds(t, 1), :]
    o_ref[...] = _silu(acc).astype(o_ref.dtype)


def _dwconv_silu_call(proj, col0, width, w, b, seq, *, tb, tc):
    t = proj.shape[0]
    hr = BF16_SUBLANES
    nrt = t // tb
    cb0 = col0 // tc
    body = functools.partial(_dwconv_silu_body, tiles_per_seq=seq // tb)
    return pl.pallas_call(
        body,
        grid=(nrt, width // tc),
        in_specs=[pl.BlockSpec((hr, tc), lambda i, j: (jnp.maximum(i * (tb // hr) - 1, 0), cb0 + j)),
                  pl.BlockSpec((tb, tc), lambda i, j: (i, cb0 + j)),
                  pl.BlockSpec((hr, tc), lambda i, j: (jnp.minimum((i + 1) * (tb // hr), t // hr - 1), cb0 + j)),
                  pl.BlockSpec((w.shape[0], tc), lambda i, j: (0, j)),
                  pl.BlockSpec((1, tc), lambda i, j: (0, j))],
        out_specs=pl.BlockSpec((tb, tc), lambda i, j: (i, j)),
        out_shape=jax.ShapeDtypeStruct((t, width), BF16),
        compiler_params=_cparams(("parallel", "parallel")),
        name="dwconv_silu",
    )(proj, proj, proj, w, b.reshape(1, width))


def _ssd_body(x_ref, b_ref, c_ref, dt_ref, dtb_ref, alog_ref, h0_ref, y_ref, hT_ref, st_scr, *, nsteps):
    q = SSM_CHUNK
    dr = pl.program_id(1)
    s = pl.program_id(2)

    @pl.when(s == 0)
    def _():
        st_scr[...] = h0_ref[0, 0]

    fwd = dr == 0
    dtv = _softplus(dt_ref[...] + dtb_ref[0])
    la = dtv * (-jnp.exp(alog_ref[0]))
    ii = lax.broadcasted_iota(jnp.int32, (q, q), 0)
    jj = lax.broadcasted_iota(jnp.int32, (q, q), 1)
    tri = (jj - ii) * jnp.where(fwd, 1, -1) <= 0
    cum = jnp.dot(tri.astype(F32), la, precision=HIGHEST, preferred_element_type=F32)
    tot = jnp.sum(la, axis=0, keepdims=True)
    cum_t = cum.T
    lane = lax.broadcasted_iota(jnp.int32, (1, LANES), 1)
    lo = lane < SSM_HEAD_DIM
    heads_per_group = SSM_HEADS // SSM_GROUPS
    for g in range(SSM_GROUPS):
        bg = b_ref[:, g * SSM_STATE:(g + 1) * SSM_STATE]
        cg = c_ref[:, g * SSM_STATE:(g + 1) * SSM_STATE]
        cb = lax.dot_general(cg, bg, (((1,), (1,)), ((), ())), preferred_element_type=F32)
        bg_t = bg.astype(F32).T.astype(BF16)
        for hp in range(g * heads_per_group // 2, (g + 1) * heads_per_group // 2):
            ha, hb = 2 * hp, 2 * hp + 1
            xp = x_ref[:, hp * LANES:(hp + 1) * LANES].astype(F32)
            cca = jnp.broadcast_to(cum[:, ha:ha + 1], (q, q))
            ccb = jnp.broadcast_to(cum[:, hb:hb + 1], (q, q))
            l_a = jnp.exp(jnp.where(tri, cca - cum_t[ha:ha + 1, :], NEG))
            l_b = jnp.exp(jnp.where(tri, ccb - cum_t[hb:hb + 1, :], NEG))
            m_a = (cb * l_a).astype(BF16)
            m_b = (cb * l_b).astype(BF16)
            dtp = jnp.where(lo, jnp.broadcast_to(dtv[:, ha:ha + 1], (q, LANES)),
                            jnp.broadcast_to(dtv[:, hb:hb + 1], (q, LANES)))
            xdt = xp * dtp
            xdt_b = xdt.astype(BF16)
            y_intra = jnp.where(lo, jnp.dot(m_a, xdt_b, preferred_element_type=F32),
                                jnp.dot(m_b, xdt_b, preferred_element_type=F32))
            ccp = jnp.where(lo, cca, ccb)
            st = st_scr[hp]
            y_inter = jnp.dot(cg, st.astype(BF16), preferred_element_type=F32) * jnp.exp(ccp)
            y_ref[0, :, hp * LANES:(hp + 1) * LANES] = y_intra + y_inter
            totp = jnp.where(lo, jnp.broadcast_to(tot[:, ha:ha + 1], (1, LANES)),
                             jnp.broadcast_to(tot[:, hb:hb + 1], (1, LANES)))
            xw = (xdt * jnp.exp(totp - ccp)).astype(BF16)
            st_scr[hp] = st * jnp.exp(totp) + jnp.dot(bg_t, xw, preferred_element_type=F32)

    @pl.when(s == nsteps - 1)
    def _():
        hT_ref[0, 0] = st_scr[...]


def _ssd_call(xbc, dt, dtb, alog, h0, batch, seq):
    inner = SSM_HEADS * SSM_HEAD_DIM
    gn = SSM_GROUPS * SSM_STATE
    nc = seq // SSM_CHUNK
    npair = SSM_HEADS // 2

    def rblk(b, d, s):
        return b * nc + jnp.where(d == 0, s, nc - 1 - s)

    body = functools.partial(_ssd_body, nsteps=nc)
    return pl.pallas_call(
        body,
        grid=(batch, 2, nc),
        in_specs=[pl.BlockSpec((SSM_CHUNK, inner), lambda b, d, s: (rblk(b, d, s), 0)),
                  pl.BlockSpec((SSM_CHUNK, gn), lambda b, d, s: (rblk(b, d, s), inner // gn)),
                  pl.BlockSpec((SSM_CHUNK, gn), lambda b, d, s: (rblk(b, d, s), inner // gn + 1)),
                  pl.BlockSpec((SSM_CHUNK, LANES), lambda b, d, s: (rblk(b, d, s), d)),
                  pl.BlockSpec((1, 1, LANES), lambda b, d, s: (d, 0, 0)),
                  pl.BlockSpec((1, 1, LANES), lambda b, d, s: (d, 0, 0)),
                  pl.BlockSpec((1, 1, npair, SSM_STATE, LANES), lambda b, d, s: (b, d, 0, 0, 0))],
        out_specs=[pl.BlockSpec((1, SSM_CHUNK, inner), lambda b, d, s: (d, rblk(b, d, s), 0)),
                   pl.BlockSpec((1, 1, npair, SSM_STATE, LANES), lambda b, d, s: (b, d, 0, 0, 0))],
        out_shape=[jax.ShapeDtypeStruct((2, batch * seq, inner), F32),
                   jax.ShapeDtypeStruct((batch, 2, npair, SSM_STATE, LANES), F32)],
        scratch_shapes=[pltpu.VMEM((npair, SSM_STATE, LANES), F32)],
        compiler_params=_cparams(("parallel", "parallel", "arbitrary")),
        name="ssd",
    )(xbc, xbc, xbc, dt, dtb, alog, h0)


def _evenout_body(attn_ref, y_ref, xs_ref, z_ref, h_ref, gate_ref, dsk_ref, nw_ref, w_ref, lg_ref, lb_ref,
                  o_ref, ssm_scr, *, row_base, tiles_per_row, alpha):
    i = pl.program_id(0)
    rid = row_base + i // tiles_per_row
    inner = xs_ref.shape[1]
    gw = inner // SSM_GROUPS
    z = z_ref[...].astype(F32)
    y = y_ref[0] + y_ref[1] + xs_ref[...].astype(F32) * dsk_ref[...]
    u = y * _silu(z)
    for g in range(SSM_GROUPS):
        ug = u[:, g * gw:(g + 1) * gw]
        ms = jnp.mean(ug * ug, axis=-1, keepdims=True)
        ssm_scr[:, g * gw:(g + 1) * gw] = (ug * lax.rsqrt(ms + RMS_EPS) * nw_ref[:, g * gw:(g + 1) * gw]).astype(BF16)
    d_attn = attn_ref.shape[1]
    acc = jnp.dot(attn_ref[...], w_ref[:d_attn, :], preferred_element_type=F32)
    acc = acc + jnp.dot(ssm_scr[...], w_ref[d_attn:, :], preferred_element_type=F32)
    o_ref[...] = _ln_residual(h_ref[...], acc, _mod_row(gate_ref, rid), lg_ref[0], lb_ref[0], alpha)


def _evenout_call(attn, y2, xbc, proj, z_cb, h, mods, layer, dskip_row, norm_w, w_out, ln_g, ln_b, ln_idx,
                  *, tm, row_base, tiles_per_row, alpha):
    t, d = h.shape
    inner = y2.shape[2]
    body = functools.partial(_evenout_body, row_base=row_base, tiles_per_row=tiles_per_row, alpha=alpha)
    return pl.pallas_call(
        body,
        grid=(t // tm,),
        in_specs=[pl.BlockSpec((tm, d), lambda i: (i, 0)),
                  pl.BlockSpec((2, tm, inner), lambda i: (0, i, 0)),
                  pl.BlockSpec((tm, inner), lambda i: (i, 0)),
                  pl.BlockSpec((tm, inner), lambda i: (i, z_cb)),
                  pl.BlockSpec((tm, d), lambda i: (i, 0)),
                  pl.BlockSpec((1, 8, d), lambda i: (layer, 0, 2)),
                  pl.BlockSpec((1, inner), lambda i: (0, 0)),
                  pl.BlockSpec((1, inner), lambda i: (0, 0)),
                  pl.BlockSpec(w_out.shape, lambda i: (0, 0)),
                  pl.BlockSpec((1, 1, d), lambda i: (ln_idx, 0, 0)),
                  pl.BlockSpec((1, 1, d), lambda i: (ln_idx, 0, 0))],
        out_specs=pl.BlockSpec((tm, d), lambda i: (i, 0)),
        out_shape=jax.ShapeDtypeStruct((t, d), F32),
        scratch_shapes=[pltpu.VMEM((tm, inner), BF16)],
        compiler_params=_cparams(("parallel",)),
        name="even_out",
    )(attn, y2, xbc, proj, h, mods, dskip_row, norm_w, w_out, ln_g, ln_b)


def _mlp_chunks(u, w1_ref, w3_ref, w2_ref, acc_ref):
    nf = w1_ref.shape[0]

    def step(f, carry):
        a = jnp.dot(u, w1_ref[f], preferred_element_type=F32)
        b = jnp.dot(u, w3_ref[f], preferred_element_type=F32)
        t = (_silu(a) * b).astype(BF16)
        acc_ref[...] += jnp.dot(t, w2_ref[f], preferred_element_type=F32)
        return carry

    acc_ref[...] = jnp.zeros_like(acc_ref)
    lax.fori_loop(0, nf, step, 0)


def _ffn_body(h_ref, sh_ref, sc_ref, gate_ref, w1_ref, w3_ref, w2_ref, lg_ref, lb_ref, o_ref, acc_scr,
              *, row_base, tiles_per_row, alpha):
    i = pl.program_id(0)
    rid = row_base + i // tiles_per_row
    h = h_ref[...]
    u = (h * (1.0 + _mod_row(sc_ref, rid)) + _mod_row(sh_ref, rid)).astype(BF16)
    _mlp_chunks(u, w1_ref, w3_ref, w2_ref, acc_scr)
    o_ref[...] = _ln_residual(h, acc_scr[...], _mod_row(gate_ref, rid), lg_ref[0], lb_ref[0], alpha)


def _ffn_call(h, mods, layer, w1c, w3c, w2c, ln_g, ln_b, ln_idx, *, tm, row_base, tiles_per_row, alpha):
    t, d = h.shape
    body = functools.partial(_ffn_body, row_base=row_base, tiles_per_row=tiles_per_row, alpha=alpha)
    wspec = lambda w: pl.BlockSpec(w.shape, lambda i: (0, 0, 0), pipeline_mode=pl.Buffered(1))
    return pl.pallas_call(
        body,
        grid=(t // tm,),
        in_specs=[pl.BlockSpec((tm, d), lambda i: (i, 0)),
                  pl.BlockSpec((1, 8, d), lambda i: (layer, 0, 3)),
                  pl.BlockSpec((1, 8, d), lambda i: (layer, 0, 4)),
                  pl.BlockSpec((1, 8, d), lambda i: (layer, 0, 5)),
                  wspec(w1c), wspec(w3c), wspec(w2c),
                  pl.BlockSpec((1, 1, d), lambda i: (ln_idx, 0, 0)),
                  pl.BlockSpec((1, 1, d), lambda i: (ln_idx, 0, 0))],
        out_specs=pl.BlockSpec((tm, d), lambda i: (i, 0)),
        out_shape=jax.ShapeDtypeStruct((t, d), F32),
        scratch_shapes=[pltpu.VMEM((tm, d), F32)],
        compiler_params=_cparams(("parallel",)),
        name="ffn",
    )(h, mods, mods, mods, w1c, w3c, w2c, ln_g, ln_b)


def _scout_body(pb_ref, pcp_ref, pc_ref, pcn_ref, php_ref, ph_ref, phn_ref, h_ref, gate_ref, cw_ref, w_ref,
                lg_ref, lb_ref, o_ref, *, row_base, tiles_per_row, tiles_per_seq, alpha):
    i = pl.program_id(0)
    rid = row_base + i // tiles_per_row
    tm = ph_ref.shape[0]
    hr = php_ref.shape[0]
    first = (i % tiles_per_seq) == 0
    last = (i % tiles_per_seq) == tiles_per_seq - 1
    prev = jnp.where(first, 0.0, pcp_ref[...].astype(F32) * php_ref[...].astype(F32))
    cur = pc_ref[...].astype(F32) * ph_ref[...].astype(F32)
    nxt = jnp.where(last, 0.0, pcn_ref[...].astype(F32) * phn_ref[...].astype(F32))
    ext = jnp.concatenate([prev, cur, nxt], axis=0)
    acc = jnp.zeros_like(cur)
    for t in range(SC_CONV):
        off = hr - SC_CONV // 2 + t
        acc = acc + ext[off:off + tm, :] * cw_ref[pl.ds(t, 1), :]
    a = (pb_ref[...].astype(F32) * acc).astype(BF16)
    y = jnp.dot(a, w_ref[...], preferred_element_type=F32)
    o_ref[...] = _ln_residual(h_ref[...], y, _mod_row(gate_ref, rid), lg_ref[0], lb_ref[0], alpha)


def _scout_call(proj, h, mods, layer, conv_w, w_out, ln_g, ln_b, ln_idx, seq,
                *, tm, row_base, tiles_per_row, alpha):
    t, d = h.shape
    hr = BF16_SUBLANES
    r = tm // hr
    nh = t // hr
    body = functools.partial(_scout_body, row_base=row_base, tiles_per_row=tiles_per_row,
                             tiles_per_seq=seq // tm, alpha=alpha)
    prev_map = lambda cb: (lambda i: (jnp.maximum(i * r - 1, 0), cb))
    next_map = lambda cb: (lambda i: (jnp.minimum((i + 1) * r, nh - 1), cb))
    return pl.pallas_call(
        body,
        grid=(t // tm,),
        in_specs=[pl.BlockSpec((tm, d), lambda i: (i, 0)),
                  pl.BlockSpec((hr, d), prev_map(1)),
                  pl.BlockSpec((tm, d), lambda i: (i, 1)),
                  pl.BlockSpec((hr, d), next_map(1)),
                  pl.BlockSpec((hr, d), prev_map(2)),
                  pl.BlockSpec((tm, d), lambda i: (i, 2)),
                  pl.BlockSpec((hr, d), next_map(2)),
                  pl.BlockSpec((tm, d), lambda i: (i, 0)),
                  pl.BlockSpec((1, 8, d), lambda i: (layer, 0, 2)),
                  pl.BlockSpec(conv_w.shape, lambda i: (0, 0)),
                  pl.BlockSpec(w_out.shape, lambda i: (0, 0)),
                  pl.BlockSpec((1, 1, d), lambda i: (ln_idx, 0, 0)),
                  pl.BlockSpec((1, 1, d), lambda i: (ln_idx, 0, 0))],
        out_specs=pl.BlockSpec((tm, d), lambda i: (i, 0)),
        out_shape=jax.ShapeDtypeStruct((t, d), F32),
        compiler_params=_cparams(("parallel",)),
        name="shortconv_out",
    )(proj, proj, proj, proj, proj, proj, proj, h, mods, conv_w, w_out, ln_g, ln_b)


def _router_body(h_ref, sh_ref, sc_ref, wr_ref, u_ref, meta_ref, cnt_ref, cnt_scr, *, row_base, tiles_per_row):
    i = pl.program_id(0)
    rid = row_base + i // tiles_per_row
    tm = h_ref.shape[0]

    @pl.when(i == 0)
    def _():
        cnt_scr[...] = jnp.zeros_like(cnt_scr)

    u = h_ref[...] * (1.0 + _mod_row(sc_ref, rid)) + _mod_row(sh_ref, rid)
    u_ref[...] = u
    logits = jnp.dot(u, wr_ref[...], precision=HIGHEST, preferred_element_type=F32)
    lane = lax.broadcasted_iota(jnp.int32, (tm, LANES), 1).astype(F32)
    lg = jnp.where(lane < N_EXPERTS, logits, NEG)
    v1 = jnp.max(lg, axis=-1, keepdims=True)
    e1 = jnp.min(jnp.where(lg == v1, lane, float(LANES)), axis=-1, keepdims=True)
    lg2 = jnp.where(lane == e1, 2 * NEG, lg)
    v2 = jnp.max(lg2, axis=-1, keepdims=True)
    e2 = jnp.min(jnp.where(lg2 == v2, lane, float(LANES)), axis=-1, keepdims=True)
    g2 = 1.0 / (1.0 + jnp.exp(v1 - v2))
    g1 = 1.0 - g2
    oh1 = (lane == e1).astype(F32)
    oh2 = (lane == e2).astype(F32)
    both = (oh1 + oh2).astype(BF16)
    ii = lax.broadcasted_iota(jnp.int32, (tm, tm), 0)
    jj = lax.broadcasted_iota(jnp.int32, (tm, tm), 1)
    strict = (jj < ii).astype(BF16)
    before = jnp.dot(strict, both, preferred_element_type=F32) + cnt_scr[...]
    p1 = jnp.sum(before * oh1, axis=-1, keepdims=True)
    p2 = jnp.sum(before * oh2, axis=-1, keepdims=True)
    cnt_scr[...] = cnt_scr[...] + jnp.sum(oh1 + oh2, axis=0, keepdims=True)
    meta = jnp.where(lane == 0, e1,
           jnp.where(lane == 1, e2,
           jnp.where(lane == 2, p1,
           jnp.where(lane == 3, p2,
           jnp.where(lane == 4, g1,
           jnp.where(lane == 5, g2, 0.0))))))
    meta_ref[...] = meta
    cnt_ref[...] = jnp.broadcast_to(cnt_scr[...], cnt_ref.shape)


def _router_call(h, mods, layer, wr_pad, *, tm, row_base, tiles_per_row):
    t, d = h.shape
    body = functools.partial(_router_body, row_base=row_base, tiles_per_row=tiles_per_row)
    return pl.pallas_call(
        body,
        grid=(t // tm,),
        in_specs=[pl.BlockSpec((tm, d), lambda i: (i, 0)),
                  pl.BlockSpec((1, 8, d), lambda i: (layer, 0, 3)),
                  pl.BlockSpec((1, 8, d), lambda i: (layer, 0, 4)),
                  pl.BlockSpec(wr_pad.shape, lambda i: (0, 0))],
        out_specs=[pl.BlockSpec((tm, d), lambda i: (i, 0)),
                   pl.BlockSpec((tm, LANES), lambda i: (i, 0)),
                   pl.BlockSpec((8, LANES), lambda i: (0, 0))],
        out_shape=[jax.ShapeDtypeStruct((t, d), F32),
                   jax.ShapeDtypeStruct((t, LANES), F32),
                   jax.ShapeDtypeStruct((8, LANES), F32)],
        scratch_shapes=[pltpu.VMEM((1, LANES), F32)],
        compiler_params=_cparams(("arbitrary",)),
        name="moe_router",
    )(h, mods, mods, wr_pad)


def _moe_body(te_ref, rt_ref, u_hbm, w1_hbm, w3_hbm, w2_hbm, o_ref, xbuf, w1s, w3s, w2s, acc_scr, gsem, wsem):
    t = pl.program_id(0)
    tr = xbuf.shape[0]
    e = te_ref[t]
    e_prev = te_ref[jnp.maximum(t - 1, 0)]

    def row_copy(r):
        tok = rt_ref[0, 0, r]
        return pltpu.make_async_copy(u_hbm.at[pl.ds(tok, 1)], xbuf.at[pl.ds(r, 1)], gsem.at[0])

    def issue(r, c):
        row_copy(r).start()
        return c

    lax.fori_loop(0, tr, issue, 0)

    @pl.when(jnp.logical_or(t == 0, e != e_prev))
    def _():
        copies = [pltpu.make_async_copy(w1_hbm.at[e], w1s, wsem.at[0]),
                  pltpu.make_async_copy(w3_hbm.at[e], w3s, wsem.at[1]),
                  pltpu.make_async_copy(w2_hbm.at[e], w2s, wsem.at[2])]
        for cp in copies:
            cp.start()
        for cp in copies:
            cp.wait()

    def drain(r, c):
        row_copy(r).wait()
        return c

    lax.fori_loop(0, tr, drain, 0)
    _mlp_chunks(xbuf[...].astype(BF16), w1s, w3s, w2s, acc_scr)
    o_ref[...] = acc_scr[...]


def _moe_call(u, tile_expert, row_token, w1c, w3c, w2c):
    t, d = u.shape
    n_tiles = tile_expert.shape[0]
    tr = MOE_TILE
    grid_spec = pltpu.PrefetchScalarGridSpec(
        num_scalar_prefetch=1,
        grid=(n_tiles,),
        in_specs=[pl.BlockSpec((1, 1, tr), lambda i, te: (i, 0, 0), memory_space=pltpu.SMEM),
                  pl.BlockSpec(memory_space=pl.ANY),
                  pl.BlockSpec(memory_space=pl.ANY),
                  pl.BlockSpec(memory_space=pl.ANY),
                  pl.BlockSpec(memory_space=pl.ANY)],
        out_specs=pl.BlockSpec((tr, d), lambda i, te: (i, 0)),
        scratch_shapes=[pltpu.VMEM((tr, d), F32),
                        pltpu.VMEM(w1c.shape[1:], BF16),
                        pltpu.VMEM(w3c.shape[1:], BF16),
                        pltpu.VMEM(w2c.shape[1:], BF16),
                        pltpu.VMEM((tr, d), F32),
                        pltpu.SemaphoreType.DMA((1,)),
                        pltpu.SemaphoreType.DMA((3,))],
    )
    return pl.pallas_call(
        _moe_body,
        grid_spec=grid_spec,
        out_shape=jax.ShapeDtypeStruct((n_tiles * tr, d), F32),
        compiler_params=_cparams(("arbitrary",)),
        name="moe_experts",
    )(tile_expert, row_token, u, w1c, w3c, w2c)


def _combine_body(dest_ref, rows_hbm, meta_ref, h_ref, gate_ref, lg_ref, lb_ref, o_ref, rbuf, gsem,
                  *, row_base, tiles_per_row, alpha):
    i = pl.program_id(0)
    rid = row_base + i // tiles_per_row
    tm = h_ref.shape[0]

    def row_copy(r, k):
        src = dest_ref[0, k, r]
        return pltpu.make_async_copy(rows_hbm.at[pl.ds(src, 1)], rbuf.at[k, pl.ds(r, 1)], gsem.at[k])

    def issue(r, c):
        row_copy(r, 0).start()
        row_copy(r, 1).start()
        return c

    def drain(r, c):
        row_copy(r, 0).wait()
        row_copy(r, 1).wait()
        return c

    lax.fori_loop(0, tm, issue, 0)
    lax.fori_loop(0, tm, drain, 0)
    meta = meta_ref[...]
    g1 = meta[:, 4:5]
    g2 = meta[:, 5:6]
    y = g1 * rbuf[0] + g2 * rbuf[1]
    o_ref[...] = _ln_residual(h_ref[...], y, _mod_row(gate_ref, rid), lg_ref[0], lb_ref[0], alpha)


def _combine_call(dest, rows_out, meta, h, mods, layer, ln_g, ln_b, ln_idx, *, tm, row_base, tiles_per_row, alpha):
    t, d = h.shape
    body = functools.partial(_combine_body, row_base=row_base, tiles_per_row=tiles_per_row, alpha=alpha)
    return pl.pallas_call(
        body,
        grid=(t // tm,),
        in_specs=[pl.BlockSpec((1, TOP_K, tm), lambda i: (i, 0, 0), memory_space=pltpu.SMEM),
                  pl.BlockSpec(memory_space=pl.ANY),
                  pl.BlockSpec((tm, LANES), lambda i: (i, 0)),
                  pl.BlockSpec((tm, d), lambda i: (i, 0)),
                  pl.BlockSpec((1, 8, d), lambda i: (layer, 0, 5)),
                  pl.BlockSpec((1, 1, d), lambda i: (ln_idx, 0, 0)),
                  pl.BlockSpec((1, 1, d), lambda i: (ln_idx, 0, 0))],
        out_specs=pl.BlockSpec((tm, d), lambda i: (i, 0)),
        out_shape=jax.ShapeDtypeStruct((t, d), F32),
        scratch_shapes=[pltpu.VMEM((TOP_K, tm, d), F32),
                        pltpu.SemaphoreType.DMA((TOP_K,))],
        compiler_params=_cparams(("arbitrary",)),
        name="moe_combine",
    )(dest, rows_out, meta, h, mods, ln_g, ln_b)


def _chunk_cols(w, fc):
    k, f = w.shape[-2:]
    lead = w.shape[:-2]
    w = w.astype(BF16).reshape(lead + (k, f // fc, fc))
    return jnp.swapaxes(w, -3, -2)


def _chunk_rows(w, fc):
    f, d = w.shape[-2:]
    return w.astype(BF16).reshape(w.shape[:-2] + (f // fc, fc, d))


def _moe_layer(h, mods, layer, wr_pad, w1c, w3c, w2c, ln_g, ln_b, ln_idx, *, tm, row_base, tiles_per_row, alpha):
    t, d = h.shape
    u, meta, cnt = _router_call(h, mods, layer, wr_pad, tm=tm, row_base=row_base, tiles_per_row=tiles_per_row)
    n_assign = t * TOP_K
    n_tiles = n_assign // MOE_TILE + N_EXPERTS
    counts = cnt[0, :N_EXPERTS].astype(jnp.int32)
    padded = (counts + MOE_TILE - 1) // MOE_TILE * MOE_TILE
    pad_ends = jnp.cumsum(padded)
    pad_starts = pad_ends - padded
    top_e = meta[:, 0:TOP_K].astype(jnp.int32)
    rank = meta[:, 2:2 + TOP_K].astype(jnp.int32)
    dest = pad_starts[top_e] + rank
    tok = jnp.broadcast_to(jnp.arange(t, dtype=jnp.int32)[:, None], (t, TOP_K))
    row_token = jnp.zeros((n_tiles * MOE_TILE,), jnp.int32).at[dest.reshape(-1)].set(tok.reshape(-1))
    tile_expert = jnp.minimum(
        jnp.searchsorted(pad_ends, jnp.arange(n_tiles, dtype=jnp.int32) * MOE_TILE, side='right'),
        N_EXPERTS - 1).astype(jnp.int32)
    rows_out = _moe_call(u, tile_expert, row_token.reshape(n_tiles, 1, MOE_TILE), w1c, w3c, w2c)
    dest_t = jnp.swapaxes(dest.reshape(t // tm, tm, TOP_K), 1, 2)
    return _combine_call(dest_t, rows_out, meta, h, mods, layer, ln_g, ln_b, ln_idx,
                         tm=tm, row_base=row_base, tiles_per_row=tiles_per_row, alpha=alpha)


def kernel(x, c, ctx, c_ctx, ada_w, ada_b, ln_g, ln_b, even_w_in, na_rpb, ssm_conv_w, ssm_conv_b, ssm_dt_bias,
           ssm_a_log, ssm_d, ssm_norm_w, even_w_out, ffn_w1, ffn_w3, ffn_w2, sc_w_in, sc_conv_w, sc_w_out,
           moe_router, moe_w1, moe_w3, moe_w2):
    batch, seq, d = x.shape
    lc = ctx.shape[1]
    depth = ada_w.shape[0]
    alpha = (2 * depth) ** 0.25
    inner = SSM_HEADS * SSM_HEAD_DIM
    gn = SSM_GROUPS * SSM_STATE
    n_main = 3 * d + inner + inner + 2 * gn
    assert batch + 1 <= 8

    cvec = jnp.zeros((8, d), F32).at[:batch].set(c).at[batch].set(c_ctx)
    mods = _mods_call(cvec, ada_w, ada_b)
    lng = ln_g.reshape(depth * 2, 1, d)
    lnb = ln_b.reshape(depth * 2, 1, d)

    h_lat = x.reshape(batch * seq, d)
    h_ctx = ctx.reshape(batch * lc, d)
    lat = dict(tm=TM_MLP, row_base=0, tiles_per_row=seq // TM_MLP, alpha=alpha)
    cx = dict(tm=lc, row_base=batch, tiles_per_row=1 << 20, alpha=alpha)

    for i in range(depth):
        j = i // 2
        ctx_live = any(m % 2 == 0 for m in range(i + 1, depth))
        if i % 2 == 0:
            w_in = even_w_in[j]
            w_main = w_in[:, :n_main].astype(BF16)
            w_dt = jnp.zeros((d, 2 * LANES), F32)
            w_dt = w_dt.at[:, :SSM_HEADS].set(w_in[:, n_main:n_main + SSM_HEADS])
            w_dt = w_dt.at[:, LANES:LANES + SSM_HEADS].set(w_in[:, n_main + SSM_HEADS:]).astype(BF16)
            pad16 = ((0, 0), (0, 0), (0, LANES - SSM_HEADS))
            dtb = jnp.pad(ssm_dt_bias[j][:, None, :], pad16)
            alog = jnp.pad(ssm_a_log[j][:, None, :], pad16)
            dskip = jnp.repeat(ssm_d[j], SSM_HEAD_DIM)[None, :]
            norm_w = ssm_norm_w[j][None, :]
            w_out = even_w_out[j].astype(BF16)
            e2 = _rpb_table(na_rpb[j])

            def in_proj(h, tm, row_base, tpr, tag):
                kw = dict(tm=tm, row_base=row_base, tiles_per_row=tpr)
                pm = _modmm_call(h, mods, i, 0, 1, w_main, BF16, tn=512, name="even_in_" + tag, **kw)
                pdt = _modmm_call(h, mods, i, 0, 1, w_dt, F32, tn=LANES, name="even_dt_" + tag, **kw)
                return pm, pdt

            proj_l, dt_l = in_proj(h_lat, TM_LAT, 0, seq // TM_LAT, "lat")
            proj_c, dt_c = in_proj(h_ctx, lc, batch, 1 << 20, "ctx")

            attn_l = _natten_call(proj_l, proj_c, e2, batch, seq, lc)
            xbc_l = _dwconv_silu_call(proj_l, 3 * d + inner, inner + 2 * gn, ssm_conv_w[j], ssm_conv_b[j], seq,
                                      tb=512, tc=512)
            xbc_c = _dwconv_silu_call(proj_c, 3 * d + inner, inner + 2 * gn, ssm_conv_w[j], ssm_conv_b[j], lc,
                                      tb=lc, tc=512)
            h0 = jnp.zeros((batch, 2, SSM_HEADS // 2, SSM_STATE, LANES), F32)
            y_c, h_c = _ssd_call(xbc_c, dt_c, dtb, alog, h0, batch, lc)
            y_l, _ = _ssd_call(xbc_l, dt_l, dtb, alog, h_c, batch, seq)

            z_cb = 3 * d // inner
            h_lat = _evenout_call(attn_l, y_l, xbc_l, proj_l, z_cb, h_lat, mods, i, dskip, norm_w, w_out,
                                  lng, lnb, 2 * i, **lat)
            w1c = _chunk_cols(ffn_w1[j], FFN_CHUNK)
            w3c = _chunk_cols(ffn_w3[j], FFN_CHUNK)
            w2c = _chunk_rows(ffn_w2[j], FFN_CHUNK)
            h_lat = _ffn_call(h_lat, mods, i, w1c, w3c, w2c, lng, lnb, 2 * i + 1, **lat)
            if ctx_live:
                attn_c = _ctxattn_call(proj_c, batch, lc, d)
                h_ctx = _evenout_call(attn_c, y_c, xbc_c, proj_c, z_cb, h_ctx, mods, i, dskip, norm_w, w_out,
                                      lng, lnb, 2 * i, **cx)
                h_ctx = _ffn_call(h_ctx, mods, i, w1c, w3c, w2c, lng, lnb, 2 * i + 1, **cx)
        else:
            w_in = sc_w_in[j].astype(BF16)
            w_out = sc_w_out[j].astype(BF16)
            wr_pad = jnp.pad(moe_router[j], ((0, 0), (0, LANES - N_EXPERTS)))
            w1c = _chunk_cols(moe_w1[j], MOE_CHUNK)
            w3c = _chunk_cols(moe_w3[j], MOE_CHUNK)
            w2c = _chunk_rows(moe_w2[j], MOE_CHUNK)
            proj_l = _modmm_call(h_lat, mods, i, 0, 1, w_in, BF16, tm=TM_LAT, tn=512, row_base=0,
                                 tiles_per_row=seq // TM_LAT, name="odd_in_lat")
            h_lat = _scout_call(proj_l, h_lat, mods, i, sc_conv_w[j], w_out, lng, lnb, 2 * i, seq, **lat)
            h_lat = _moe_layer(h_lat, mods, i, wr_pad, w1c, w3c, w2c, lng, lnb, 2 * i + 1, **lat)
            if ctx_live:
                proj_c = _modmm_call(h_ctx, mods, i, 0, 1, w_in, BF16, tm=lc, tn=512, row_base=batch,
                                     tiles_per_row=1 << 20, name="odd_in_ctx")
                h_ctx = _scout_call(proj_c, h_ctx, mods, i, sc_conv_w[j], w_out, lng, lnb, 2 * i, lc, **cx)
                h_ctx = _moe_layer(h_ctx, mods, i, wr_pad, w1c, w3c, w2c, lng, lnb, 2 * i + 1, **cx)
    return h_lat.reshape(batch, seq, d)
```

```python
import functools

import numpy as np
import jax
import jax.numpy as jnp
from jax import lax
from jax.experimental import pallas as pl
from jax.experimental.pallas import tpu as pltpu

F32 = jnp.float32
BF16 = jnp.bfloat16
HIGHEST = lax.Precision.HIGHEST

GRID_W = 64
NA_HEAD_DIM = 64
NA_ROWS = 8
NA_COLS = 16
SSM_HEAD_DIM = 64
SSM_HEADS = 16
SSM_GROUPS = 4
SSM_STATE = 128
SSM_CONV = 5
SSM_CHUNK = 128
SC_CONV = 3
N_EXPERTS = 8
TOP_K = 2
LN_EPS = 1e-5
RMS_EPS = 1e-5

LANES = 128
BF16_SUBLANES = 16
NEG = -1e30
VMEM_LIMIT = 56 * 1024 * 1024

TM_LAT = 1024
TN_PROJ = 1024
TM_MLP = 512
NA_QR = 4
NA_WIN = NA_QR + NA_ROWS
NA_DMIN = 1 - NA_QR
NA_ND = 2 * (NA_QR + NA_ROWS - 1)
MOE_TILE = 512
ROWCOPY_CHUNK = 512
FFN_CHUNK = 256
MOE_CHUNK = 512


def _cparams(sem, vmem=None):
    return pltpu.CompilerParams(dimension_semantics=sem, vmem_limit_bytes=vmem or VMEM_LIMIT)


def _silu(x):
    return x * jax.nn.sigmoid(x)


def _softplus(x):
    return jnp.maximum(x, 0.0) + jnp.log(1.0 + jnp.exp(-jnp.abs(x)))


def _ln_residual(h, y, gate, g, b, alpha):
    v = alpha * h + gate * y
    mu = jnp.mean(v, axis=-1, keepdims=True)
    d = v - mu
    var = jnp.mean(d * d, axis=-1, keepdims=True)
    return d * lax.rsqrt(var + LN_EPS) * g + b


def _mod_row(ref, rid):
    return ref[0, pl.ds(rid, 1), :]


def _mods_body(c_ref, w_ref, b_ref, o_ref):
    s = _silu(c_ref[...])
    o_ref[0] = jnp.dot(s, w_ref[0], precision=HIGHEST, preferred_element_type=F32) + b_ref[0]


def _mods_call(cvec, ada_w, ada_b):
    depth, d, n = ada_w.shape
    tn = 1024
    return pl.pallas_call(
        _mods_body,
        grid=(depth, n // tn),
        in_specs=[pl.BlockSpec((8, d), lambda l, j: (0, 0)),
                  pl.BlockSpec((1, d, tn), lambda l, j: (l, 0, j)),
                  pl.BlockSpec((1, 1, tn), lambda l, j: (l, 0, j))],
        out_specs=pl.BlockSpec((1, 8, tn), lambda l, j: (l, 0, j)),
        out_shape=jax.ShapeDtypeStruct((depth, 8, n), F32),
        compiler_params=_cparams(("parallel", "parallel")),
        name="mods",
    )(cvec, ada_w, ada_b.reshape(depth, 1, n))


def _modmm_body(h_ref, sh_ref, sc_ref, w_ref, o_ref, u_scr, *, row_base, tiles_per_row):
    i = pl.program_id(0)
    j = pl.program_id(1)

    @pl.when(j == 0)
    def _():
        rid = row_base + i // tiles_per_row
        u_scr[...] = (h_ref[...] * (1.0 + _mod_row(sc_ref, rid)) + _mod_row(sh_ref, rid)).astype(BF16)

    o_ref[...] = jnp.dot(u_scr[...], w_ref[...], preferred_element_type=F32).astype(o_ref.dtype)


def _modmm_call(h, mods, layer, k_shift, k_scale, w, out_dtype, *, tm, tn, row_base, tiles_per_row, name):
    t, d = h.shape
    n = w.shape[1]
    body = functools.partial(_modmm_body, row_base=row_base, tiles_per_row=tiles_per_row)
    return pl.pallas_call(
        body,
        grid=(t // tm, n // tn),
        in_specs=[pl.BlockSpec((tm, d), lambda i, j: (i, 0)),
                  pl.BlockSpec((1, 8, d), lambda i, j: (layer, 0, k_shift)),
                  pl.BlockSpec((1, 8, d), lambda i, j: (layer, 0, k_scale)),
                  pl.BlockSpec((d, tn), lambda i, j: (0, j))],
        out_specs=pl.BlockSpec((tm, tn), lambda i, j: (i, j)),
        out_shape=jax.ShapeDtypeStruct((t, n), out_dtype),
        scratch_shapes=[pltpu.VMEM((tm, d), BF16)],
        compiler_params=_cparams(("parallel", "arbitrary")),
        name=name,
    )(h, mods, mods, w)


def _rpb_onehot():
    qc = np.arange(GRID_W)[:, None]
    kc = np.arange(GRID_W)[None, :]
    c0 = np.clip(qc - NA_COLS // 2, 0, GRID_W - NA_COLS)
    inside = (kc >= c0) & (kc < c0 + NA_COLS)
    dc = kc - qc + NA_COLS - 1
    oh = np.zeros((LANES, GRID_W, GRID_W), np.float32)
    for d in range(2 * NA_COLS - 1):
        oh[d] = ((dc == d) & inside).astype(np.float32)
    mask = np.where(inside, 0.0, NEG).astype(np.float32)
    return oh.reshape(LANES, GRID_W * GRID_W), mask.reshape(1, GRID_W * GRID_W)


def _rpb_body(r_ref, oh_ref, m_ref, o_ref):
    o_ref[...] = jnp.dot(r_ref[...], oh_ref[...], precision=HIGHEST,
                         preferred_element_type=F32) + m_ref[...]


def _rpb_table(rpb):
    h = rpb.shape[0]
    nd = 2 * NA_ROWS - 1
    oh, mask = _rpb_onehot()
    r2 = jnp.pad(rpb.reshape(h * nd, 2 * NA_COLS - 1), ((0, 0), (0, LANES - 2 * NA_COLS + 1)))
    e = pl.pallas_call(
        _rpb_body,
        out_shape=jax.ShapeDtypeStruct((h * nd, GRID_W * GRID_W), F32),
        name="rpb_table",
    )(r2, jnp.asarray(oh), jnp.asarray(mask))
    e = jnp.swapaxes(e.reshape(h, nd, GRID_W, GRID_W), 2, 3)
    e = jnp.pad(e, ((0, 0), (1 - NA_DMIN, NA_DMIN + NA_ND - nd), (0, 0), (0, 0)),
                constant_values=NEG)
    first, second = e[:, 1:], e[:, :-1]
    neg = jnp.full_like(first, NEG)
    return jnp.concatenate([jnp.concatenate([first, second], axis=-1),
                            jnp.concatenate([first, neg], axis=-1),
                            jnp.concatenate([neg, second], axis=-1),
                            jnp.concatenate([neg, neg], axis=-1)], axis=1)


def _natten_window(r_first, rows):
    return jnp.clip(r_first - NA_ROWS // 2, 0, rows - NA_WIN)


def _natten_scores(q, r_first, k_ref, kc_ref, e_ref, s_scr, rows):
    nk = NA_WIN * GRID_W
    ws = _natten_window(r_first, rows)
    kwin = k_ref[pl.ds(pl.multiple_of(ws * GRID_W, LANES), nk), :]
    kc = kc_ref[...]
    idx = []
    for g in range(NA_QR // 2):
        r = r_first + 2 * g
        r0a = jnp.clip(r - NA_ROWS // 2, 0, rows - NA_ROWS)
        r0b = jnp.clip(r + 1 - NA_ROWS // 2, 0, rows - NA_ROWS)
        col = []
        for t in range(NA_WIN):
            kr = ws + t
            d = kr - r + (NA_ROWS - 1)
            out_first = jnp.logical_or(kr < r0a, kr >= r0a + NA_ROWS).astype(jnp.int32)
            out_second = jnp.logical_or(kr < r0b, kr >= r0b + NA_ROWS).astype(jnp.int32)
            col.append((2 * out_first + out_second) * NA_ND + d - NA_DMIN)
        idx.append(col)
    lo = lax.broadcasted_iota(jnp.int32, (1, LANES), 1) < NA_HEAD_DIM
    scale = jnp.asarray(NA_HEAD_DIM ** -0.5, BF16)
    nt = (((1,), (1,)), ((), ()))
    for a in range(2):
        sel = lo if a == 0 else jnp.logical_not(lo)
        qa = jnp.where(sel, q, jnp.zeros_like(q)) * scale
        bias = jnp.concatenate(
            [jnp.concatenate([e_ref[a, idx[g][t]] for g in range(NA_QR // 2)], axis=1) for t in range(NA_WIN)],
            axis=0)
        s_scr[a, :nk, :] = lax.dot_general(kwin, qa, nt, preferred_element_type=F32) + bias
        s_scr[a, nk:, :] = lax.dot_general(kc, qa, nt, preferred_element_type=F32)


def _natten_softmax_pv(r_first, s_scr, p_scr, vt_ref, vct_ref, rows):
    nk = NA_WIN * GRID_W
    nkb = s_scr.shape[1] // GRID_W
    ws = _natten_window(r_first, rows)
    vt_win = vt_ref[:, pl.ds(pl.multiple_of(ws * GRID_W, LANES), nk)]
    vct = vct_ref[...]
    outs = []
    for a in range(2):
        linv = []
        for g in range(NA_QR // 2):
            cols = slice(g * LANES, (g + 1) * LANES)
            mx = None
            for t in range(nkb):
                blk = s_scr[a, t * GRID_W:(t + 1) * GRID_W, cols]
                mx = blk if mx is None else jnp.maximum(mx, blk)
            m = jnp.max(mx, axis=0, keepdims=True)
            ls = None
            for t in range(nkb):
                p = jnp.exp(s_scr[a, t * GRID_W:(t + 1) * GRID_W, cols] - m)
                p_scr[a, t * GRID_W:(t + 1) * GRID_W, cols] = p.astype(BF16)
                ls = p if ls is None else ls + p
            linv.append(1.0 / jnp.sum(ls, axis=0, keepdims=True))
        o_t = (jnp.dot(vt_win, p_scr[a, :nk, :], preferred_element_type=F32)
               + jnp.dot(vct, p_scr[a, nk:, :], preferred_element_type=F32))
        outs.append(o_t * jnp.concatenate(linv, axis=-1))
    o_t = jnp.concatenate([outs[0][:NA_HEAD_DIM], outs[1][NA_HEAD_DIM:]], axis=0)
    return o_t.T


def _natten_body(q0_ref, q_ref, qn_ref, k_ref, vt_ref, kc_ref, vct_ref, e_ref, o_ref,
                 sa_scr, sb_scr, pa_scr, pb_scr, *, rows, nrb):
    nq = NA_QR * GRID_W
    i = pl.program_id(2)
    b0 = 2 * i

    @pl.when(i == 0)
    def _():
        _natten_scores(q0_ref[...], 0, k_ref, kc_ref, e_ref, sa_scr, rows)

    _natten_scores(q_ref[nq:, :], (b0 + 1) * NA_QR, k_ref, kc_ref, e_ref, sb_scr, rows)
    o_ref[:nq, :] = _natten_softmax_pv(b0 * NA_QR, sa_scr, pa_scr, vt_ref, vct_ref, rows).astype(o_ref.dtype)
    nxt = jnp.minimum(b0 + 2, nrb - 1)
    _natten_scores(qn_ref[...], nxt * NA_QR, k_ref, kc_ref, e_ref, sa_scr, rows)
    o_ref[nq:, :] = _natten_softmax_pv((b0 + 1) * NA_QR, sb_scr, pb_scr, vt_ref, vct_ref, rows).astype(o_ref.dtype)


def _natten_call(proj_l, proj_c, e2, batch, seq, lc):
    d = e2.shape[0] * NA_HEAD_DIM
    npair = d // LANES
    rows = seq // GRID_W
    nrb = rows // NA_QR
    nq = NA_QR * GRID_W
    nkeys = NA_WIN * GRID_W + lc
    vt_l = jnp.swapaxes(proj_l[:, 2 * d:3 * d].reshape(batch, seq, d), 1, 2).reshape(batch * d, seq)
    vt_c = jnp.swapaxes(proj_c[:, 2 * d:3 * d].reshape(batch, lc, d), 1, 2).reshape(batch * d, lc)
    body = functools.partial(_natten_body, rows=rows, nrb=nrb)
    nst = nrb // 2
    return pl.pallas_call(
        body,
        grid=(batch, npair, nst),
        in_specs=[pl.BlockSpec((nq, LANES), lambda b, p, i: (b * nrb, p)),
                  pl.BlockSpec((2 * nq, LANES), lambda b, p, i: (b * nst + i, p)),
                  pl.BlockSpec((nq, LANES), lambda b, p, i: (b * nrb + jnp.minimum(2 * i + 2, nrb - 1), p)),
                  pl.BlockSpec((seq, LANES), lambda b, p, i: (b, npair + p)),
                  pl.BlockSpec((LANES, seq), lambda b, p, i: (b * npair + p, 0)),
                  pl.BlockSpec((lc, LANES), lambda b, p, i: (b, npair + p)),
                  pl.BlockSpec((LANES, lc), lambda b, p, i: (b * npair + p, 0)),
                  pl.BlockSpec((2, e2.shape[1], GRID_W, LANES), lambda b, p, i: (p, 0, 0, 0))],
        out_specs=pl.BlockSpec((2 * nq, LANES), lambda b, p, i: (b * nst + i, p)),
        out_shape=jax.ShapeDtypeStruct((batch * seq, d), BF16),
        scratch_shapes=[pltpu.VMEM((2, nkeys, nq), F32), pltpu.VMEM((2, nkeys, nq), F32),
                        pltpu.VMEM((2, nkeys, nq), BF16), pltpu.VMEM((2, nkeys, nq), BF16)],
        compiler_params=_cparams(("parallel", "parallel", "arbitrary")),
        name="natten",
    )(proj_l, proj_l, proj_l, proj_l, vt_l, proj_c, vt_c, e2)


def _ctxattn_body(q_ref, k_ref, v_ref, o_ref):
    q = q_ref[...]
    k = k_ref[...]
    v = v_ref[...]
    lane = lax.broadcasted_iota(jnp.int32, (1, LANES), 1)
    lo = lane < NA_HEAD_DIM
    scale = NA_HEAD_DIM ** -0.5
    outs = []
    for a in range(2):
        sel = lo if a == 0 else jnp.logical_not(lo)
        qa = jnp.where(sel, q, jnp.zeros_like(q)) * jnp.asarray(scale, BF16)
        s = lax.dot_general(qa, k, (((1,), (1,)), ((), ())), preferred_element_type=F32)
        m = jnp.max(s, axis=-1, keepdims=True)
        p = jnp.exp(s - m)
        l = jnp.sum(p, axis=-1, keepdims=True)
        outs.append(jnp.dot(p.astype(BF16), v, preferred_element_type=F32) / l)
    o_ref[...] = jnp.where(lo, outs[0], outs[1]).astype(o_ref.dtype)


def _ctxattn_call(proj_c, batch, lc, d):
    npair = d // LANES
    return pl.pallas_call(
        _ctxattn_body,
        grid=(batch, npair),
        in_specs=[pl.BlockSpec((lc, LANES), lambda b, p: (b, p)),
                  pl.BlockSpec((lc, LANES), lambda b, p: (b, npair + p)),
                  pl.BlockSpec((lc, LANES), lambda b, p: (b, 2 * npair + p))],
        out_specs=pl.BlockSpec((lc, LANES), lambda b, p: (b, p)),
        out_shape=jax.ShapeDtypeStruct((batch * lc, d), BF16),
        compiler_params=_cparams(("parallel", "parallel")),
        name="ctx_attn",
    )(proj_c, proj_c, proj_c)


def _dwconv_silu_body(xp_ref, x_ref, xn_ref, w_ref, b_ref, o_ref, *, tiles_per_seq):
    i = pl.program_id(0)
    k = w_ref.shape[0]
    half = k // 2
    tb = x_ref.shape[0]
    hr = xp_ref.shape[0]
    first = (i % tiles_per_seq) == 0
    last = (i % tiles_per_seq) == tiles_per_seq - 1
    prev = jnp.where(first, 0.0, xp_ref[...].astype(F32))
    nxt = jnp.where(last, 0.0, xn_ref[...].astype(F32))
    ext = jnp.concatenate([prev, x_ref[...].astype(F32), nxt], axis=0)
    acc = jnp.zeros((tb, x_ref.shape[1]), F32) + b_ref[...]
    for t in range(k):
        off = hr - half + t
        acc = acc + ext[off:off + tb, :] * w_ref[pl.ds(t, 1), :]
    o_ref[...] = _silu(acc).astype(o_ref.dtype)


def _dwconv_silu_call(proj, col0, width, w, b, seq, *, tb, tc):
    t = proj.shape[0]
    hr = BF16_SUBLANES
    nrt = t // tb
    cb0 = col0 // tc
    body = functools.partial(_dwconv_silu_body, tiles_per_seq=seq // tb)
    return pl.pallas_call(
        body,
        grid=(nrt, width // tc),
        in_specs=[pl.BlockSpec((hr, tc), lambda i, j: (jnp.maximum(i * (tb // hr) - 1, 0), cb0 + j)),
                  pl.BlockSpec((tb, tc), lambda i, j: (i, cb0 + j)),
                  pl.BlockSpec((hr, tc), lambda i, j: (jnp.minimum((i + 1) * (tb // hr), t // hr - 1), cb0 + j)),
                  pl.BlockSpec((w.shape[0], tc), lambda i, j: (0, j)),
                  pl.BlockSpec((1, tc), lambda i, j: (0, j))],
        out_specs=pl.BlockSpec((tb, tc), lambda i, j: (i, j)),
        out_shape=jax.ShapeDtypeStruct((t, width), BF16),
        compiler_params=_cparams(("parallel", "parallel")),
        name="dwconv_silu",
    )(proj, proj, proj, w, b.reshape(1, width))


def _ssd_body(x_ref, b_ref, c_ref, dt_ref, dtb_ref, alog_ref, h0_ref, y_ref, hT_ref, st_scr, *, nsteps):
    q = SSM_CHUNK
    dr = pl.program_id(1)
    s = pl.program_id(2)

    @pl.when(s == 0)
    def _():
        st_scr[...] = h0_ref[0, 0]

    fwd = dr == 0
    dtv = _softplus(dt_ref[...] + dtb_ref[0])
    la = dtv * (-jnp.exp(alog_ref[0]))
    ii = lax.broadcasted_iota(jnp.int32, (q, q), 0)
    jj = lax.broadcasted_iota(jnp.int32, (q, q), 1)
    tri = (jj - ii) * jnp.where(fwd, 1, -1) <= 0
    cum = jnp.dot(tri.astype(F32), la, precision=HIGHEST, preferred_element_type=F32)
    tot = jnp.sum(la, axis=0, keepdims=True)
    cum_t = cum.T
    lane = lax.broadcasted_iota(jnp.int32, (1, LANES), 1)
    lo = lane < SSM_HEAD_DIM
    heads_per_group = SSM_HEADS // SSM_GROUPS
    for g in range(SSM_GROUPS):
        bg = b_ref[:, g * SSM_STATE:(g + 1) * SSM_STATE]
        cg = c_ref[:, g * SSM_STATE:(g + 1) * SSM_STATE]
        cb = lax.dot_general(cg, bg, (((1,), (1,)), ((), ())), preferred_element_type=F32)
        bg_t = bg.astype(F32).T.astype(BF16)
        for hp in range(g * heads_per_group // 2, (g + 1) * heads_per_group // 2):
            ha, hb = 2 * hp, 2 * hp + 1
            xp = x_ref[:, hp * LANES:(hp + 1) * LANES].astype(F32)
            cca = jnp.broadcast_to(cum[:, ha:ha + 1], (q, q))
            ccb = jnp.broadcast_to(cum[:, hb:hb + 1], (q, q))
            l_a = jnp.exp(jnp.where(tri, cca - cum_t[ha:ha + 1, :], NEG))
            l_b = jnp.exp(jnp.where(tri, ccb - cum_t[hb:hb + 1, :], NEG))
            m_a = (cb * l_a).astype(BF16)
            m_b = (cb * l_b).astype(BF16)
            dtp = jnp.where(lo, jnp.broadcast_to(dtv[:, ha:ha + 1], (q, LANES)),
                            jnp.broadcast_to(dtv[:, hb:hb + 1], (q, LANES)))
            xdt = xp * dtp
            xdt_b = xdt.astype(BF16)
            y_intra = jnp.where(lo, jnp.dot(m_a, xdt_b, preferred_element_type=F32),
                                jnp.dot(m_b, xdt_b, preferred_element_type=F32))
            ccp = jnp.where(lo, cca, ccb)
            st = st_scr[hp]
            y_inter = jnp.dot(cg, st.astype(BF16), preferred_element_type=F32) * jnp.exp(ccp)
            y_ref[0, :, hp * LANES:(hp + 1) * LANES] = y_intra + y_inter
            totp = jnp.where(lo, jnp.broadcast_to(tot[:, ha:ha + 1], (1, LANES)),
                             jnp.broadcast_to(tot[:, hb:hb + 1], (1, LANES)))
            xw = (xdt * jnp.exp(totp - ccp)).astype(BF16)
            st_scr[hp] = st * jnp.exp(totp) + jnp.dot(bg_t, xw, preferred_element_type=F32)

    @pl.when(s == nsteps - 1)
    def _():
        hT_ref[0, 0] = st_scr[...]


def _ssd_call(xbc, dt, dtb, alog, h0, batch, seq):
    inner = SSM_HEADS * SSM_HEAD_DIM
    gn = SSM_GROUPS * SSM_STATE
    nc = seq // SSM_CHUNK
    npair = SSM_HEADS // 2

    def rblk(b, d, s):
        return b * nc + jnp.where(d == 0, s, nc - 1 - s)

    body = functools.partial(_ssd_body, nsteps=nc)
    return pl.pallas_call(
        body,
        grid=(batch, 2, nc),
        in_specs=[pl.BlockSpec((SSM_CHUNK, inner), lambda b, d, s: (rblk(b, d, s), 0)),
                  pl.BlockSpec((SSM_CHUNK, gn), lambda b, d, s: (rblk(b, d, s), inner // gn)),
                  pl.BlockSpec((SSM_CHUNK, gn), lambda b, d, s: (rblk(b, d, s), inner // gn + 1)),
                  pl.BlockSpec((SSM_CHUNK, LANES), lambda b, d, s: (rblk(b, d, s), d)),
                  pl.BlockSpec((1, 1, LANES), lambda b, d, s: (d, 0, 0)),
                  pl.BlockSpec((1, 1, LANES), lambda b, d, s: (d, 0, 0)),
                  pl.BlockSpec((1, 1, npair, SSM_STATE, LANES), lambda b, d, s: (b, d, 0, 0, 0))],
        out_specs=[pl.BlockSpec((1, SSM_CHUNK, inner), lambda b, d, s: (d, rblk(b, d, s), 0)),
                   pl.BlockSpec((1, 1, npair, SSM_STATE, LANES), lambda b, d, s: (b, d, 0, 0, 0))],
        out_shape=[jax.ShapeDtypeStruct((2, batch * seq, inner), F32),
                   jax.ShapeDtypeStruct((batch, 2, npair, SSM_STATE, LANES), F32)],
        scratch_shapes=[pltpu.VMEM((npair, SSM_STATE, LANES), F32)],
        compiler_params=_cparams(("parallel", "parallel", "arbitrary")),
        name="ssd",
    )(xbc, xbc, xbc, dt, dtb, alog, h0)


def _evenout_body(attn_ref, y_ref, xs_ref, z_ref, h_ref, gate_ref, dsk_ref, nw_ref, w_ref, lg_ref, lb_ref,
                  o_ref, ssm_scr, *, row_base, tiles_per_row, alpha):
    i = pl.program_id(0)
    rid = row_base + i // tiles_per_row
    inner = xs_ref.shape[1]
    gw = inner // SSM_GROUPS
    z = z_ref[...].astype(F32)
    y = y_ref[0] + y_ref[1] + xs_ref[...].astype(F32) * dsk_ref[...]
    u = y * _silu(z)
    for g in range(SSM_GROUPS):
        ug = u[:, g * gw:(g + 1) * gw]
        ms = jnp.mean(ug * ug, axis=-1, keepdims=True)
        ssm_scr[:, g * gw:(g + 1) * gw] = (ug * lax.rsqrt(ms + RMS_EPS) * nw_ref[:, g * gw:(g + 1) * gw]).astype(BF16)
    d_attn = attn_ref.shape[1]
    acc = jnp.dot(attn_ref[...], w_ref[:d_attn, :], preferred_element_type=F32)
    acc = acc + jnp.dot(ssm_scr[...], w_ref[d_attn:, :], preferred_element_type=F32)
    o_ref[...] = _ln_residual(h_ref[...], acc, _mod_row(gate_ref, rid), lg_ref[0], lb_ref[0], alpha)


def _evenout_call(attn, y2, xbc, proj, z_cb, h, mods, layer, dskip_row, norm_w, w_out, ln_g, ln_b, ln_idx,
                  *, tm, row_base, tiles_per_row, alpha):
    t, d = h.shape
    inner = y2.shape[2]
    body = functools.partial(_evenout_body, row_base=row_base, tiles_per_row=tiles_per_row, alpha=alpha)
    return pl.pallas_call(
        body,
        grid=(t // tm,),
        in_specs=[pl.BlockSpec((tm, d), lambda i: (i, 0)),
                  pl.BlockSpec((2, tm, inner), lambda i: (0, i, 0)),
                  pl.BlockSpec((tm, inner), lambda i: (i, 0)),
                  pl.BlockSpec((tm, inner), lambda i: (i, z_cb)),
                  pl.BlockSpec((tm, d), lambda i: (i, 0)),
                  pl.BlockSpec((1, 8, d), lambda i: (layer, 0, 2)),
                  pl.BlockSpec((1, inner), lambda i: (0, 0)),
                  pl.BlockSpec((1, inner), lambda i: (0, 0)),
                  pl.BlockSpec(w_out.shape, lambda i: (0, 0)),
                  pl.BlockSpec((1, 1, d), lambda i: (ln_idx, 0, 0)),
                  pl.BlockSpec((1, 1, d), lambda i: (ln_idx, 0, 0))],
        out_specs=pl.BlockSpec((tm, d), lambda i: (i, 0)),
        out_shape=jax.ShapeDtypeStruct((t, d), F32),
        scratch_shapes=[pltpu.VMEM((tm, inner), BF16)],
        compiler_params=_cparams(("parallel",)),
        name="even_out",
    )(attn, y2, xbc, proj, h, mods, dskip_row, norm_w, w_out, ln_g, ln_b)


def _mlp_chunks(u, w1_ref, w3_ref, w2_ref, acc_ref):
    nf = w1_ref.shape[0]

    def step(f, carry):
        a = jnp.dot(u, w1_ref[f], preferred_element_type=F32)
        b = jnp.dot(u, w3_ref[f], preferred_element_type=F32)
        t = (_silu(a) * b).astype(BF16)
        acc_ref[...] += jnp.dot(t, w2_ref[f], preferred_element_type=F32)
        return carry

    acc_ref[...] = jnp.zeros_like(acc_ref)
    lax.fori_loop(0, nf, step, 0)


def _ffn_body(h_ref, sh_ref, sc_ref, gate_ref, w1_ref, w3_ref, w2_ref, lg_ref, lb_ref, o_ref, acc_scr,
              *, row_base, tiles_per_row, alpha):
    i = pl.program_id(0)
    rid = row_base + i // tiles_per_row
    h = h_ref[...]
    u = (h * (1.0 + _mod_row(sc_ref, rid)) + _mod_row(sh_ref, rid)).astype(BF16)
    _mlp_chunks(u, w1_ref, w3_ref, w2_ref, acc_scr)
    o_ref[...] = _ln_residual(h, acc_scr[...], _mod_row(gate_ref, rid), lg_ref[0], lb_ref[0], alpha)


def _ffn_call(h, mods, layer, w1c, w3c, w2c, ln_g, ln_b, ln_idx, *, tm, row_base, tiles_per_row, alpha):
    t, d = h.shape
    body = functools.partial(_ffn_body, row_base=row_base, tiles_per_row=tiles_per_row, alpha=alpha)
    wspec = lambda w: pl.BlockSpec(w.shape, lambda i: (0, 0, 0), pipeline_mode=pl.Buffered(1))
    return pl.pallas_call(
        body,
        grid=(t // tm,),
        in_specs=[pl.BlockSpec((tm, d), lambda i: (i, 0)),
                  pl.BlockSpec((1, 8, d), lambda i: (layer, 0, 3)),
                  pl.BlockSpec((1, 8, d), lambda i: (layer, 0, 4)),
                  pl.BlockSpec((1, 8, d), lambda i: (layer, 0, 5)),
                  wspec(w1c), wspec(w3c), wspec(w2c),
                  pl.BlockSpec((1, 1, d), lambda i: (ln_idx, 0, 0)),
                  pl.BlockSpec((1, 1, d), lambda i: (ln_idx, 0, 0))],
        out_specs=pl.BlockSpec((tm, d), lambda i: (i, 0)),
        out_shape=jax.ShapeDtypeStruct((t, d), F32),
        scratch_shapes=[pltpu.VMEM((tm, d), F32)],
        compiler_params=_cparams(("parallel",)),
        name="ffn",
    )(h, mods, mods, mods, w1c, w3c, w2c, ln_g, ln_b)


def _scout_body(pb_ref, pcp_ref, pc_ref, pcn_ref, php_ref, ph_ref, phn_ref, h_ref, gate_ref, cw_ref, w_ref,
                lg_ref, lb_ref, o_ref, *, row_base, tiles_per_row, tiles_per_seq, alpha):
    i = pl.program_id(0)
    rid = row_base + i // tiles_per_row
    tm = ph_ref.shape[0]
    hr = php_ref.shape[0]
    first = (i % tiles_per_seq) == 0
    last = (i % tiles_per_seq) == tiles_per_seq - 1
    prev = jnp.where(first, 0.0, pcp_ref[...].astype(F32) * php_ref[...].astype(F32))
    cur = pc_ref[...].astype(F32) * ph_ref[...].astype(F32)
    nxt = jnp.where(last, 0.0, pcn_ref[...].astype(F32) * phn_ref[...].astype(F32))
    ext = jnp.concatenate([prev, cur, nxt], axis=0)
    acc = jnp.zeros_like(cur)
    for t in range(SC_CONV):
        off = hr - SC_CONV // 2 + t
        acc = acc + ext[off:off + tm, :] * cw_ref[pl.ds(t, 1), :]
    a = (pb_ref[...].astype(F32) * acc).astype(BF16)
    y = jnp.dot(a, w_ref[...], preferred_element_type=F32)
    o_ref[...] = _ln_residual(h_ref[...], y, _mod_row(gate_ref, rid), lg_ref[0], lb_ref[0], alpha)


def _scout_call(proj, h, mods, layer, conv_w, w_out, ln_g, ln_b, ln_idx, seq,
                *, tm, row_base, tiles_per_row, alpha):
    t, d = h.shape
    hr = BF16_SUBLANES
    r = tm // hr
    nh = t // hr
    body = functools.partial(_scout_body, row_base=row_base, tiles_per_row=tiles_per_row,
                             tiles_per_seq=seq // tm, alpha=alpha)
    prev_map = lambda cb: (lambda i: (jnp.maximum(i * r - 1, 0), cb))
    next_map = lambda cb: (lambda i: (jnp.minimum((i + 1) * r, nh - 1), cb))
    return pl.pallas_call(
        body,
        grid=(t // tm,),
        in_specs=[pl.BlockSpec((tm, d), lambda i: (i, 0)),
                  pl.BlockSpec((hr, d), prev_map(1)),
                  pl.BlockSpec((tm, d), lambda i: (i, 1)),
                  pl.BlockSpec((hr, d), next_map(1)),
                  pl.BlockSpec((hr, d), prev_map(2)),
                  pl.BlockSpec((tm, d), lambda i: (i, 2)),
                  pl.BlockSpec((hr, d), next_map(2)),
                  pl.BlockSpec((tm, d), lambda i: (i, 0)),
                  pl.BlockSpec((1, 8, d), lambda i: (layer, 0, 2)),
                  pl.BlockSpec(conv_w.shape, lambda i: (0, 0)),
                  pl.BlockSpec(w_out.shape, lambda i: (0, 0)),
                  pl.BlockSpec((1, 1, d), lambda i: (ln_idx, 0, 0)),
                  pl.BlockSpec((1, 1, d), lambda i: (ln_idx, 0, 0))],
        out_specs=pl.BlockSpec((tm, d), lambda i: (i, 0)),
        out_shape=jax.ShapeDtypeStruct((t, d), F32),
        compiler_params=_cparams(("parallel",)),
        name="shortconv_out",
    )(proj, proj, proj, proj, proj, proj, proj, h, mods, conv_w, w_out, ln_g, ln_b)


def _router_body(h_ref, sh_ref, sc_ref, wr_ref, u_ref, meta_ref, cnt_ref, cnt_scr, *, row_base, tiles_per_row):
    i = pl.program_id(0)
    rid = row_base + i // tiles_per_row
    tm = h_ref.shape[0]

    @pl.when(i == 0)
    def _():
        cnt_scr[...] = jnp.zeros_like(cnt_scr)

    u = h_ref[...] * (1.0 + _mod_row(sc_ref, rid)) + _mod_row(sh_ref, rid)
    u_ref[...] = u
    logits = jnp.dot(u, wr_ref[...], precision=HIGHEST, preferred_element_type=F32)
    lane = lax.broadcasted_iota(jnp.int32, (tm, LANES), 1).astype(F32)
    lg = jnp.where(lane < N_EXPERTS, logits, NEG)
    v1 = jnp.max(lg, axis=-1, keepdims=True)
    e1 = jnp.min(jnp.where(lg == v1, lane, float(LANES)), axis=-1, keepdims=True)
    lg2 = jnp.where(lane == e1, 2 * NEG, lg)
    v2 = jnp.max(lg2, axis=-1, keepdims=True)
    e2 = jnp.min(jnp.where(lg2 == v2, lane, float(LANES)), axis=-1, keepdims=True)
    g2 = 1.0 / (1.0 + jnp.exp(v1 - v2))
    g1 = 1.0 - g2
    oh1 = (lane == e1).astype(F32)
    oh2 = (lane == e2).astype(F32)
    both = (oh1 + oh2).astype(BF16)
    ii = lax.broadcasted_iota(jnp.int32, (tm, tm), 0)
    jj = lax.broadcasted_iota(jnp.int32, (tm, tm), 1)
    strict = (jj < ii).astype(BF16)
    before = jnp.dot(strict, both, preferred_element_type=F32) + cnt_scr[...]
    p1 = jnp.sum(before * oh1, axis=-1, keepdims=True)
    p2 = jnp.sum(before * oh2, axis=-1, keepdims=True)
    cnt_scr[...] = cnt_scr[...] + jnp.sum(oh1 + oh2, axis=0, keepdims=True)
    meta = jnp.where(lane == 0, e1,
           jnp.where(lane == 1, e2,
           jnp.where(lane == 2, p1,
           jnp.where(lane == 3, p2,
           jnp.where(lane == 4, g1,
           jnp.where(lane == 5, g2, 0.0))))))
    meta_ref[...] = meta
    cnt_ref[...] = jnp.broadcast_to(cnt_scr[...], cnt_ref.shape)


def _router_call(h, mods, layer, wr_pad, *, tm, row_base, tiles_per_row):
    t, d = h.shape
    body = functools.partial(_router_body, row_base=row_base, tiles_per_row=tiles_per_row)
    return pl.pallas_call(
        body,
        grid=(t // tm,),
        in_specs=[pl.BlockSpec((tm, d), lambda i: (i, 0)),
                  pl.BlockSpec((1, 8, d), lambda i: (layer, 0, 3)),
                  pl.BlockSpec((1, 8, d), lambda i: (layer, 0, 4)),
                  pl.BlockSpec(wr_pad.shape, lambda i: (0, 0))],
        out_specs=[pl.BlockSpec((tm, d), lambda i: (i, 0)),
                   pl.BlockSpec((tm, LANES), lambda i: (i, 0)),
                   pl.BlockSpec((8, LANES), lambda i: (0, 0))],
        out_shape=[jax.ShapeDtypeStruct((t, d), F32),
                   jax.ShapeDtypeStruct((t, LANES), F32),
                   jax.ShapeDtypeStruct((8, LANES), F32)],
        scratch_shapes=[pltpu.VMEM((1, LANES), F32)],
        compiler_params=_cparams(("arbitrary",)),
        name="moe_router",
    )(h, mods, mods, wr_pad)


def _rowcopy_body(tv_ref, idx_ref, src_hbm, dst_hbm, zbuf, sem, zsem, *, fill_tile):
    i = pl.program_id(0)
    ch = idx_ref.shape[2]

    if fill_tile:
        @pl.when(i == 0)
        def _():
            zbuf[...] = jnp.zeros_like(zbuf)

            def fill_copy(t):
                return pltpu.make_async_copy(zbuf, dst_hbm.at[pl.ds(t * fill_tile, fill_tile)], zsem.at[0])

            def fill(t, c):
                @pl.when(tv_ref[t] < fill_tile)
                def _():
                    fill_copy(t).start()
                return c

            def fill_wait(t, c):
                @pl.when(tv_ref[t] < fill_tile)
                def _():
                    fill_copy(t).wait()
                return c

            lax.fori_loop(0, tv_ref.shape[0], fill, 0)
            lax.fori_loop(0, tv_ref.shape[0], fill_wait, 0)

    def row_copy(n):
        return pltpu.make_async_copy(src_hbm.at[pl.ds(idx_ref[0, 0, n], 1)],
                                     dst_hbm.at[pl.ds(idx_ref[0, 1, n], 1)], sem.at[0])

    def issue(n, c):
        row_copy(n).start()
        return c

    def drain(n, c):
        row_copy(n).wait()
        return c

    lax.fori_loop(0, ch, issue, 0, unroll=8)
    lax.fori_loop(0, ch, drain, 0, unroll=8)


def _rowcopy_call(src, src_idx, dst_idx, n_dst, tile_valid, fill_tile, name):
    n = src_idx.shape[0]
    d = src.shape[1]
    ch = ROWCOPY_CHUNK
    idx = jnp.stack([src_idx.reshape(n // ch, ch), dst_idx.reshape(n // ch, ch)], axis=1)
    grid_spec = pltpu.PrefetchScalarGridSpec(
        num_scalar_prefetch=1,
        grid=(n // ch,),
        in_specs=[pl.BlockSpec((1, 2, ch), lambda i, tv: (i, 0, 0), memory_space=pltpu.SMEM),
                  pl.BlockSpec(memory_space=pl.ANY)],
        out_specs=pl.BlockSpec(memory_space=pl.ANY),
        scratch_shapes=[pltpu.VMEM((max(fill_tile, 8), d), F32),
                        pltpu.SemaphoreType.DMA((1,)),
                        pltpu.SemaphoreType.DMA((1,))],
    )
    return pl.pallas_call(
        functools.partial(_rowcopy_body, fill_tile=fill_tile),
        grid_spec=grid_spec,
        out_shape=jax.ShapeDtypeStruct((n_dst, d), F32),
        compiler_params=_cparams(("arbitrary",)),
        name=name,
    )(tile_valid, idx, src)


def _moe_body(te_ref, x_ref, w1_hbm, w3_hbm, w2_hbm, o_ref, w1s, w3s, w2s, acc_scr, wsem):
    t = pl.program_id(0)
    e = te_ref[t]
    e_prev = te_ref[jnp.maximum(t - 1, 0)]

    @pl.when(jnp.logical_or(t == 0, e != e_prev))
    def _():
        nf, _, fc = w1s.shape
        copies = [pltpu.make_async_copy(w2_hbm.at[e], w2s, wsem.at[2])]
        for f in range(nf):
            copies.append(pltpu.make_async_copy(w1_hbm.at[e, :, pl.ds(f * fc, fc)], w1s.at[f], wsem.at[0]))
            copies.append(pltpu.make_async_copy(w3_hbm.at[e, :, pl.ds(f * fc, fc)], w3s.at[f], wsem.at[1]))
        for cp in copies:
            cp.start()
        for cp in copies:
            cp.wait()

    _mlp_chunks(x_ref[...].astype(BF16), w1s, w3s, w2s, acc_scr)
    o_ref[...] = acc_scr[...]


def _moe_call(rows_in, tile_expert, w1, w3, w2c):
    n_rows, d = rows_in.shape
    tr = MOE_TILE
    w1c = jax.ShapeDtypeStruct((w1.shape[0], w1.shape[2] // MOE_CHUNK, d, MOE_CHUNK), BF16)
    w3c = w1c
    grid_spec = pltpu.PrefetchScalarGridSpec(
        num_scalar_prefetch=1,
        grid=(n_rows // tr,),
        in_specs=[pl.BlockSpec((tr, d), lambda i, te: (i, 0)),
                  pl.BlockSpec(memory_space=pl.ANY),
                  pl.BlockSpec(memory_space=pl.ANY),
                  pl.BlockSpec(memory_space=pl.ANY)],
        out_specs=pl.BlockSpec((tr, d), lambda i, te: (i, 0)),
        scratch_shapes=[pltpu.VMEM(w1c.shape[1:], BF16),
                        pltpu.VMEM(w3c.shape[1:], BF16),
                        pltpu.VMEM(w2c.shape[1:], BF16),
                        pltpu.VMEM((tr, d), F32),
                        pltpu.SemaphoreType.DMA((3,))],
    )
    return pl.pallas_call(
        _moe_body,
        grid_spec=grid_spec,
        out_shape=jax.ShapeDtypeStruct((n_rows, d), F32),
        compiler_params=_cparams(("arbitrary",)),
        name="moe_experts",
    )(tile_expert, rows_in, w1, w3, w2c)


def _combine_body(y_ref, meta_ref, h_ref, gate_ref, lg_ref, lb_ref, o_ref, *, row_base, tiles_per_row, alpha):
    i = pl.program_id(0)
    rid = row_base + i // tiles_per_row
    meta = meta_ref[...]
    y = meta[:, 4:5] * y_ref[0] + meta[:, 5:6] * y_ref[1]
    o_ref[...] = _ln_residual(h_ref[...], y, _mod_row(gate_ref, rid), lg_ref[0], lb_ref[0], alpha)


def _combine_call(y2, meta, h, mods, layer, ln_g, ln_b, ln_idx, *, tm, row_base, tiles_per_row, alpha):
    t, d = h.shape
    body = functools.partial(_combine_body, row_base=row_base, tiles_per_row=tiles_per_row, alpha=alpha)
    return pl.pallas_call(
        body,
        grid=(t // tm,),
        in_specs=[pl.BlockSpec((TOP_K, tm, d), lambda i: (0, i, 0)),
                  pl.BlockSpec((tm, LANES), lambda i: (i, 0)),
                  pl.BlockSpec((tm, d), lambda i: (i, 0)),
                  pl.BlockSpec((1, 8, d), lambda i: (layer, 0, 5)),
                  pl.BlockSpec((1, 1, d), lambda i: (ln_idx, 0, 0)),
                  pl.BlockSpec((1, 1, d), lambda i: (ln_idx, 0, 0))],
        out_specs=pl.BlockSpec((tm, d), lambda i: (i, 0)),
        out_shape=jax.ShapeDtypeStruct((t, d), F32),
        compiler_params=_cparams(("parallel",)),
        name="moe_combine",
    )(y2, meta, h, mods, ln_g, ln_b)


def _chunk_cols(w, fc):
    k, f = w.shape[-2:]
    lead = w.shape[:-2]
    w = w.astype(BF16).reshape(lead + (k, f // fc, fc))
    return jnp.swapaxes(w, -3, -2)


def _chunk_rows(w, fc):
    f, d = w.shape[-2:]
    return w.astype(BF16).reshape(w.shape[:-2] + (f // fc, fc, d))


def _moe_layer(h, mods, layer, wr_pad, w1c, w3c, w2c, ln_g, ln_b, ln_idx, *, tm, row_base, tiles_per_row, alpha):
    t, d = h.shape
    u, meta, cnt = _router_call(h, mods, layer, wr_pad, tm=tm, row_base=row_base, tiles_per_row=tiles_per_row)
    n_assign = t * TOP_K
    n_tiles = n_assign // MOE_TILE + N_EXPERTS
    counts = cnt[0, :N_EXPERTS].astype(jnp.int32)
    padded = (counts + MOE_TILE - 1) // MOE_TILE * MOE_TILE
    pad_ends = jnp.cumsum(padded)
    pad_starts = pad_ends - padded
    top_e = meta[:, 0:TOP_K].astype(jnp.int32)
    rank = meta[:, 2:2 + TOP_K].astype(jnp.int32)
    dest = (pad_starts[top_e] + rank).reshape(-1)
    tile_start = jnp.arange(n_tiles, dtype=jnp.int32) * MOE_TILE
    tile_expert = jnp.minimum(jnp.searchsorted(pad_ends, tile_start, side='right'), N_EXPERTS - 1).astype(jnp.int32)
    seg_end = jnp.where(tile_start < pad_ends[-1], (pad_starts + counts)[tile_expert], 0)
    tile_valid = jnp.clip(seg_end - tile_start, 0, MOE_TILE).astype(jnp.int32)
    tok = jnp.repeat(jnp.arange(t, dtype=jnp.int32), TOP_K)
    slot = tok + jnp.tile(jnp.arange(TOP_K, dtype=jnp.int32) * t, t)
    rows_in = _rowcopy_call(u, tok, dest, n_tiles * MOE_TILE, tile_valid, MOE_TILE, "moe_dispatch")
    rows_out = _moe_call(rows_in, tile_expert, w1c, w3c, w2c)
    y2 = _rowcopy_call(rows_out, dest, slot, TOP_K * t, tile_valid, 0, "moe_undispatch")
    return _combine_call(y2.reshape(TOP_K, t, d), meta, h, mods, layer, ln_g, ln_b, ln_idx,
                         tm=tm, row_base=row_base, tiles_per_row=tiles_per_row, alpha=alpha)


def kernel(x, c, ctx, c_ctx, ada_w, ada_b, ln_g, ln_b, even_w_in, na_rpb, ssm_conv_w, ssm_conv_b, ssm_dt_bias,
           ssm_a_log, ssm_d, ssm_norm_w, even_w_out, ffn_w1, ffn_w3, ffn_w2, sc_w_in, sc_conv_w, sc_w_out,
           moe_router, moe_w1, moe_w3, moe_w2):
    batch, seq, d = x.shape
    lc = ctx.shape[1]
    depth = ada_w.shape[0]
    alpha = (2 * depth) ** 0.25
    inner = SSM_HEADS * SSM_HEAD_DIM
    gn = SSM_GROUPS * SSM_STATE
    n_main = 3 * d + inner + inner + 2 * gn
    assert batch + 1 <= 8

    cvec = jnp.zeros((8, d), F32).at[:batch].set(c).at[batch].set(c_ctx)
    mods = _mods_call(cvec, ada_w, ada_b)
    lng = ln_g.reshape(depth * 2, 1, d)
    lnb = ln_b.reshape(depth * 2, 1, d)

    h_lat = x.reshape(batch * seq, d)
    h_ctx = ctx.reshape(batch * lc, d)
    lat = dict(tm=TM_MLP, row_base=0, tiles_per_row=seq // TM_MLP, alpha=alpha)
    cx = dict(tm=lc, row_base=batch, tiles_per_row=1 << 20, alpha=alpha)

    for i in range(depth):
        j = i // 2
        ctx_live = any(m % 2 == 0 for m in range(i + 1, depth))
        if i % 2 == 0:
            w_in = even_w_in[j]
            w_main = w_in[:, :n_main].astype(BF16)
            w_dt = jnp.zeros((d, 2 * LANES), F32)
            w_dt = w_dt.at[:, :SSM_HEADS].set(w_in[:, n_main:n_main + SSM_HEADS])
            w_dt = w_dt.at[:, LANES:LANES + SSM_HEADS].set(w_in[:, n_main + SSM_HEADS:]).astype(BF16)
            pad16 = ((0, 0), (0, 0), (0, LANES - SSM_HEADS))
            dtb = jnp.pad(ssm_dt_bias[j][:, None, :], pad16)
            alog = jnp.pad(ssm_a_log[j][:, None, :], pad16)
            dskip = jnp.repeat(ssm_d[j], SSM_HEAD_DIM)[None, :]
            norm_w = ssm_norm_w[j][None, :]
            w_out = even_w_out[j].astype(BF16)
            e2 = _rpb_table(na_rpb[j])

            def in_proj(h, tm, row_base, tpr, tag):
                kw = dict(tm=tm, row_base=row_base, tiles_per_row=tpr)
                pm = _modmm_call(h, mods, i, 0, 1, w_main, BF16, tn=TN_PROJ, name="even_in_" + tag, **kw)
                pdt = _modmm_call(h, mods, i, 0, 1, w_dt, F32, tn=LANES, name="even_dt_" + tag, **kw)
                return pm, pdt

            proj_l, dt_l = in_proj(h_lat, TM_LAT, 0, seq // TM_LAT, "lat")
            proj_c, dt_c = in_proj(h_ctx, lc, batch, 1 << 20, "ctx")

            attn_l = _natten_call(proj_l, proj_c, e2, batch, seq, lc)
            xbc_l = _dwconv_silu_call(proj_l, 3 * d + inner, inner + 2 * gn, ssm_conv_w[j], ssm_conv_b[j], seq,
                                      tb=512, tc=512)
            xbc_c = _dwconv_silu_call(proj_c, 3 * d + inner, inner + 2 * gn, ssm_conv_w[j], ssm_conv_b[j], lc,
                                      tb=lc, tc=512)
            h0 = jnp.zeros((batch, 2, SSM_HEADS // 2, SSM_STATE, LANES), F32)
            y_c, h_c = _ssd_call(xbc_c, dt_c, dtb, alog, h0, batch, lc)
            y_l, _ = _ssd_call(xbc_l, dt_l, dtb, alog, h_c, batch, seq)

            z_cb = 3 * d // inner
            h_lat = _evenout_call(attn_l, y_l, xbc_l, proj_l, z_cb, h_lat, mods, i, dskip, norm_w, w_out,
                                  lng, lnb, 2 * i, **lat)
            w1c = _chunk_cols(ffn_w1[j], FFN_CHUNK)
            w3c = _chunk_cols(ffn_w3[j], FFN_CHUNK)
            w2c = _chunk_rows(ffn_w2[j], FFN_CHUNK)
            h_lat = _ffn_call(h_lat, mods, i, w1c, w3c, w2c, lng, lnb, 2 * i + 1, **lat)
            if ctx_live:
                attn_c = _ctxattn_call(proj_c, batch, lc, d)
                h_ctx = _evenout_call(attn_c, y_c, xbc_c, proj_c, z_cb, h_ctx, mods, i, dskip, norm_w, w_out,
                                      lng, lnb, 2 * i, **cx)
                h_ctx = _ffn_call(h_ctx, mods, i, w1c, w3c, w2c, lng, lnb, 2 * i + 1, **cx)
        else:
            w_in = sc_w_in[j].astype(BF16)
            w_out = sc_w_out[j].astype(BF16)
            wr_pad = jnp.pad(moe_router[j], ((0, 0), (0, LANES - N_EXPERTS)))
            w1c = moe_w1[j].astype(BF16)
            w3c = moe_w3[j].astype(BF16)
            w2c = _chunk_rows(moe_w2[j], MOE_CHUNK)
            proj_l = _modmm_call(h_lat, mods, i, 0, 1, w_in, BF16, tm=TM_LAT, tn=TN_PROJ, row_base=0,
                                 tiles_per_row=seq // TM_LAT, name="odd_in_lat")
            h_lat = _scout_call(proj_l, h_lat, mods, i, sc_conv_w[j], w_out, lng, lnb, 2 * i, seq, **lat)
            h_lat = _moe_layer(h_lat, mods, i, wr_pad, w1c, w3c, w2c, lng, lnb, 2 * i + 1, **lat)
            if ctx_live:
                proj_c = _modmm_call(h_ctx, mods, i, 0, 1, w_in, BF16, tm=lc, tn=TN_PROJ, row_base=batch,
                                     tiles_per_row=1 << 20, name="odd_in_ctx")
                h_ctx = _scout_call(proj_c, h_ctx, mods, i, sc_conv_w[j], w_out, lng, lnb, 2 * i, lc, **cx)
                h_ctx = _moe_layer(h_ctx, mods, i, wr_pad, w1c, w3c, w2c, lng, lnb, 2 * i + 1, **cx)
    return h_lat.reshape(batch, seq, d)
```

```python
import functools

import numpy as np
import jax
import jax.numpy as jnp
from jax import lax
from jax.experimental import pallas as pl
from jax.experimental.pallas import tpu as pltpu

F32 = jnp.float32
BF16 = jnp.bfloat16
HIGHEST = lax.Precision.HIGHEST

GRID_W = 64
NA_HEAD_DIM = 64
NA_ROWS = 8
NA_COLS = 16
SSM_HEAD_DIM = 64
SSM_HEADS = 16
SSM_GROUPS = 4
SSM_STATE = 128
SSM_CONV = 5
SSM_CHUNK = 128
SC_CONV = 3
N_EXPERTS = 8
TOP_K = 2
LN_EPS = 1e-5
RMS_EPS = 1e-5

LANES = 128
BF16_SUBLANES = 16
NEG = -1e30
VMEM_LIMIT = 56 * 1024 * 1024

TM_LAT = 1024
TN_PROJ = 1024
TM_MLP = 512
NA_QR = 4
NA_WIN = NA_QR + NA_ROWS
NA_DMIN = 1 - NA_QR
NA_ND = 2 * (NA_QR + NA_ROWS - 1)
MOE_TILE = 512
ROWCOPY_CHUNK = 512
FFN_CHUNK = 256
MOE_CHUNK = 512


def _cparams(sem, vmem=None):
    return pltpu.CompilerParams(dimension_semantics=sem, vmem_limit_bytes=vmem or VMEM_LIMIT)


def _silu(x):
    return x * jax.nn.sigmoid(x)


def _softplus(x):
    return jnp.maximum(x, 0.0) + jnp.log(1.0 + jnp.exp(-jnp.abs(x)))


def _ln_residual(h, y, gate, g, b, alpha):
    v = alpha * h + gate * y
    mu = jnp.mean(v, axis=-1, keepdims=True)
    d = v - mu
    var = jnp.mean(d * d, axis=-1, keepdims=True)
    return d * lax.rsqrt(var + LN_EPS) * g + b


def _mod_row(ref, rid):
    return ref[0, pl.ds(rid, 1), :]


def _mods_body(c_ref, w_ref, b_ref, o_ref):
    s = _silu(c_ref[...])
    o_ref[0] = jnp.dot(s, w_ref[0], precision=HIGHEST, preferred_element_type=F32) + b_ref[0]


def _mods_call(cvec, ada_w, ada_b):
    depth, d, n = ada_w.shape
    tn = 1024
    return pl.pallas_call(
        _mods_body,
        grid=(depth, n // tn),
        in_specs=[pl.BlockSpec((8, d), lambda l, j: (0, 0)),
                  pl.BlockSpec((1, d, tn), lambda l, j: (l, 0, j)),
                  pl.BlockSpec((1, 1, tn), lambda l, j: (l, 0, j))],
        out_specs=pl.BlockSpec((1, 8, tn), lambda l, j: (l, 0, j)),
        out_shape=jax.ShapeDtypeStruct((depth, 8, n), F32),
        compiler_params=_cparams(("parallel", "parallel")),
        name="mods",
    )(cvec, ada_w, ada_b.reshape(depth, 1, n))


def _modmm_body(h_ref, sh_ref, sc_ref, w_ref, o_ref, u_scr, *, row_base, tiles_per_row):
    i = pl.program_id(0)
    j = pl.program_id(1)

    @pl.when(j == 0)
    def _():
        rid = row_base + i // tiles_per_row
        u_scr[...] = (h_ref[...] * (1.0 + _mod_row(sc_ref, rid)) + _mod_row(sh_ref, rid)).astype(BF16)

    o_ref[...] = jnp.dot(u_scr[...], w_ref[...], preferred_element_type=F32).astype(o_ref.dtype)


def _modmm_call(h, mods, layer, k_shift, k_scale, w, out_dtype, *, tm, tn, row_base, tiles_per_row, name):
    t, d = h.shape
    n = w.shape[1]
    body = functools.partial(_modmm_body, row_base=row_base, tiles_per_row=tiles_per_row)
    return pl.pallas_call(
        body,
        grid=(t // tm, n // tn),
        in_specs=[pl.BlockSpec((tm, d), lambda i, j: (i, 0)),
                  pl.BlockSpec((1, 8, d), lambda i, j: (layer, 0, k_shift)),
                  pl.BlockSpec((1, 8, d), lambda i, j: (layer, 0, k_scale)),
                  pl.BlockSpec((d, tn), lambda i, j: (0, j))],
        out_specs=pl.BlockSpec((tm, tn), lambda i, j: (i, j)),
        out_shape=jax.ShapeDtypeStruct((t, n), out_dtype),
        scratch_shapes=[pltpu.VMEM((tm, d), BF16)],
        compiler_params=_cparams(("parallel", "arbitrary")),
        name=name,
    )(h, mods, mods, w)


def _rpb_onehot():
    qc = np.arange(GRID_W)[:, None]
    kc = np.arange(GRID_W)[None, :]
    c0 = np.clip(qc - NA_COLS // 2, 0, GRID_W - NA_COLS)
    inside = (kc >= c0) & (kc < c0 + NA_COLS)
    dc = kc - qc + NA_COLS - 1
    oh = np.zeros((LANES, GRID_W, GRID_W), np.float32)
    for d in range(2 * NA_COLS - 1):
        oh[d] = ((dc == d) & inside).astype(np.float32)
    mask = np.where(inside, 0.0, NEG).astype(np.float32)
    return oh.reshape(LANES, GRID_W * GRID_W), mask.reshape(1, GRID_W * GRID_W)


def _rpb_body(r_ref, oh_ref, m_ref, o_ref):
    o_ref[...] = jnp.dot(r_ref[...], oh_ref[...], precision=HIGHEST,
                         preferred_element_type=F32) + m_ref[...]


def _rpb_table(rpb):
    h = rpb.shape[0]
    nd = 2 * NA_ROWS - 1
    oh, mask = _rpb_onehot()
    r2 = jnp.pad(rpb.reshape(h * nd, 2 * NA_COLS - 1), ((0, 0), (0, LANES - 2 * NA_COLS + 1)))
    e = pl.pallas_call(
        _rpb_body,
        out_shape=jax.ShapeDtypeStruct((h * nd, GRID_W * GRID_W), F32),
        name="rpb_table",
    )(r2, jnp.asarray(oh), jnp.asarray(mask))
    e = jnp.swapaxes(e.reshape(h, nd, GRID_W, GRID_W), 2, 3)
    e = jnp.pad(e, ((0, 0), (1 - NA_DMIN, NA_DMIN + NA_ND - nd), (0, 0), (0, 0)),
                constant_values=NEG)
    first, second = e[:, 1:], e[:, :-1]
    neg = jnp.full_like(first, NEG)
    return jnp.concatenate([jnp.concatenate([first, second], axis=-1),
                            jnp.concatenate([first, neg], axis=-1),
                            jnp.concatenate([neg, second], axis=-1),
                            jnp.concatenate([neg, neg], axis=-1)], axis=1)


def _natten_window(r_first, rows):
    return jnp.clip(r_first - NA_ROWS // 2, 0, rows - NA_WIN)


def _natten_scores(q, r_first, k_ref, kc_ref, e_ref, s_scr, rows):
    nk = NA_WIN * GRID_W
    ws = _natten_window(r_first, rows)
    kwin = k_ref[pl.ds(pl.multiple_of(ws * GRID_W, LANES), nk), :]
    kc = kc_ref[...]
    idx = []
    for g in range(NA_QR // 2):
        r = r_first + 2 * g
        r0a = jnp.clip(r - NA_ROWS // 2, 0, rows - NA_ROWS)
        r0b = jnp.clip(r + 1 - NA_ROWS // 2, 0, rows - NA_ROWS)
        col = []
        for t in range(NA_WIN):
            kr = ws + t
            d = kr - r + (NA_ROWS - 1)
            out_first = jnp.logical_or(kr < r0a, kr >= r0a + NA_ROWS).astype(jnp.int32)
            out_second = jnp.logical_or(kr < r0b, kr >= r0b + NA_ROWS).astype(jnp.int32)
            col.append((2 * out_first + out_second) * NA_ND + d - NA_DMIN)
        idx.append(col)
    lo = lax.broadcasted_iota(jnp.int32, (1, LANES), 1) < NA_HEAD_DIM
    scale = jnp.asarray(NA_HEAD_DIM ** -0.5, BF16)
    nt = (((1,), (1,)), ((), ()))
    for a in range(2):
        sel = lo if a == 0 else jnp.logical_not(lo)
        qa = jnp.where(sel, q, jnp.zeros_like(q)) * scale
        bias = jnp.concatenate(
            [jnp.concatenate([e_ref[a, idx[g][t]] for g in range(NA_QR // 2)], axis=1) for t in range(NA_WIN)],
            axis=0)
        s_scr[a, :nk, :] = lax.dot_general(kwin, qa, nt, preferred_element_type=F32) + bias
        s_scr[a, nk:, :] = lax.dot_general(kc, qa, nt, preferred_element_type=F32)


def _natten_softmax_pv(r_first, s_scr, p_scr, vt_ref, vct_ref, rows):
    nk = NA_WIN * GRID_W
    nkb = s_scr.shape[1] // GRID_W
    ws = _natten_window(r_first, rows)
    vt_win = vt_ref[:, pl.ds(pl.multiple_of(ws * GRID_W, LANES), nk)]
    vct = vct_ref[...]
    outs = []
    for a in range(2):
        linv = []
        for g in range(NA_QR // 2):
            cols = slice(g * LANES, (g + 1) * LANES)
            mx = None
            for t in range(nkb):
                blk = s_scr[a, t * GRID_W:(t + 1) * GRID_W, cols]
                mx = blk if mx is None else jnp.maximum(mx, blk)
            m = jnp.max(mx, axis=0, keepdims=True)
            ls = None
            for t in range(nkb):
                p = jnp.exp(s_scr[a, t * GRID_W:(t + 1) * GRID_W, cols] - m)
                p_scr[a, t * GRID_W:(t + 1) * GRID_W, cols] = p.astype(BF16)
                ls = p if ls is None else ls + p
            linv.append(1.0 / jnp.sum(ls, axis=0, keepdims=True))
        o_t = (jnp.dot(vt_win, p_scr[a, :nk, :], preferred_element_type=F32)
               + jnp.dot(vct, p_scr[a, nk:, :], preferred_element_type=F32))
        outs.append(o_t * jnp.concatenate(linv, axis=-1))
    o_t = jnp.concatenate([outs[0][:NA_HEAD_DIM], outs[1][NA_HEAD_DIM:]], axis=0)
    return o_t.T


def _natten_body(q0_ref, q_ref, qn_ref, k_ref, vt_ref, kc_ref, vct_ref, e_ref, o_ref,
                 sa_scr, sb_scr, pa_scr, pb_scr, *, rows, nrb):
    nq = NA_QR * GRID_W
    i = pl.program_id(2)
    b0 = 2 * i

    @pl.when(i == 0)
    def _():
        _natten_scores(q0_ref[...], 0, k_ref, kc_ref, e_ref, sa_scr, rows)

    _natten_scores(q_ref[nq:, :], (b0 + 1) * NA_QR, k_ref, kc_ref, e_ref, sb_scr, rows)
    o_ref[:nq, :] = _natten_softmax_pv(b0 * NA_QR, sa_scr, pa_scr, vt_ref, vct_ref, rows).astype(o_ref.dtype)
    nxt = jnp.minimum(b0 + 2, nrb - 1)
    _natten_scores(qn_ref[...], nxt * NA_QR, k_ref, kc_ref, e_ref, sa_scr, rows)
    o_ref[nq:, :] = _natten_softmax_pv((b0 + 1) * NA_QR, sb_scr, pb_scr, vt_ref, vct_ref, rows).astype(o_ref.dtype)


def _natten_call(proj_l, proj_c, e2, batch, seq, lc):
    d = e2.shape[0] * NA_HEAD_DIM
    npair = d // LANES
    rows = seq // GRID_W
    nrb = rows // NA_QR
    nq = NA_QR * GRID_W
    nkeys = NA_WIN * GRID_W + lc
    vt_l = jnp.swapaxes(proj_l[:, 2 * d:3 * d].reshape(batch, seq, d), 1, 2).reshape(batch * d, seq)
    vt_c = jnp.swapaxes(proj_c[:, 2 * d:3 * d].reshape(batch, lc, d), 1, 2).reshape(batch * d, lc)
    body = functools.partial(_natten_body, rows=rows, nrb=nrb)
    nst = nrb // 2
    return pl.pallas_call(
        body,
        grid=(batch, npair, nst),
        in_specs=[pl.BlockSpec((nq, LANES), lambda b, p, i: (b * nrb, p)),
                  pl.BlockSpec((2 * nq, LANES), lambda b, p, i: (b * nst + i, p)),
                  pl.BlockSpec((nq, LANES), lambda b, p, i: (b * nrb + jnp.minimum(2 * i + 2, nrb - 1), p)),
                  pl.BlockSpec((seq, LANES), lambda b, p, i: (b, npair + p)),
                  pl.BlockSpec((LANES, seq), lambda b, p, i: (b * npair + p, 0)),
                  pl.BlockSpec((lc, LANES), lambda b, p, i: (b, npair + p)),
                  pl.BlockSpec((LANES, lc), lambda b, p, i: (b * npair + p, 0)),
                  pl.BlockSpec((2, e2.shape[1], GRID_W, LANES), lambda b, p, i: (p, 0, 0, 0))],
        out_specs=pl.BlockSpec((2 * nq, LANES), lambda b, p, i: (b * nst + i, p)),
        out_shape=jax.ShapeDtypeStruct((batch * seq, d), BF16),
        scratch_shapes=[pltpu.VMEM((2, nkeys, nq), F32), pltpu.VMEM((2, nkeys, nq), F32),
                        pltpu.VMEM((2, nkeys, nq), BF16), pltpu.VMEM((2, nkeys, nq), BF16)],
        compiler_params=_cparams(("parallel", "parallel", "arbitrary")),
        name="natten",
    )(proj_l, proj_l, proj_l, proj_l, vt_l, proj_c, vt_c, e2)


def _ctxattn_body(q_ref, k_ref, v_ref, o_ref):
    q = q_ref[...]
    k = k_ref[...]
    v = v_ref[...]
    lane = lax.broadcasted_iota(jnp.int32, (1, LANES), 1)
    lo = lane < NA_HEAD_DIM
    scale = NA_HEAD_DIM ** -0.5
    outs = []
    for a in range(2):
        sel = lo if a == 0 else jnp.logical_not(lo)
        qa = jnp.where(sel, q, jnp.zeros_like(q)) * jnp.asarray(scale, BF16)
        s = lax.dot_general(qa, k, (((1,), (1,)), ((), ())), preferred_element_type=F32)
        m = jnp.max(s, axis=-1, keepdims=True)
        p = jnp.exp(s - m)
        l = jnp.sum(p, axis=-1, keepdims=True)
        outs.append(jnp.dot(p.astype(BF16), v, preferred_element_type=F32) / l)
    o_ref[...] = jnp.where(lo, outs[0], outs[1]).astype(o_ref.dtype)


def _ctxattn_call(proj_c, batch, lc, d):
    npair = d // LANES
    return pl.pallas_call(
        _ctxattn_body,
        grid=(batch, npair),
        in_specs=[pl.BlockSpec((lc, LANES), lambda b, p: (b, p)),
                  pl.BlockSpec((lc, LANES), lambda b, p: (b, npair + p)),
                  pl.BlockSpec((lc, LANES), lambda b, p: (b, 2 * npair + p))],
        out_specs=pl.BlockSpec((lc, LANES), lambda b, p: (b, p)),
        out_shape=jax.ShapeDtypeStruct((batch * lc, d), BF16),
        compiler_params=_cparams(("parallel", "parallel")),
        name="ctx_attn",
    )(proj_c, proj_c, proj_c)


def _dwconv_silu_body(xp_ref, x_ref, xn_ref, w_ref, b_ref, o_ref, *, tiles_per_seq):
    i = pl.program_id(0)
    k = w_ref.shape[0]
    half = k // 2
    tb = x_ref.shape[0]
    hr = xp_ref.shape[0]
    first = (i % tiles_per_seq) == 0
    last = (i % tiles_per_seq) == tiles_per_seq - 1
    prev = jnp.where(first, 0.0, xp_ref[...].astype(F32))
    nxt = jnp.where(last, 0.0, xn_ref[...].astype(F32))
    ext = jnp.concatenate([prev, x_ref[...].astype(F32), nxt], axis=0)
    acc = jnp.zeros((tb, x_ref.shape[1]), F32) + b_ref[...]
    for t in range(k):
        off = hr - half + t
        acc = acc + ext[off:off + tb, :] * w_ref[pl.ds(t, 1), :]
    o_ref[...] = _silu(acc).astype(o_ref.dtype)


def _dwconv_silu_call(proj, col0, width, w, b, seq, *, tb, tc):
    t = proj.shape[0]
    hr = BF16_SUBLANES
    nrt = t // tb
    cb0 = col0 // tc
    body = functools.partial(_dwconv_silu_body, tiles_per_seq=seq // tb)
    return pl.pallas_call(
        body,
        grid=(nrt, width // tc),
        in_specs=[pl.BlockSpec((hr, tc), lambda i, j: (jnp.maximum(i * (tb // hr) - 1, 0), cb0 + j)),
                  pl.BlockSpec((tb, tc), lambda i, j: (i, cb0 + j)),
                  pl.BlockSpec((hr, tc), lambda i, j: (jnp.minimum((i + 1) * (tb // hr), t // hr - 1), cb0 + j)),
                  pl.BlockSpec((w.shape[0], tc), lambda i, j: (0, j)),
                  pl.BlockSpec((1, tc), lambda i, j: (0, j))],
        out_specs=pl.BlockSpec((tb, tc), lambda i, j: (i, j)),
        out_shape=jax.ShapeDtypeStruct((t, width), BF16),
        compiler_params=_cparams(("parallel", "parallel")),
        name="dwconv_silu",
    )(proj, proj, proj, w, b.reshape(1, width))


def _ssd_body(x_ref, b_ref, c_ref, dt_ref, dtb_ref, alog_ref, h0_ref, y_ref, hT_ref, st_scr, *, nsteps):
    q = SSM_CHUNK
    dr = pl.program_id(1)
    s = pl.program_id(2)

    @pl.when(s == 0)
    def _():
        st_scr[...] = h0_ref[0, 0]

    fwd = dr == 0
    dtv = _softplus(dt_ref[...] + dtb_ref[0])
    la = dtv * (-jnp.exp(alog_ref[0]))
    ii = lax.broadcasted_iota(jnp.int32, (q, q), 0)
    jj = lax.broadcasted_iota(jnp.int32, (q, q), 1)
    tri = (jj - ii) * jnp.where(fwd, 1, -1) <= 0
    cum = jnp.dot(tri.astype(F32), la, precision=HIGHEST, preferred_element_type=F32)
    tot = jnp.sum(la, axis=0, keepdims=True)
    cum_t = cum.T
    lane = lax.broadcasted_iota(jnp.int32, (1, LANES), 1)
    lo = lane < SSM_HEAD_DIM
    heads_per_group = SSM_HEADS // SSM_GROUPS
    for g in range(SSM_GROUPS):
        bg = b_ref[:, g * SSM_STATE:(g + 1) * SSM_STATE]
        cg = c_ref[:, g * SSM_STATE:(g + 1) * SSM_STATE]
        cb = lax.dot_general(cg, bg, (((1,), (1,)), ((), ())), preferred_element_type=F32)
        bg_t = bg.astype(F32).T.astype(BF16)
        for hp in range(g * heads_per_group // 2, (g + 1) * heads_per_group // 2):
            ha, hb = 2 * hp, 2 * hp + 1
            xp = x_ref[:, hp * LANES:(hp + 1) * LANES].astype(F32)
            cca = jnp.broadcast_to(cum[:, ha:ha + 1], (q, q))
            ccb = jnp.broadcast_to(cum[:, hb:hb + 1], (q, q))
            l_a = jnp.exp(jnp.where(tri, cca - cum_t[ha:ha + 1, :], NEG))
            l_b = jnp.exp(jnp.where(tri, ccb - cum_t[hb:hb + 1, :], NEG))
            m_a = (cb * l_a).astype(BF16)
            m_b = (cb * l_b).astype(BF16)
            dtp = jnp.where(lo, jnp.broadcast_to(dtv[:, ha:ha + 1], (q, LANES)),
                            jnp.broadcast_to(dtv[:, hb:hb + 1], (q, LANES)))
            xdt = xp * dtp
            xdt_b = xdt.astype(BF16)
            y_intra = jnp.where(lo, jnp.dot(m_a, xdt_b, preferred_element_type=F32),
                                jnp.dot(m_b, xdt_b, preferred_element_type=F32))
            ccp = jnp.where(lo, cca, ccb)
            st = st_scr[hp]
            y_inter = jnp.dot(cg, st.astype(BF16), preferred_element_type=F32) * jnp.exp(ccp)
            y_ref[0, :, hp * LANES:(hp + 1) * LANES] = y_intra + y_inter
            totp = jnp.where(lo, jnp.broadcast_to(tot[:, ha:ha + 1], (1, LANES)),
                             jnp.broadcast_to(tot[:, hb:hb + 1], (1, LANES)))
            xw = (xdt * jnp.exp(totp - ccp)).astype(BF16)
            st_scr[hp] = st * jnp.exp(totp) + jnp.dot(bg_t, xw, preferred_element_type=F32)

    @pl.when(s == nsteps - 1)
    def _():
        hT_ref[0, 0] = st_scr[...]


def _ssd_call(xbc, dt, dtb, alog, h0, batch, seq):
    inner = SSM_HEADS * SSM_HEAD_DIM
    gn = SSM_GROUPS * SSM_STATE
    nc = seq // SSM_CHUNK
    npair = SSM_HEADS // 2

    def rblk(b, d, s):
        return b * nc + jnp.where(d == 0, s, nc - 1 - s)

    body = functools.partial(_ssd_body, nsteps=nc)
    return pl.pallas_call(
        body,
        grid=(batch, 2, nc),
        in_specs=[pl.BlockSpec((SSM_CHUNK, inner), lambda b, d, s: (rblk(b, d, s), 0)),
                  pl.BlockSpec((SSM_CHUNK, gn), lambda b, d, s: (rblk(b, d, s), inner // gn)),
                  pl.BlockSpec((SSM_CHUNK, gn), lambda b, d, s: (rblk(b, d, s), inner // gn + 1)),
                  pl.BlockSpec((SSM_CHUNK, LANES), lambda b, d, s: (rblk(b, d, s), d)),
                  pl.BlockSpec((1, 1, LANES), lambda b, d, s: (d, 0, 0)),
                  pl.BlockSpec((1, 1, LANES), lambda b, d, s: (d, 0, 0)),
                  pl.BlockSpec((1, 1, npair, SSM_STATE, LANES), lambda b, d, s: (b, d, 0, 0, 0))],
        out_specs=[pl.BlockSpec((1, SSM_CHUNK, inner), lambda b, d, s: (d, rblk(b, d, s), 0)),
                   pl.BlockSpec((1, 1, npair, SSM_STATE, LANES), lambda b, d, s: (b, d, 0, 0, 0))],
        out_shape=[jax.ShapeDtypeStruct((2, batch * seq, inner), F32),
                   jax.ShapeDtypeStruct((batch, 2, npair, SSM_STATE, LANES), F32)],
        scratch_shapes=[pltpu.VMEM((npair, SSM_STATE, LANES), F32)],
        compiler_params=_cparams(("parallel", "parallel", "arbitrary")),
        name="ssd",
    )(xbc, xbc, xbc, dt, dtb, alog, h0)


def _evenout_body(attn_ref, y_ref, xs_ref, z_ref, h_ref, gate_ref, dsk_ref, nw_ref, w_ref, lg_ref, lb_ref,
                  o_ref, ssm_scr, *, row_base, tiles_per_row, alpha):
    i = pl.program_id(0)
    rid = row_base + i // tiles_per_row
    inner = xs_ref.shape[1]
    gw = inner // SSM_GROUPS
    z = z_ref[...].astype(F32)
    y = y_ref[0] + y_ref[1] + xs_ref[...].astype(F32) * dsk_ref[...]
    u = y * _silu(z)
    for g in range(SSM_GROUPS):
        ug = u[:, g * gw:(g + 1) * gw]
        ms = jnp.mean(ug * ug, axis=-1, keepdims=True)
        ssm_scr[:, g * gw:(g + 1) * gw] = (ug * lax.rsqrt(ms + RMS_EPS) * nw_ref[:, g * gw:(g + 1) * gw]).astype(BF16)
    d_attn = attn_ref.shape[1]
    acc = jnp.dot(attn_ref[...], w_ref[:d_attn, :], preferred_element_type=F32)
    acc = acc + jnp.dot(ssm_scr[...], w_ref[d_attn:, :], preferred_element_type=F32)
    o_ref[...] = _ln_residual(h_ref[...], acc, _mod_row(gate_ref, rid), lg_ref[0], lb_ref[0], alpha)


def _evenout_call(attn, y2, xbc, proj, z_cb, h, mods, layer, dskip_row, norm_w, w_out, ln_g, ln_b, ln_idx,
                  *, tm, row_base, tiles_per_row, alpha):
    t, d = h.shape
    inner = y2.shape[2]
    body = functools.partial(_evenout_body, row_base=row_base, tiles_per_row=tiles_per_row, alpha=alpha)
    return pl.pallas_call(
        body,
        grid=(t // tm,),
        in_specs=[pl.BlockSpec((tm, d), lambda i: (i, 0)),
                  pl.BlockSpec((2, tm, inner), lambda i: (0, i, 0)),
                  pl.BlockSpec((tm, inner), lambda i: (i, 0)),
                  pl.BlockSpec((tm, inner), lambda i: (i, z_cb)),
                  pl.BlockSpec((tm, d), lambda i: (i, 0)),
                  pl.BlockSpec((1, 8, d), lambda i: (layer, 0, 2)),
                  pl.BlockSpec((1, inner), lambda i: (0, 0)),
                  pl.BlockSpec((1, inner), lambda i: (0, 0)),
                  pl.BlockSpec(w_out.shape, lambda i: (0, 0)),
                  pl.BlockSpec((1, 1, d), lambda i: (ln_idx, 0, 0)),
                  pl.BlockSpec((1, 1, d), lambda i: (ln_idx, 0, 0))],
        out_specs=pl.BlockSpec((tm, d), lambda i: (i, 0)),
        out_shape=jax.ShapeDtypeStruct((t, d), F32),
        scratch_shapes=[pltpu.VMEM((tm, inner), BF16)],
        compiler_params=_cparams(("parallel",)),
        name="even_out",
    )(attn, y2, xbc, proj, h, mods, dskip_row, norm_w, w_out, ln_g, ln_b)


def _mlp_chunks(u, w1_ref, w3_ref, w2_ref, acc_ref):
    nf = w1_ref.shape[0]

    def step(f, carry):
        a = jnp.dot(u, w1_ref[f], preferred_element_type=F32)
        b = jnp.dot(u, w3_ref[f], preferred_element_type=F32)
        t = (_silu(a) * b).astype(BF16)
        acc_ref[...] += jnp.dot(t, w2_ref[f], preferred_element_type=F32)
        return carry

    acc_ref[...] = jnp.zeros_like(acc_ref)
    lax.fori_loop(0, nf, step, 0)


def _ffn_body(h_ref, sh_ref, sc_ref, gate_ref, w1_ref, w3_ref, w2_ref, lg_ref, lb_ref, o_ref, acc_scr,
              *, row_base, tiles_per_row, alpha):
    i = pl.program_id(0)
    rid = row_base + i // tiles_per_row
    h = h_ref[...]
    u = (h * (1.0 + _mod_row(sc_ref, rid)) + _mod_row(sh_ref, rid)).astype(BF16)
    _mlp_chunks(u, w1_ref, w3_ref, w2_ref, acc_scr)
    o_ref[...] = _ln_residual(h, acc_scr[...], _mod_row(gate_ref, rid), lg_ref[0], lb_ref[0], alpha)


def _ffn_call(h, mods, layer, w1c, w3c, w2c, ln_g, ln_b, ln_idx, *, tm, row_base, tiles_per_row, alpha):
    t, d = h.shape
    body = functools.partial(_ffn_body, row_base=row_base, tiles_per_row=tiles_per_row, alpha=alpha)
    wspec = lambda w: pl.BlockSpec(w.shape, lambda i: (0, 0, 0), pipeline_mode=pl.Buffered(1))
    return pl.pallas_call(
        body,
        grid=(t // tm,),
        in_specs=[pl.BlockSpec((tm, d), lambda i: (i, 0)),
                  pl.BlockSpec((1, 8, d), lambda i: (layer, 0, 3)),
                  pl.BlockSpec((1, 8, d), lambda i: (layer, 0, 4)),
                  pl.BlockSpec((1, 8, d), lambda i: (layer, 0, 5)),
                  wspec(w1c), wspec(w3c), wspec(w2c),
                  pl.BlockSpec((1, 1, d), lambda i: (ln_idx, 0, 0)),
                  pl.BlockSpec((1, 1, d), lambda i: (ln_idx, 0, 0))],
        out_specs=pl.BlockSpec((tm, d), lambda i: (i, 0)),
        out_shape=jax.ShapeDtypeStruct((t, d), F32),
        scratch_shapes=[pltpu.VMEM((tm, d), F32)],
        compiler_params=_cparams(("parallel",)),
        name="ffn",
    )(h, mods, mods, mods, w1c, w3c, w2c, ln_g, ln_b)


def _scout_body(pb_ref, pcp_ref, pc_ref, pcn_ref, php_ref, ph_ref, phn_ref, h_ref, gate_ref, cw_ref, w_ref,
                lg_ref, lb_ref, o_ref, *, row_base, tiles_per_row, tiles_per_seq, alpha):
    i = pl.program_id(0)
    rid = row_base + i // tiles_per_row
    tm = ph_ref.shape[0]
    hr = php_ref.shape[0]
    first = (i % tiles_per_seq) == 0
    last = (i % tiles_per_seq) == tiles_per_seq - 1
    prev = jnp.where(first, 0.0, pcp_ref[...].astype(F32) * php_ref[...].astype(F32))
    cur = pc_ref[...].astype(F32) * ph_ref[...].astype(F32)
    nxt = jnp.where(last, 0.0, pcn_ref[...].astype(F32) * phn_ref[...].astype(F32))
    ext = jnp.concatenate([prev, cur, nxt], axis=0)
    acc = jnp.zeros_like(cur)
    for t in range(SC_CONV):
        off = hr - SC_CONV // 2 + t
        acc = acc + ext[off:off + tm, :] * cw_ref[pl.ds(t, 1), :]
    a = (pb_ref[...].astype(F32) * acc).astype(BF16)
    y = jnp.dot(a, w_ref[...], preferred_element_type=F32)
    o_ref[...] = _ln_residual(h_ref[...], y, _mod_row(gate_ref, rid), lg_ref[0], lb_ref[0], alpha)


def _scout_call(proj, h, mods, layer, conv_w, w_out, ln_g, ln_b, ln_idx, seq,
                *, tm, row_base, tiles_per_row, alpha):
    t, d = h.shape
    hr = BF16_SUBLANES
    r = tm // hr
    nh = t // hr
    body = functools.partial(_scout_body, row_base=row_base, tiles_per_row=tiles_per_row,
                             tiles_per_seq=seq // tm, alpha=alpha)
    prev_map = lambda cb: (lambda i: (jnp.maximum(i * r - 1, 0), cb))
    next_map = lambda cb: (lambda i: (jnp.minimum((i + 1) * r, nh - 1), cb))
    return pl.pallas_call(
        body,
        grid=(t // tm,),
        in_specs=[pl.BlockSpec((tm, d), lambda i: (i, 0)),
                  pl.BlockSpec((hr, d), prev_map(1)),
                  pl.BlockSpec((tm, d), lambda i: (i, 1)),
                  pl.BlockSpec((hr, d), next_map(1)),
                  pl.BlockSpec((hr, d), prev_map(2)),
                  pl.BlockSpec((tm, d), lambda i: (i, 2)),
                  pl.BlockSpec((hr, d), next_map(2)),
                  pl.BlockSpec((tm, d), lambda i: (i, 0)),
                  pl.BlockSpec((1, 8, d), lambda i: (layer, 0, 2)),
                  pl.BlockSpec(conv_w.shape, lambda i: (0, 0)),
                  pl.BlockSpec(w_out.shape, lambda i: (0, 0)),
                  pl.BlockSpec((1, 1, d), lambda i: (ln_idx, 0, 0)),
                  pl.BlockSpec((1, 1, d), lambda i: (ln_idx, 0, 0))],
        out_specs=pl.BlockSpec((tm, d), lambda i: (i, 0)),
        out_shape=jax.ShapeDtypeStruct((t, d), F32),
        compiler_params=_cparams(("parallel",)),
        name="shortconv_out",
    )(proj, proj, proj, proj, proj, proj, proj, h, mods, conv_w, w_out, ln_g, ln_b)


def _router_body(h_ref, sh_ref, sc_ref, wr_ref, u_ref, meta_ref, cnt_ref, cnt_scr, *, row_base, tiles_per_row):
    i = pl.program_id(0)
    rid = row_base + i // tiles_per_row
    tm = h_ref.shape[0]

    @pl.when(i == 0)
    def _():
        cnt_scr[...] = jnp.zeros_like(cnt_scr)

    u = h_ref[...] * (1.0 + _mod_row(sc_ref, rid)) + _mod_row(sh_ref, rid)
    u_ref[...] = u
    logits = jnp.dot(u, wr_ref[...], precision=HIGHEST, preferred_element_type=F32)
    lane = lax.broadcasted_iota(jnp.int32, (tm, LANES), 1).astype(F32)
    lg = jnp.where(lane < N_EXPERTS, logits, NEG)
    v1 = jnp.max(lg, axis=-1, keepdims=True)
    e1 = jnp.min(jnp.where(lg == v1, lane, float(LANES)), axis=-1, keepdims=True)
    lg2 = jnp.where(lane == e1, 2 * NEG, lg)
    v2 = jnp.max(lg2, axis=-1, keepdims=True)
    e2 = jnp.min(jnp.where(lg2 == v2, lane, float(LANES)), axis=-1, keepdims=True)
    g2 = 1.0 / (1.0 + jnp.exp(v1 - v2))
    g1 = 1.0 - g2
    oh1 = (lane == e1).astype(F32)
    oh2 = (lane == e2).astype(F32)
    both = (oh1 + oh2).astype(BF16)
    ii = lax.broadcasted_iota(jnp.int32, (tm, tm), 0)
    jj = lax.broadcasted_iota(jnp.int32, (tm, tm), 1)
    strict = (jj < ii).astype(BF16)
    before = jnp.dot(strict, both, preferred_element_type=F32) + cnt_scr[...]
    p1 = jnp.sum(before * oh1, axis=-1, keepdims=True)
    p2 = jnp.sum(before * oh2, axis=-1, keepdims=True)
    cnt_scr[...] = cnt_scr[...] + jnp.sum(oh1 + oh2, axis=0, keepdims=True)
    meta = jnp.where(lane == 0, e1,
           jnp.where(lane == 1, e2,
           jnp.where(lane == 2, p1,
           jnp.where(lane == 3, p2,
           jnp.where(lane == 4, g1,
           jnp.where(lane == 5, g2, 0.0))))))
    meta_ref[...] = meta
    cnt_ref[...] = jnp.broadcast_to(cnt_scr[...], cnt_ref.shape)


def _router_call(h, mods, layer, wr_pad, *, tm, row_base, tiles_per_row):
    t, d = h.shape
    body = functools.partial(_router_body, row_base=row_base, tiles_per_row=tiles_per_row)
    return pl.pallas_call(
        body,
        grid=(t // tm,),
        in_specs=[pl.BlockSpec((tm, d), lambda i: (i, 0)),
                  pl.BlockSpec((1, 8, d), lambda i: (layer, 0, 3)),
                  pl.BlockSpec((1, 8, d), lambda i: (layer, 0, 4)),
                  pl.BlockSpec(wr_pad.shape, lambda i: (0, 0))],
        out_specs=[pl.BlockSpec((tm, d), lambda i: (i, 0)),
                   pl.BlockSpec((tm, LANES), lambda i: (i, 0)),
                   pl.BlockSpec((8, LANES), lambda i: (0, 0))],
        out_shape=[jax.ShapeDtypeStruct((t, d), F32),
                   jax.ShapeDtypeStruct((t, LANES), F32),
                   jax.ShapeDtypeStruct((8, LANES), F32)],
        scratch_shapes=[pltpu.VMEM((1, LANES), F32)],
        compiler_params=_cparams(("arbitrary",)),
        name="moe_router",
    )(h, mods, mods, wr_pad)


def _dispatch_body(tv_ref, dest_ref, u_ref, rows_hbm, zbuf, sem, zsem):
    i = pl.program_id(0)
    tm = u_ref.shape[0]
    tile = zbuf.shape[0]

    @pl.when(i == 0)
    def _():
        zbuf[...] = jnp.zeros_like(zbuf)

        def fill_copy(t):
            return pltpu.make_async_copy(zbuf, rows_hbm.at[pl.ds(t * tile, tile)], zsem.at[0])

        def fill(t, c):
            @pl.when(tv_ref[t] < tile)
            def _():
                fill_copy(t).start()
            return c

        def fill_wait(t, c):
            @pl.when(tv_ref[t] < tile)
            def _():
                fill_copy(t).wait()
            return c

        lax.fori_loop(0, tv_ref.shape[0], fill, 0)
        lax.fori_loop(0, tv_ref.shape[0], fill_wait, 0)

    def row_copy(r, k):
        return pltpu.make_async_copy(u_ref.at[pl.ds(r, 1)], rows_hbm.at[pl.ds(dest_ref[0, k, r], 1)], sem.at[k])

    def issue(r, c):
        for k in range(TOP_K):
            row_copy(r, k).start()
        return c

    def drain(r, c):
        for k in range(TOP_K):
            row_copy(r, k).wait()
        return c

    lax.fori_loop(0, tm, issue, 0, unroll=8)
    lax.fori_loop(0, tm, drain, 0, unroll=8)


def _dispatch_call(u, dest_t, n_rows, tile_valid, *, tm):
    t, d = u.shape
    grid_spec = pltpu.PrefetchScalarGridSpec(
        num_scalar_prefetch=1,
        grid=(t // tm,),
        in_specs=[pl.BlockSpec((1, TOP_K, tm), lambda i, tv: (i, 0, 0), memory_space=pltpu.SMEM),
                  pl.BlockSpec((tm, d), lambda i, tv: (i, 0))],
        out_specs=pl.BlockSpec(memory_space=pl.ANY),
        scratch_shapes=[pltpu.VMEM((MOE_TILE, d), F32),
                        pltpu.SemaphoreType.DMA((TOP_K,)),
                        pltpu.SemaphoreType.DMA((1,))],
    )
    return pl.pallas_call(
        _dispatch_body,
        grid_spec=grid_spec,
        out_shape=jax.ShapeDtypeStruct((n_rows, d), F32),
        compiler_params=_cparams(("arbitrary",)),
        name="moe_dispatch",
    )(tile_valid, dest_t, u)


def _moe_body(te_ref, x_ref, w1_hbm, w3_hbm, w2_hbm, o_ref, w1s, w3s, w2s, acc_scr, wsem):
    t = pl.program_id(0)
    e = te_ref[t]
    e_prev = te_ref[jnp.maximum(t - 1, 0)]

    @pl.when(jnp.logical_or(t == 0, e != e_prev))
    def _():
        nf, _, fc = w1s.shape
        copies = [pltpu.make_async_copy(w2_hbm.at[e], w2s, wsem.at[2])]
        for f in range(nf):
            copies.append(pltpu.make_async_copy(w1_hbm.at[e, :, pl.ds(f * fc, fc)], w1s.at[f], wsem.at[0]))
            copies.append(pltpu.make_async_copy(w3_hbm.at[e, :, pl.ds(f * fc, fc)], w3s.at[f], wsem.at[1]))
        for cp in copies:
            cp.start()
        for cp in copies:
            cp.wait()

    _mlp_chunks(x_ref[...].astype(BF16), w1s, w3s, w2s, acc_scr)
    o_ref[...] = acc_scr[...]


def _moe_call(rows_in, tile_expert, w1, w3, w2c):
    n_rows, d = rows_in.shape
    tr = MOE_TILE
    w1c = jax.ShapeDtypeStruct((w1.shape[0], w1.shape[2] // MOE_CHUNK, d, MOE_CHUNK), BF16)
    w3c = w1c
    grid_spec = pltpu.PrefetchScalarGridSpec(
        num_scalar_prefetch=1,
        grid=(n_rows // tr,),
        in_specs=[pl.BlockSpec((tr, d), lambda i, te: (i, 0)),
                  pl.BlockSpec(memory_space=pl.ANY),
                  pl.BlockSpec(memory_space=pl.ANY),
                  pl.BlockSpec(memory_space=pl.ANY)],
        out_specs=pl.BlockSpec((tr, d), lambda i, te: (i, 0)),
        scratch_shapes=[pltpu.VMEM(w1c.shape[1:], BF16),
                        pltpu.VMEM(w3c.shape[1:], BF16),
                        pltpu.VMEM(w2c.shape[1:], BF16),
                        pltpu.VMEM((tr, d), F32),
                        pltpu.SemaphoreType.DMA((3,))],
    )
    return pl.pallas_call(
        _moe_body,
        grid_spec=grid_spec,
        out_shape=jax.ShapeDtypeStruct((n_rows, d), F32),
        compiler_params=_cparams(("arbitrary",)),
        name="moe_experts",
    )(tile_expert, rows_in, w1, w3, w2c)


def _combine_body(dest_ref, rows_hbm, meta_ref, h_ref, gate_ref, lg_ref, lb_ref, o_ref, rbuf, gsem,
                  *, row_base, tiles_per_row, alpha):
    i = pl.program_id(0)
    rid = row_base + i // tiles_per_row
    tm = h_ref.shape[0]

    def row_copy(r, k):
        return pltpu.make_async_copy(rows_hbm.at[pl.ds(dest_ref[0, k, r], 1)], rbuf.at[k, pl.ds(r, 1)], gsem.at[k])

    def issue(r, c):
        for k in range(TOP_K):
            row_copy(r, k).start()
        return c

    def drain(r, c):
        for k in range(TOP_K):
            row_copy(r, k).wait()
        return c

    lax.fori_loop(0, tm, issue, 0, unroll=8)
    lax.fori_loop(0, tm, drain, 0, unroll=8)
    meta = meta_ref[...]
    y = meta[:, 4:5] * rbuf[0] + meta[:, 5:6] * rbuf[1]
    o_ref[...] = _ln_residual(h_ref[...], y, _mod_row(gate_ref, rid), lg_ref[0], lb_ref[0], alpha)


def _combine_call(dest_t, rows_out, meta, h, mods, layer, ln_g, ln_b, ln_idx, *, tm, row_base, tiles_per_row, alpha):
    t, d = h.shape
    body = functools.partial(_combine_body, row_base=row_base, tiles_per_row=tiles_per_row, alpha=alpha)
    return pl.pallas_call(
        body,
        grid=(t // tm,),
        in_specs=[pl.BlockSpec((1, TOP_K, tm), lambda i: (i, 0, 0), memory_space=pltpu.SMEM),
                  pl.BlockSpec(memory_space=pl.ANY),
                  pl.BlockSpec((tm, LANES), lambda i: (i, 0)),
                  pl.BlockSpec((tm, d), lambda i: (i, 0)),
                  pl.BlockSpec((1, 8, d), lambda i: (layer, 0, 5)),
                  pl.BlockSpec((1, 1, d), lambda i: (ln_idx, 0, 0)),
                  pl.BlockSpec((1, 1, d), lambda i: (ln_idx, 0, 0))],
        out_specs=pl.BlockSpec((tm, d), lambda i: (i, 0)),
        out_shape=jax.ShapeDtypeStruct((t, d), F32),
        scratch_shapes=[pltpu.VMEM((TOP_K, tm, d), F32),
                        pltpu.SemaphoreType.DMA((TOP_K,))],
        compiler_params=_cparams(("arbitrary",)),
        name="moe_combine",
    )(dest_t, rows_out, meta, h, mods, ln_g, ln_b)


def _chunk_cols(w, fc):
    k, f = w.shape[-2:]
    lead = w.shape[:-2]
    w = w.astype(BF16).reshape(lead + (k, f // fc, fc))
    return jnp.swapaxes(w, -3, -2)


def _chunk_rows(w, fc):
    f, d = w.shape[-2:]
    return w.astype(BF16).reshape(w.shape[:-2] + (f // fc, fc, d))


def _moe_layer(h, mods, layer, wr_pad, w1c, w3c, w2c, ln_g, ln_b, ln_idx, *, tm, row_base, tiles_per_row, alpha):
    t, d = h.shape
    u, meta, cnt = _router_call(h, mods, layer, wr_pad, tm=tm, row_base=row_base, tiles_per_row=tiles_per_row)
    n_assign = t * TOP_K
    n_tiles = n_assign // MOE_TILE + N_EXPERTS
    counts = cnt[0, :N_EXPERTS].astype(jnp.int32)
    padded = (counts + MOE_TILE - 1) // MOE_TILE * MOE_TILE
    pad_ends = jnp.cumsum(padded)
    pad_starts = pad_ends - padded
    top_e = meta[:, 0:TOP_K].astype(jnp.int32)
    rank = meta[:, 2:2 + TOP_K].astype(jnp.int32)
    dest = pad_starts[top_e] + rank
    dest_t = jnp.swapaxes(dest.reshape(t // tm, tm, TOP_K), 1, 2)
    tile_start = jnp.arange(n_tiles, dtype=jnp.int32) * MOE_TILE
    tile_expert = jnp.minimum(jnp.searchsorted(pad_ends, tile_start, side='right'), N_EXPERTS - 1).astype(jnp.int32)
    seg_end = jnp.where(tile_start < pad_ends[-1], (pad_starts + counts)[tile_expert], 0)
    tile_valid = jnp.clip(seg_end - tile_start, 0, MOE_TILE).astype(jnp.int32)
    rows_in = _dispatch_call(u, dest_t, n_tiles * MOE_TILE, tile_valid, tm=tm)
    rows_out = _moe_call(rows_in, tile_expert, w1c, w3c, w2c)
    return _combine_call(dest_t, rows_out, meta, h, mods, layer, ln_g, ln_b, ln_idx,
                         tm=tm, row_base=row_base, tiles_per_row=tiles_per_row, alpha=alpha)


def kernel(x, c, ctx, c_ctx, ada_w, ada_b, ln_g, ln_b, even_w_in, na_rpb, ssm_conv_w, ssm_conv_b, ssm_dt_bias,
           ssm_a_log, ssm_d, ssm_norm_w, even_w_out, ffn_w1, ffn_w3, ffn_w2, sc_w_in, sc_conv_w, sc_w_out,
           moe_router, moe_w1, moe_w3, moe_w2):
    batch, seq, d = x.shape
    lc = ctx.shape[1]
    depth = ada_w.shape[0]
    alpha = (2 * depth) ** 0.25
    inner = SSM_HEADS * SSM_HEAD_DIM
    gn = SSM_GROUPS * SSM_STATE
    n_main = 3 * d + inner + inner + 2 * gn
    assert batch + 1 <= 8

    cvec = jnp.zeros((8, d), F32).at[:batch].set(c).at[batch].set(c_ctx)
    mods = _mods_call(cvec, ada_w, ada_b)
    lng = ln_g.reshape(depth * 2, 1, d)
    lnb = ln_b.reshape(depth * 2, 1, d)

    h_lat = x.reshape(batch * seq, d)
    h_ctx = ctx.reshape(batch * lc, d)
    lat = dict(tm=TM_MLP, row_base=0, tiles_per_row=seq // TM_MLP, alpha=alpha)
    cx = dict(tm=lc, row_base=batch, tiles_per_row=1 << 20, alpha=alpha)

    for i in range(depth):
        j = i // 2
        ctx_live = any(m % 2 == 0 for m in range(i + 1, depth))
        if i % 2 == 0:
            w_in = even_w_in[j]
            w_main = w_in[:, :n_main].astype(BF16)
            w_dt = jnp.zeros((d, 2 * LANES), F32)
            w_dt = w_dt.at[:, :SSM_HEADS].set(w_in[:, n_main:n_main + SSM_HEADS])
            w_dt = w_dt.at[:, LANES:LANES + SSM_HEADS].set(w_in[:, n_main + SSM_HEADS:]).astype(BF16)
            pad16 = ((0, 0), (0, 0), (0, LANES - SSM_HEADS))
            dtb = jnp.pad(ssm_dt_bias[j][:, None, :], pad16)
            alog = jnp.pad(ssm_a_log[j][:, None, :], pad16)
            dskip = jnp.repeat(ssm_d[j], SSM_HEAD_DIM)[None, :]
            norm_w = ssm_norm_w[j][None, :]
            w_out = even_w_out[j].astype(BF16)
            e2 = _rpb_table(na_rpb[j])

            def in_proj(h, tm, row_base, tpr, tag):
                kw = dict(tm=tm, row_base=row_base, tiles_per_row=tpr)
                pm = _modmm_call(h, mods, i, 0, 1, w_main, BF16, tn=TN_PROJ, name="even_in_" + tag, **kw)
                pdt = _modmm_call(h, mods, i, 0, 1, w_dt, F32, tn=LANES, name="even_dt_" + tag, **kw)
                return pm, pdt

            proj_l, dt_l = in_proj(h_lat, TM_LAT, 0, seq // TM_LAT, "lat")
            proj_c, dt_c = in_proj(h_ctx, lc, batch, 1 << 20, "ctx")

            attn_l = _natten_call(proj_l, proj_c, e2, batch, seq, lc)
            xbc_l = _dwconv_silu_call(proj_l, 3 * d + inner, inner + 2 * gn, ssm_conv_w[j], ssm_conv_b[j], seq,
                                      tb=512, tc=512)
            xbc_c = _dwconv_silu_call(proj_c, 3 * d + inner, inner + 2 * gn, ssm_conv_w[j], ssm_conv_b[j], lc,
                                      tb=lc, tc=512)
            h0 = jnp.zeros((batch, 2, SSM_HEADS // 2, SSM_STATE, LANES), F32)
            y_c, h_c = _ssd_call(xbc_c, dt_c, dtb, alog, h0, batch, lc)
            y_l, _ = _ssd_call(xbc_l, dt_l, dtb, alog, h_c, batch, seq)

            z_cb = 3 * d // inner
            h_lat = _evenout_call(attn_l, y_l, xbc_l, proj_l, z_cb, h_lat, mods, i, dskip, norm_w, w_out,
                                  lng, lnb, 2 * i, **lat)
            w1c = _chunk_cols(ffn_w1[j], FFN_CHUNK)
            w3c = _chunk_cols(ffn_w3[j], FFN_CHUNK)
            w2c = _chunk_rows(ffn_w2[j], FFN_CHUNK)
            h_lat = _ffn_call(h_lat, mods, i, w1c, w3c, w2c, lng, lnb, 2 * i + 1, **lat)
            if ctx_live:
                attn_c = _ctxattn_call(proj_c, batch, lc, d)
                h_ctx = _evenout_call(attn_c, y_c, xbc_c, proj_c, z_cb, h_ctx, mods, i, dskip, norm_w, w_out,
                                      lng, lnb, 2 * i, **cx)
                h_ctx = _ffn_call(h_ctx, mods, i, w1c, w3c, w2c, lng, lnb, 2 * i + 1, **cx)
        else:
            w_in = sc_w_in[j].astype(BF16)
            w_out = sc_w_out[j].astype(BF16)
            wr_pad = jnp.pad(moe_router[j], ((0, 0), (0, LANES - N_EXPERTS)))
            w1c = moe_w1[j].astype(BF16)
            w3c = moe_w3[j].astype(BF16)
            w2c = _chunk_rows(moe_w2[j], MOE_CHUNK)
            proj_l = _modmm_call(h_lat, mods, i, 0, 1, w_in, BF16, tm=TM_LAT, tn=TN_PROJ, row_base=0,
                                 tiles_per_row=seq // TM_LAT, name="odd_in_lat")
            h_lat = _scout_call(proj_l, h_lat, mods, i, sc_conv_w[j], w_out, lng, lnb, 2 * i, seq, **lat)
            h_lat = _moe_layer(h_lat, mods, i, wr_pad, w1c, w3c, w2c, lng, lnb, 2 * i + 1, **lat)
            if ctx_live:
                proj_c = _modmm_call(h_ctx, mods, i, 0, 1, w_in, BF16, tm=lc, tn=TN_PROJ, row_base=batch,
                                     tiles_per_row=1 << 20, name="odd_in_ctx")
                h_ctx = _scout_call(proj_c, h_ctx, mods, i, sc_conv_w[j], w_out, lng, lnb, 2 * i, lc, **cx)
                h_ctx = _moe_layer(h_ctx, mods, i, wr_pad, w1c, w3c, w2c, lng, lnb, 2 * i + 1, **cx)
    return h_lat.reshape(batch, seq, d)
```

```python
import functools

import numpy as np
import jax
import jax.numpy as jnp
from jax import lax
from jax.experimental import pallas as pl
from jax.experimental.pallas import tpu as pltpu

F32 = jnp.float32
BF16 = jnp.bfloat16
HIGHEST = lax.Precision.HIGHEST

GRID_W = 64
NA_HEAD_DIM = 64
NA_ROWS = 8
NA_COLS = 16
SSM_HEAD_DIM = 64
SSM_HEADS = 16
SSM_GROUPS = 4
SSM_STATE = 128
SSM_CONV = 5
SSM_CHUNK = 128
SC_CONV = 3
N_EXPERTS = 8
TOP_K = 2
LN_EPS = 1e-5
RMS_EPS = 1e-5

LANES = 128
BF16_SUBLANES = 16
NEG = -1e30
VMEM_LIMIT = 56 * 1024 * 1024

TM_PROJ = 512
TN_PROJ = 1024
TM_MLP = 512
NA_QR = 4
NA_WIN = NA_QR + NA_ROWS
NA_DMIN = 1 - NA_QR
NA_ND = 2 * (NA_QR + NA_ROWS - 1)
NA_BLOCKS = 4
MOE_TILE = 512
FFN_CHUNK = 256
MOE_CHUNK = 512


def _cparams(sem, vmem=None):
    return pltpu.CompilerParams(dimension_semantics=sem, vmem_limit_bytes=vmem or VMEM_LIMIT)


def _silu(x):
    return x * jax.nn.sigmoid(x)


def _softplus(x):
    return jnp.maximum(x, 0.0) + jnp.log(1.0 + jnp.exp(-jnp.abs(x)))


def _ln_residual(h, y, gate, g, b, alpha):
    v = alpha * h + gate * y
    mu = jnp.mean(v, axis=-1, keepdims=True)
    d = v - mu
    var = jnp.mean(d * d, axis=-1, keepdims=True)
    return d * lax.rsqrt(var + LN_EPS) * g + b


def _mod_row(ref, rid):
    return ref[0, pl.ds(rid, 1), :]


def _mods_body(c_ref, w_ref, b_ref, o_ref):
    s = _silu(c_ref[...])
    o_ref[0] = jnp.dot(s, w_ref[0], precision=HIGHEST, preferred_element_type=F32) + b_ref[0]


def _mods_call(cvec, ada_w, ada_b):
    depth, d, n = ada_w.shape
    tn = 1024
    return pl.pallas_call(
        _mods_body,
        grid=(depth, n // tn),
        in_specs=[pl.BlockSpec((8, d), lambda l, j: (0, 0)),
                  pl.BlockSpec((1, d, tn), lambda l, j: (l, 0, j)),
                  pl.BlockSpec((1, 1, tn), lambda l, j: (l, 0, j))],
        out_specs=pl.BlockSpec((1, 8, tn), lambda l, j: (l, 0, j)),
        out_shape=jax.ShapeDtypeStruct((depth, 8, n), F32),
        compiler_params=_cparams(("parallel", "parallel")),
        name="mods",
    )(cvec, ada_w, ada_b.reshape(depth, 1, n))


def _inproj_body(*refs, row_base, tiles_per_row, tn, with_dt):
    if with_dt:
        h_ref, sh_ref, sc_ref, w_ref, wdt_ref, o_ref, odt_ref = refs
    else:
        h_ref, sh_ref, sc_ref, w_ref, o_ref = refs
    rid = row_base + pl.program_id(0) // tiles_per_row
    u = (h_ref[...] * (1.0 + _mod_row(sc_ref, rid)) + _mod_row(sh_ref, rid)).astype(BF16)
    for n0 in range(0, w_ref.shape[1], tn):
        o_ref[:, n0:n0 + tn] = jnp.dot(u, w_ref[:, n0:n0 + tn], preferred_element_type=F32).astype(o_ref.dtype)
    if with_dt:
        odt_ref[...] = jnp.dot(u, wdt_ref[...], preferred_element_type=F32)


def _inproj_call(h, mods, layer, w, w_dt, *, tm, row_base, tiles_per_row, name):
    t, d = h.shape
    n = w.shape[1]
    with_dt = w_dt is not None
    body = functools.partial(_inproj_body, row_base=row_base, tiles_per_row=tiles_per_row, tn=TN_PROJ,
                             with_dt=with_dt)
    resident = lambda a: pl.BlockSpec(a.shape, lambda i: (0, 0), pipeline_mode=pl.Buffered(1))
    in_specs = [pl.BlockSpec((tm, d), lambda i: (i, 0)),
                pl.BlockSpec((1, 8, d), lambda i: (layer, 0, 0)),
                pl.BlockSpec((1, 8, d), lambda i: (layer, 0, 1)),
                resident(w)]
    out_specs = [pl.BlockSpec((tm, n), lambda i: (i, 0))]
    out_shape = [jax.ShapeDtypeStruct((t, n), BF16)]
    args = [h, mods, mods, w]
    if with_dt:
        in_specs.append(resident(w_dt))
        out_specs.append(pl.BlockSpec((tm, w_dt.shape[1]), lambda i: (i, 0)))
        out_shape.append(jax.ShapeDtypeStruct((t, w_dt.shape[1]), F32))
        args.append(w_dt)
    outs = pl.pallas_call(
        body,
        grid=(t // tm,),
        in_specs=in_specs,
        out_specs=out_specs,
        out_shape=out_shape,
        compiler_params=_cparams(("parallel",)),
        name=name,
    )(*args)
    return (outs[0], outs[1]) if with_dt else outs[0]


def _rpb_onehot():
    qc = np.arange(GRID_W)[:, None]
    kc = np.arange(GRID_W)[None, :]
    c0 = np.clip(qc - NA_COLS // 2, 0, GRID_W - NA_COLS)
    inside = (kc >= c0) & (kc < c0 + NA_COLS)
    dc = kc - qc + NA_COLS - 1
    oh = np.zeros((LANES, GRID_W, GRID_W), np.float32)
    for d in range(2 * NA_COLS - 1):
        oh[d] = ((dc == d) & inside).astype(np.float32)
    mask = np.where(inside, 0.0, NEG).astype(np.float32)
    return oh.reshape(LANES, GRID_W * GRID_W), mask.reshape(1, GRID_W * GRID_W)


def _rpb_body(r_ref, oh_ref, m_ref, o_ref):
    o_ref[...] = jnp.dot(r_ref[...], oh_ref[...], precision=HIGHEST,
                         preferred_element_type=F32) + m_ref[...]


def _rpb_table(rpb):
    h = rpb.shape[0]
    nd = 2 * NA_ROWS - 1
    oh, mask = _rpb_onehot()
    r2 = jnp.pad(rpb.reshape(h * nd, 2 * NA_COLS - 1), ((0, 0), (0, LANES - 2 * NA_COLS + 1)))
    e = pl.pallas_call(
        _rpb_body,
        out_shape=jax.ShapeDtypeStruct((h * nd, GRID_W * GRID_W), F32),
        name="rpb_table",
    )(r2, jnp.asarray(oh), jnp.asarray(mask))
    e = jnp.swapaxes(e.reshape(h, nd, GRID_W, GRID_W), 2, 3)
    e = jnp.pad(e, ((0, 0), (1 - NA_DMIN, NA_DMIN + NA_ND - nd), (0, 0), (0, 0)),
                constant_values=NEG)
    first, second = e[:, 1:], e[:, :-1]
    neg = jnp.full_like(first, NEG)
    return jnp.concatenate([jnp.concatenate([first, second], axis=-1),
                            jnp.concatenate([first, neg], axis=-1),
                            jnp.concatenate([neg, second], axis=-1),
                            jnp.concatenate([neg, neg], axis=-1)], axis=1)


def _natten_window(r_first, rows):
    return jnp.clip(r_first - NA_ROWS // 2, 0, rows - NA_WIN)


def _natten_scores(q, r_first, k_ref, kc_ref, e_ref, s_scr, rows):
    nk = NA_WIN * GRID_W
    ws = _natten_window(r_first, rows)
    kwin = k_ref[pl.ds(pl.multiple_of(ws * GRID_W, LANES), nk), :]
    kc = kc_ref[...]
    idx = []
    for g in range(NA_QR // 2):
        r = r_first + 2 * g
        r0a = jnp.clip(r - NA_ROWS // 2, 0, rows - NA_ROWS)
        r0b = jnp.clip(r + 1 - NA_ROWS // 2, 0, rows - NA_ROWS)
        col = []
        for t in range(NA_WIN):
            kr = ws + t
            d = kr - r + (NA_ROWS - 1)
            out_first = jnp.logical_or(kr < r0a, kr >= r0a + NA_ROWS).astype(jnp.int32)
            out_second = jnp.logical_or(kr < r0b, kr >= r0b + NA_ROWS).astype(jnp.int32)
            col.append((2 * out_first + out_second) * NA_ND + d - NA_DMIN)
        idx.append(col)
    lo = lax.broadcasted_iota(jnp.int32, (1, LANES), 1) < NA_HEAD_DIM
    scale = jnp.asarray(NA_HEAD_DIM ** -0.5, BF16)
    nt = (((1,), (1,)), ((), ()))
    for a in range(2):
        sel = lo if a == 0 else jnp.logical_not(lo)
        qa = jnp.where(sel, q, jnp.zeros_like(q)) * scale
        bias = jnp.concatenate(
            [jnp.concatenate([e_ref[a, idx[g][t]] for g in range(NA_QR // 2)], axis=1) for t in range(NA_WIN)],
            axis=0)
        s_scr[a, :nk, :] = lax.dot_general(kwin, qa, nt, preferred_element_type=F32) + bias
        s_scr[a, nk:, :] = lax.dot_general(kc, qa, nt, preferred_element_type=F32)


def _natten_softmax_pv(r_first, s_scr, p_scr, vt_ref, vct_ref, rows):
    nk = NA_WIN * GRID_W
    nkb = s_scr.shape[1] // GRID_W
    ws = _natten_window(r_first, rows)
    vt_win = vt_ref[:, pl.ds(pl.multiple_of(ws * GRID_W, LANES), nk)]
    vct = vct_ref[...]
    outs = []
    for a in range(2):
        linv = []
        for g in range(NA_QR // 2):
            cols = slice(g * LANES, (g + 1) * LANES)
            mx = None
            for t in range(nkb):
                blk = s_scr[a, t * GRID_W:(t + 1) * GRID_W, cols]
                mx = blk if mx is None else jnp.maximum(mx, blk)
            m = jnp.max(mx, axis=0, keepdims=True)
            ls = None
            for t in range(nkb):
                p = jnp.exp(s_scr[a, t * GRID_W:(t + 1) * GRID_W, cols] - m)
                p_scr[a, t * GRID_W:(t + 1) * GRID_W, cols] = p.astype(BF16)
                ls = p if ls is None else ls + p
            linv.append(1.0 / jnp.sum(ls, axis=0, keepdims=True))
        o_t = (jnp.dot(vt_win, p_scr[a, :nk, :], preferred_element_type=F32)
               + jnp.dot(vct, p_scr[a, nk:, :], preferred_element_type=F32))
        outs.append(o_t * jnp.concatenate(linv, axis=-1))
    o_t = jnp.concatenate([outs[0][:NA_HEAD_DIM], outs[1][NA_HEAD_DIM:]], axis=0)
    return o_t.T


def _natten_body(q0_ref, q_ref, qn_ref, k_ref, vt_ref, kc_ref, vct_ref, e_ref, o_ref,
                 sa_scr, sb_scr, pa_scr, pb_scr, *, rows, nrb):
    nq = NA_QR * GRID_W
    i = pl.program_id(2)
    b0 = NA_BLOCKS * i
    bufs = ((sa_scr, pa_scr), (sb_scr, pb_scr))

    @pl.when(i == 0)
    def _():
        _natten_scores(q0_ref[...], 0, k_ref, kc_ref, e_ref, sa_scr, rows)

    for k in range(NA_BLOCKS):
        s_cur, p_cur = bufs[k % 2]
        s_nxt = bufs[(k + 1) % 2][0]
        if k + 1 < NA_BLOCKS:
            q_nxt, b_nxt = q_ref[(k + 1) * nq:(k + 2) * nq, :], b0 + k + 1
        else:
            q_nxt, b_nxt = qn_ref[...], jnp.minimum(b0 + NA_BLOCKS, nrb - 1)
        _natten_scores(q_nxt, b_nxt * NA_QR, k_ref, kc_ref, e_ref, s_nxt, rows)
        o_ref[k * nq:(k + 1) * nq, :] = _natten_softmax_pv(
            (b0 + k) * NA_QR, s_cur, p_cur, vt_ref, vct_ref, rows).astype(o_ref.dtype)


def _natten_call(proj_l, proj_c, e2, batch, seq, lc):
    d = e2.shape[0] * NA_HEAD_DIM
    npair = d // LANES
    rows = seq // GRID_W
    nrb = rows // NA_QR
    nq = NA_QR * GRID_W
    nkeys = NA_WIN * GRID_W + lc
    vt_l = jnp.swapaxes(proj_l[:, 2 * d:3 * d].reshape(batch, seq, d), 1, 2).reshape(batch * d, seq)
    vt_c = jnp.swapaxes(proj_c[:, 2 * d:3 * d].reshape(batch, lc, d), 1, 2).reshape(batch * d, lc)
    body = functools.partial(_natten_body, rows=rows, nrb=nrb)
    nst = nrb // NA_BLOCKS
    return pl.pallas_call(
        body,
        grid=(batch, npair, nst),
        in_specs=[pl.BlockSpec((nq, LANES), lambda b, p, i: (b * nrb, p)),
                  pl.BlockSpec((NA_BLOCKS * nq, LANES), lambda b, p, i: (b * nst + i, p)),
                  pl.BlockSpec((nq, LANES),
                               lambda b, p, i: (b * nrb + jnp.minimum(NA_BLOCKS * (i + 1), nrb - 1), p)),
                  pl.BlockSpec((seq, LANES), lambda b, p, i: (b, npair + p)),
                  pl.BlockSpec((LANES, seq), lambda b, p, i: (b * npair + p, 0)),
                  pl.BlockSpec((lc, LANES), lambda b, p, i: (b, npair + p)),
                  pl.BlockSpec((LANES, lc), lambda b, p, i: (b * npair + p, 0)),
                  pl.BlockSpec((2, e2.shape[1], GRID_W, LANES), lambda b, p, i: (p, 0, 0, 0))],
        out_specs=pl.BlockSpec((NA_BLOCKS * nq, LANES), lambda b, p, i: (b * nst + i, p)),
        out_shape=jax.ShapeDtypeStruct((batch * seq, d), BF16),
        scratch_shapes=[pltpu.VMEM((2, nkeys, nq), F32), pltpu.VMEM((2, nkeys, nq), F32),
                        pltpu.VMEM((2, nkeys, nq), BF16), pltpu.VMEM((2, nkeys, nq), BF16)],
        compiler_params=_cparams(("parallel", "parallel", "arbitrary")),
        name="natten",
    )(proj_l, proj_l, proj_l, proj_l, vt_l, proj_c, vt_c, e2)


def _ctxattn_body(q_ref, k_ref, v_ref, o_ref):
    q = q_ref[...]
    k = k_ref[...]
    v = v_ref[...]
    lane = lax.broadcasted_iota(jnp.int32, (1, LANES), 1)
    lo = lane < NA_HEAD_DIM
    scale = NA_HEAD_DIM ** -0.5
    outs = []
    for a in range(2):
        sel = lo if a == 0 else jnp.logical_not(lo)
        qa = jnp.where(sel, q, jnp.zeros_like(q)) * jnp.asarray(scale, BF16)
        s = lax.dot_general(qa, k, (((1,), (1,)), ((), ())), preferred_element_type=F32)
        m = jnp.max(s, axis=-1, keepdims=True)
        p = jnp.exp(s - m)
        l = jnp.sum(p, axis=-1, keepdims=True)
        outs.append(jnp.dot(p.astype(BF16), v, preferred_element_type=F32) / l)
    o_ref[...] = jnp.where(lo, outs[0], outs[1]).astype(o_ref.dtype)


def _ctxattn_call(proj_c, batch, lc, d):
    npair = d // LANES
    return pl.pallas_call(
        _ctxattn_body,
        grid=(batch, npair),
        in_specs=[pl.BlockSpec((lc, LANES), lambda b, p: (b, p)),
                  pl.BlockSpec((lc, LANES), lambda b, p: (b, npair + p)),
                  pl.BlockSpec((lc, LANES), lambda b, p: (b, 2 * npair + p))],
        out_specs=pl.BlockSpec((lc, LANES), lambda b, p: (b, p)),
        out_shape=jax.ShapeDtypeStruct((batch * lc, d), BF16),
        compiler_params=_cparams(("parallel", "parallel")),
        name="ctx_attn",
    )(proj_c, proj_c, proj_c)


def _dwconv_silu_body(xp_ref, x_ref, xn_ref, w_ref, b_ref, o_ref, *, tiles_per_seq):
    i = pl.program_id(0)
    k = w_ref.shape[0]
    half = k // 2
    tb = x_ref.shape[0]
    hr = xp_ref.shape[0]
    first = (i % tiles_per_seq) == 0
    last = (i % tiles_per_seq) == tiles_per_seq - 1
    prev = jnp.where(first, 0.0, xp_ref[...].astype(F32))
    nxt = jnp.where(last, 0.0, xn_ref[...].astype(F32))
    ext = jnp.concatenate([prev, x_ref[...].astype(F32), nxt], axis=0)
    acc = jnp.zeros((tb, x_ref.shape[1]), F32) + b_ref[...]
    for t in range(k):
        off = hr - half + t
        acc = acc + ext[off:off + tb, :] * w_ref[pl.ds(t, 1), :]
    o_ref[...] = _silu(acc).astype(o_ref.dtype)


def _dwconv_silu_call(proj, col0, width, w, b, seq, *, tb, tc):
    t = proj.shape[0]
    hr = BF16_SUBLANES
    nrt = t // tb
    cb0 = col0 // tc
    body = functools.partial(_dwconv_silu_body, tiles_per_seq=seq // tb)
    return pl.pallas_call(
        body,
        grid=(nrt, width // tc),
        in_specs=[pl.BlockSpec((hr, tc), lambda i, j: (jnp.maximum(i * (tb // hr) - 1, 0), cb0 + j)),
                  pl.BlockSpec((tb, tc), lambda i, j: (i, cb0 + j)),
                  pl.BlockSpec((hr, tc), lambda i, j: (jnp.minimum((i + 1) * (tb // hr), t // hr - 1), cb0 + j)),
                  pl.BlockSpec((w.shape[0], tc), lambda i, j: (0, j)),
                  pl.BlockSpec((1, tc), lambda i, j: (0, j))],
        out_specs=pl.BlockSpec((tb, tc), lambda i, j: (i, j)),
        out_shape=jax.ShapeDtypeStruct((t, width), BF16),
        compiler_params=_cparams(("parallel", "parallel")),
        name="dwconv_silu",
    )(proj, proj, proj, w, b.reshape(1, width))


def _ssd_body(x_ref, b_ref, c_ref, dt_ref, dtb_ref, alog_ref, h0_ref, y_ref, hT_ref, st_scr, *, nsteps):
    q = SSM_CHUNK
    dr = pl.program_id(1)
    s = pl.program_id(2)

    @pl.when(s == 0)
    def _():
        st_scr[...] = h0_ref[0, 0]

    fwd = dr == 0
    dtv = _softplus(dt_ref[...] + dtb_ref[0])
    la = dtv * (-jnp.exp(alog_ref[0]))
    ii = lax.broadcasted_iota(jnp.int32, (q, q), 0)
    jj = lax.broadcasted_iota(jnp.int32, (q, q), 1)
    tri = (jj - ii) * jnp.where(fwd, 1, -1) <= 0
    cum = jnp.dot(tri.astype(F32), la, precision=HIGHEST, preferred_element_type=F32)
    tot = jnp.sum(la, axis=0, keepdims=True)
    cum_t = cum.T
    lane = lax.broadcasted_iota(jnp.int32, (1, LANES), 1)
    lo = lane < SSM_HEAD_DIM
    heads_per_group = SSM_HEADS // SSM_GROUPS
    for g in range(SSM_GROUPS):
        bg = b_ref[:, g * SSM_STATE:(g + 1) * SSM_STATE]
        cg = c_ref[:, g * SSM_STATE:(g + 1) * SSM_STATE]
        cb = lax.dot_general(cg, bg, (((1,), (1,)), ((), ())), preferred_element_type=F32)
        bg_t = bg.astype(F32).T.astype(BF16)
        for hp in range(g * heads_per_group // 2, (g + 1) * heads_per_group // 2):
            ha, hb = 2 * hp, 2 * hp + 1
            xp = x_ref[:, hp * LANES:(hp + 1) * LANES].astype(F32)
            cca = jnp.broadcast_to(cum[:, ha:ha + 1], (q, q))
            ccb = jnp.broadcast_to(cum[:, hb:hb + 1], (q, q))
            l_a = jnp.exp(jnp.where(tri, cca - cum_t[ha:ha + 1, :], NEG))
            l_b = jnp.exp(jnp.where(tri, ccb - cum_t[hb:hb + 1, :], NEG))
            m_a = (cb * l_a).astype(BF16)
            m_b = (cb * l_b).astype(BF16)
            dtp = jnp.where(lo, jnp.broadcast_to(dtv[:, ha:ha + 1], (q, LANES)),
                            jnp.broadcast_to(dtv[:, hb:hb + 1], (q, LANES)))
            xdt = xp * dtp
            xdt_b = xdt.astype(BF16)
            y_intra = jnp.where(lo, jnp.dot(m_a, xdt_b, preferred_element_type=F32),
                                jnp.dot(m_b, xdt_b, preferred_element_type=F32))
            ccp = jnp.where(lo, cca, ccb)
            st = st_scr[hp]
            y_inter = jnp.dot(cg, st.astype(BF16), preferred_element_type=F32) * jnp.exp(ccp)
            y_ref[0, :, hp * LANES:(hp + 1) * LANES] = y_intra + y_inter
            totp = jnp.where(lo, jnp.broadcast_to(tot[:, ha:ha + 1], (1, LANES)),
                             jnp.broadcast_to(tot[:, hb:hb + 1], (1, LANES)))
            xw = (xdt * jnp.exp(totp - ccp)).astype(BF16)
            st_scr[hp] = st * jnp.exp(totp) + jnp.dot(bg_t, xw, preferred_element_type=F32)

    @pl.when(s == nsteps - 1)
    def _():
        hT_ref[0, 0] = st_scr[...]


def _ssd_call(xbc, dt, dtb, alog, h0, batch, seq):
    inner = SSM_HEADS * SSM_HEAD_DIM
    gn = SSM_GROUPS * SSM_STATE
    nc = seq // SSM_CHUNK
    npair = SSM_HEADS // 2

    def rblk(b, d, s):
        return b * nc + jnp.where(d == 0, s, nc - 1 - s)

    body = functools.partial(_ssd_body, nsteps=nc)
    return pl.pallas_call(
        body,
        grid=(batch, 2, nc),
        in_specs=[pl.BlockSpec((SSM_CHUNK, inner), lambda b, d, s: (rblk(b, d, s), 0)),
                  pl.BlockSpec((SSM_CHUNK, gn), lambda b, d, s: (rblk(b, d, s), inner // gn)),
                  pl.BlockSpec((SSM_CHUNK, gn), lambda b, d, s: (rblk(b, d, s), inner // gn + 1)),
                  pl.BlockSpec((SSM_CHUNK, LANES), lambda b, d, s: (rblk(b, d, s), d)),
                  pl.BlockSpec((1, 1, LANES), lambda b, d, s: (d, 0, 0)),
                  pl.BlockSpec((1, 1, LANES), lambda b, d, s: (d, 0, 0)),
                  pl.BlockSpec((1, 1, npair, SSM_STATE, LANES), lambda b, d, s: (b, d, 0, 0, 0))],
        out_specs=[pl.BlockSpec((1, SSM_CHUNK, inner), lambda b, d, s: (d, rblk(b, d, s), 0)),
                   pl.BlockSpec((1, 1, npair, SSM_STATE, LANES), lambda b, d, s: (b, d, 0, 0, 0))],
        out_shape=[jax.ShapeDtypeStruct((2, batch * seq, inner), F32),
                   jax.ShapeDtypeStruct((batch, 2, npair, SSM_STATE, LANES), F32)],
        scratch_shapes=[pltpu.VMEM((npair, SSM_STATE, LANES), F32)],
        compiler_params=_cparams(("parallel", "parallel", "arbitrary")),
        name="ssd",
    )(xbc, xbc, xbc, dt, dtb, alog, h0)


def _evenout_body(attn_ref, y_ref, xs_ref, z_ref, h_ref, gate_ref, dsk_ref, nw_ref, w_ref, lg_ref, lb_ref,
                  o_ref, ssm_scr, *, row_base, tiles_per_row, alpha):
    i = pl.program_id(0)
    rid = row_base + i // tiles_per_row
    inner = xs_ref.shape[1]
    gw = inner // SSM_GROUPS
    z = z_ref[...].astype(F32)
    y = y_ref[0] + y_ref[1] + xs_ref[...].astype(F32) * dsk_ref[...]
    u = y * _silu(z)
    for g in range(SSM_GROUPS):
        ug = u[:, g * gw:(g + 1) * gw]
        ms = jnp.mean(ug * ug, axis=-1, keepdims=True)
        ssm_scr[:, g * gw:(g + 1) * gw] = (ug * lax.rsqrt(ms + RMS_EPS) * nw_ref[:, g * gw:(g + 1) * gw]).astype(BF16)
    d_attn = attn_ref.shape[1]
    acc = jnp.dot(attn_ref[...], w_ref[:d_attn, :], preferred_element_type=F32)
    acc = acc + jnp.dot(ssm_scr[...], w_ref[d_attn:, :], preferred_element_type=F32)
    o_ref[...] = _ln_residual(h_ref[...], acc, _mod_row(gate_ref, rid), lg_ref[0], lb_ref[0], alpha)


def _evenout_call(attn, y2, xbc, proj, z_cb, h, mods, layer, dskip_row, norm_w, w_out, ln_g, ln_b, ln_idx,
                  *, tm, row_base, tiles_per_row, alpha):
    t, d = h.shape
    inner = y2.shape[2]
    body = functools.partial(_evenout_body, row_base=row_base, tiles_per_row=tiles_per_row, alpha=alpha)
    return pl.pallas_call(
        body,
        grid=(t // tm,),
        in_specs=[pl.BlockSpec((tm, d), lambda i: (i, 0)),
                  pl.BlockSpec((2, tm, inner), lambda i: (0, i, 0)),
                  pl.BlockSpec((tm, inner), lambda i: (i, 0)),
                  pl.BlockSpec((tm, inner), lambda i: (i, z_cb)),
                  pl.BlockSpec((tm, d), lambda i: (i, 0)),
                  pl.BlockSpec((1, 8, d), lambda i: (layer, 0, 2)),
                  pl.BlockSpec((1, inner), lambda i: (0, 0)),
                  pl.BlockSpec((1, inner), lambda i: (0, 0)),
                  pl.BlockSpec(w_out.shape, lambda i: (0, 0)),
                  pl.BlockSpec((1, 1, d), lambda i: (ln_idx, 0, 0)),
                  pl.BlockSpec((1, 1, d), lambda i: (ln_idx, 0, 0))],
        out_specs=pl.BlockSpec((tm, d), lambda i: (i, 0)),
        out_shape=jax.ShapeDtypeStruct((t, d), F32),
        scratch_shapes=[pltpu.VMEM((tm, inner), BF16)],
        compiler_params=_cparams(("parallel",)),
        name="even_out",
    )(attn, y2, xbc, proj, h, mods, dskip_row, norm_w, w_out, ln_g, ln_b)


def _mlp_chunks(u, w1_ref, w3_ref, w2_ref, acc_ref):
    nf = w1_ref.shape[0]

    def step(f, carry):
        a = jnp.dot(u, w1_ref[f], preferred_element_type=F32)
        b = jnp.dot(u, w3_ref[f], preferred_element_type=F32)
        t = (_silu(a) * b).astype(BF16)
        acc_ref[...] += jnp.dot(t, w2_ref[f], preferred_element_type=F32)
        return carry

    acc_ref[...] = jnp.zeros_like(acc_ref)
    lax.fori_loop(0, nf, step, 0)


def _ffn_body(h_ref, sh_ref, sc_ref, gate_ref, w1_ref, w3_ref, w2_ref, lg_ref, lb_ref, o_ref, acc_scr,
              *, row_base, tiles_per_row, alpha):
    i = pl.program_id(0)
    rid = row_base + i // tiles_per_row
    h = h_ref[...]
    u = (h * (1.0 + _mod_row(sc_ref, rid)) + _mod_row(sh_ref, rid)).astype(BF16)
    _mlp_chunks(u, w1_ref, w3_ref, w2_ref, acc_scr)
    o_ref[...] = _ln_residual(h, acc_scr[...], _mod_row(gate_ref, rid), lg_ref[0], lb_ref[0], alpha)


def _ffn_call(h, mods, layer, w1c, w3c, w2c, ln_g, ln_b, ln_idx, *, tm, row_base, tiles_per_row, alpha):
    t, d = h.shape
    body = functools.partial(_ffn_body, row_base=row_base, tiles_per_row=tiles_per_row, alpha=alpha)
    wspec = lambda w: pl.BlockSpec(w.shape, lambda i: (0, 0, 0), pipeline_mode=pl.Buffered(1))
    return pl.pallas_call(
        body,
        grid=(t // tm,),
        in_specs=[pl.BlockSpec((tm, d), lambda i: (i, 0)),
                  pl.BlockSpec((1, 8, d), lambda i: (layer, 0, 3)),
                  pl.BlockSpec((1, 8, d), lambda i: (layer, 0, 4)),
                  pl.BlockSpec((1, 8, d), lambda i: (layer, 0, 5)),
                  wspec(w1c), wspec(w3c), wspec(w2c),
                  pl.BlockSpec((1, 1, d), lambda i: (ln_idx, 0, 0)),
                  pl.BlockSpec((1, 1, d), lambda i: (ln_idx, 0, 0))],
        out_specs=pl.BlockSpec((tm, d), lambda i: (i, 0)),
        out_shape=jax.ShapeDtypeStruct((t, d), F32),
        scratch_shapes=[pltpu.VMEM((tm, d), F32)],
        compiler_params=_cparams(("parallel",)),
        name="ffn",
    )(h, mods, mods, mods, w1c, w3c, w2c, ln_g, ln_b)


def _scout_body(pb_ref, pcp_ref, pc_ref, pcn_ref, php_ref, ph_ref, phn_ref, h_ref, gate_ref, cw_ref, w_ref,
                lg_ref, lb_ref, o_ref, *, row_base, tiles_per_row, tiles_per_seq, alpha):
    i = pl.program_id(0)
    rid = row_base + i // tiles_per_row
    tm = ph_ref.shape[0]
    hr = php_ref.shape[0]
    first = (i % tiles_per_seq) == 0
    last = (i % tiles_per_seq) == tiles_per_seq - 1
    prev = jnp.where(first, 0.0, pcp_ref[...].astype(F32) * php_ref[...].astype(F32))
    cur = pc_ref[...].astype(F32) * ph_ref[...].astype(F32)
    nxt = jnp.where(last, 0.0, pcn_ref[...].astype(F32) * phn_ref[...].astype(F32))
    ext = jnp.concatenate([prev, cur, nxt], axis=0)
    acc = jnp.zeros_like(cur)
    for t in range(SC_CONV):
        off = hr - SC_CONV // 2 + t
        acc = acc + ext[off:off + tm, :] * cw_ref[pl.ds(t, 1), :]
    a = (pb_ref[...].astype(F32) * acc).astype(BF16)
    y = jnp.dot(a, w_ref[...], preferred_element_type=F32)
    o_ref[...] = _ln_residual(h_ref[...], y, _mod_row(gate_ref, rid), lg_ref[0], lb_ref[0], alpha)


def _scout_call(proj, h, mods, layer, conv_w, w_out, ln_g, ln_b, ln_idx, seq,
                *, tm, row_base, tiles_per_row, alpha):
    t, d = h.shape
    hr = BF16_SUBLANES
    r = tm // hr
    nh = t // hr
    body = functools.partial(_scout_body, row_base=row_base, tiles_per_row=tiles_per_row,
                             tiles_per_seq=seq // tm, alpha=alpha)
    prev_map = lambda cb: (lambda i: (jnp.maximum(i * r - 1, 0), cb))
    next_map = lambda cb: (lambda i: (jnp.minimum((i + 1) * r, nh - 1), cb))
    return pl.pallas_call(
        body,
        grid=(t // tm,),
        in_specs=[pl.BlockSpec((tm, d), lambda i: (i, 0)),
                  pl.BlockSpec((hr, d), prev_map(1)),
                  pl.BlockSpec((tm, d), lambda i: (i, 1)),
                  pl.BlockSpec((hr, d), next_map(1)),
                  pl.BlockSpec((hr, d), prev_map(2)),
                  pl.BlockSpec((tm, d), lambda i: (i, 2)),
                  pl.BlockSpec((hr, d), next_map(2)),
                  pl.BlockSpec((tm, d), lambda i: (i, 0)),
                  pl.BlockSpec((1, 8, d), lambda i: (layer, 0, 2)),
                  pl.BlockSpec(conv_w.shape, lambda i: (0, 0)),
                  pl.BlockSpec(w_out.shape, lambda i: (0, 0)),
                  pl.BlockSpec((1, 1, d), lambda i: (ln_idx, 0, 0)),
                  pl.BlockSpec((1, 1, d), lambda i: (ln_idx, 0, 0))],
        out_specs=pl.BlockSpec((tm, d), lambda i: (i, 0)),
        out_shape=jax.ShapeDtypeStruct((t, d), F32),
        compiler_params=_cparams(("parallel",)),
        name="shortconv_out",
    )(proj, proj, proj, proj, proj, proj, proj, h, mods, conv_w, w_out, ln_g, ln_b)


def _router_body(h_ref, sh_ref, sc_ref, wr_ref, u_ref, meta_ref, cnt_ref, cnt_scr, *, row_base, tiles_per_row):
    i = pl.program_id(0)
    rid = row_base + i // tiles_per_row
    tm = h_ref.shape[0]

    @pl.when(i == 0)
    def _():
        cnt_scr[...] = jnp.zeros_like(cnt_scr)

    u = h_ref[...] * (1.0 + _mod_row(sc_ref, rid)) + _mod_row(sh_ref, rid)
    u_ref[...] = u
    logits = jnp.dot(u, wr_ref[...], precision=HIGHEST, preferred_element_type=F32)
    lane = lax.broadcasted_iota(jnp.int32, (tm, LANES), 1).astype(F32)
    lg = jnp.where(lane < N_EXPERTS, logits, NEG)
    v1 = jnp.max(lg, axis=-1, keepdims=True)
    e1 = jnp.min(jnp.where(lg == v1, lane, float(LANES)), axis=-1, keepdims=True)
    lg2 = jnp.where(lane == e1, 2 * NEG, lg)
    v2 = jnp.max(lg2, axis=-1, keepdims=True)
    e2 = jnp.min(jnp.where(lg2 == v2, lane, float(LANES)), axis=-1, keepdims=True)
    g2 = 1.0 / (1.0 + jnp.exp(v1 - v2))
    g1 = 1.0 - g2
    oh1 = (lane == e1).astype(F32)
    oh2 = (lane == e2).astype(F32)
    both = (oh1 + oh2).astype(BF16)
    ii = lax.broadcasted_iota(jnp.int32, (tm, tm), 0)
    jj = lax.broadcasted_iota(jnp.int32, (tm, tm), 1)
    strict = (jj < ii).astype(BF16)
    before = jnp.dot(strict, both, preferred_element_type=F32) + cnt_scr[...]
    p1 = jnp.sum(before * oh1, axis=-1, keepdims=True)
    p2 = jnp.sum(before * oh2, axis=-1, keepdims=True)
    cnt_scr[...] = cnt_scr[...] + jnp.sum(oh1 + oh2, axis=0, keepdims=True)
    meta = jnp.where(lane == 0, e1,
           jnp.where(lane == 1, e2,
           jnp.where(lane == 2, p1,
           jnp.where(lane == 3, p2,
           jnp.where(lane == 4, g1,
           jnp.where(lane == 5, g2, 0.0))))))
    meta_ref[...] = meta
    cnt_ref[...] = jnp.broadcast_to(cnt_scr[...], cnt_ref.shape)


def _router_call(h, mods, layer, wr_pad, *, tm, row_base, tiles_per_row):
    t, d = h.shape
    body = functools.partial(_router_body, row_base=row_base, tiles_per_row=tiles_per_row)
    return pl.pallas_call(
        body,
        grid=(t // tm,),
        in_specs=[pl.BlockSpec((tm, d), lambda i: (i, 0)),
                  pl.BlockSpec((1, 8, d), lambda i: (layer, 0, 3)),
                  pl.BlockSpec((1, 8, d), lambda i: (layer, 0, 4)),
                  pl.BlockSpec(wr_pad.shape, lambda i: (0, 0))],
        out_specs=[pl.BlockSpec((tm, d), lambda i: (i, 0)),
                   pl.BlockSpec((tm, LANES), lambda i: (i, 0)),
                   pl.BlockSpec((8, LANES), lambda i: (0, 0))],
        out_shape=[jax.ShapeDtypeStruct((t, d), F32),
                   jax.ShapeDtypeStruct((t, LANES), F32),
                   jax.ShapeDtypeStruct((8, LANES), F32)],
        scratch_shapes=[pltpu.VMEM((1, LANES), F32)],
        compiler_params=_cparams(("arbitrary",)),
        name="moe_router",
    )(h, mods, mods, wr_pad)


def _dispatch_body(tv_ref, dest_ref, u_ref, rows_hbm, zbuf, sem, zsem):
    i = pl.program_id(0)
    tm = u_ref.shape[0]
    tile = zbuf.shape[0]

    @pl.when(i == 0)
    def _():
        zbuf[...] = jnp.zeros_like(zbuf)

        def fill_copy(t):
            return pltpu.make_async_copy(zbuf, rows_hbm.at[pl.ds(t * tile, tile)], zsem.at[0])

        def fill(t, c):
            @pl.when(tv_ref[t] < tile)
            def _():
                fill_copy(t).start()
            return c

        def fill_wait(t, c):
            @pl.when(tv_ref[t] < tile)
            def _():
                fill_copy(t).wait()
            return c

        lax.fori_loop(0, tv_ref.shape[0], fill, 0)
        lax.fori_loop(0, tv_ref.shape[0], fill_wait, 0)

    def row_copy(r, k):
        return pltpu.make_async_copy(u_ref.at[pl.ds(r, 1)], rows_hbm.at[pl.ds(dest_ref[0, k, r], 1)], sem.at[k])

    def issue(r, c):
        for k in range(TOP_K):
            row_copy(r, k).start()
        return c

    def drain(r, c):
        for k in range(TOP_K):
            row_copy(r, k).wait()
        return c

    lax.fori_loop(0, tm, issue, 0, unroll=8)
    lax.fori_loop(0, tm, drain, 0, unroll=8)


def _dispatch_call(u, dest_t, n_rows, tile_valid, *, tm):
    t, d = u.shape
    grid_spec = pltpu.PrefetchScalarGridSpec(
        num_scalar_prefetch=1,
        grid=(t // tm,),
        in_specs=[pl.BlockSpec((1, TOP_K, tm), lambda i, tv: (i, 0, 0), memory_space=pltpu.SMEM),
                  pl.BlockSpec((tm, d), lambda i, tv: (i, 0))],
        out_specs=pl.BlockSpec(memory_space=pl.ANY),
        scratch_shapes=[pltpu.VMEM((MOE_TILE, d), F32),
                        pltpu.SemaphoreType.DMA((TOP_K,)),
                        pltpu.SemaphoreType.DMA((1,))],
    )
    return pl.pallas_call(
        _dispatch_body,
        grid_spec=grid_spec,
        out_shape=jax.ShapeDtypeStruct((n_rows, d), F32),
        compiler_params=_cparams(("arbitrary",)),
        name="moe_dispatch",
    )(tile_valid, dest_t, u)


def _moe_body(te_ref, tv_ref, x_ref, w1_hbm, w3_hbm, w2_hbm, o_ref, w1s, w3s, w2s, acc_scr, wsem):
    t = pl.program_id(0)
    e = te_ref[t]
    e_prev = te_ref[jnp.maximum(t - 1, 0)]

    @pl.when(jnp.logical_or(t == 0, e != e_prev))
    def _():
        nf, _, fc = w1s.shape
        copies = [pltpu.make_async_copy(w2_hbm.at[e], w2s, wsem.at[2])]
        for f in range(nf):
            copies.append(pltpu.make_async_copy(w1_hbm.at[e, :, pl.ds(f * fc, fc)], w1s.at[f], wsem.at[0]))
            copies.append(pltpu.make_async_copy(w3_hbm.at[e, :, pl.ds(f * fc, fc)], w3s.at[f], wsem.at[1]))
        for cp in copies:
            cp.start()
        for cp in copies:
            cp.wait()

    @pl.when(tv_ref[t] > 0)
    def _():
        _mlp_chunks(x_ref[...].astype(BF16), w1s, w3s, w2s, acc_scr)
        o_ref[...] = acc_scr[...]

    @pl.when(tv_ref[t] == 0)
    def _():
        o_ref[...] = jnp.zeros_like(o_ref)


def _moe_call(rows_in, tile_expert, tile_valid, w1, w3, w2c):
    n_rows, d = rows_in.shape
    tr = MOE_TILE
    w1c = jax.ShapeDtypeStruct((w1.shape[0], w1.shape[2] // MOE_CHUNK, d, MOE_CHUNK), BF16)
    w3c = w1c
    grid_spec = pltpu.PrefetchScalarGridSpec(
        num_scalar_prefetch=2,
        grid=(n_rows // tr,),
        in_specs=[pl.BlockSpec((tr, d), lambda i, te, tv: (i, 0)),
                  pl.BlockSpec(memory_space=pl.ANY),
                  pl.BlockSpec(memory_space=pl.ANY),
                  pl.BlockSpec(memory_space=pl.ANY)],
        out_specs=pl.BlockSpec((tr, d), lambda i, te, tv: (i, 0)),
        scratch_shapes=[pltpu.VMEM(w1c.shape[1:], BF16),
                        pltpu.VMEM(w3c.shape[1:], BF16),
                        pltpu.VMEM(w2c.shape[1:], BF16),
                        pltpu.VMEM((tr, d), F32),
                        pltpu.SemaphoreType.DMA((3,))],
    )
    return pl.pallas_call(
        _moe_body,
        grid_spec=grid_spec,
        out_shape=jax.ShapeDtypeStruct((n_rows, d), F32),
        compiler_params=_cparams(("arbitrary",)),
        name="moe_experts",
    )(tile_expert, tile_valid, rows_in, w1, w3, w2c)


def _combine_body(dest_ref, rows_hbm, meta_ref, h_ref, gate_ref, lg_ref, lb_ref, o_ref, rbuf, gsem,
                  *, row_base, tiles_per_row, alpha):
    i = pl.program_id(0)
    rid = row_base + i // tiles_per_row
    tm = h_ref.shape[0]

    def row_copy(r, k):
        return pltpu.make_async_copy(rows_hbm.at[pl.ds(dest_ref[0, k, r], 1)], rbuf.at[k, pl.ds(r, 1)], gsem.at[k])

    def issue(r, c):
        for k in range(TOP_K):
            row_copy(r, k).start()
        return c

    def drain(r, c):
        for k in range(TOP_K):
            row_copy(r, k).wait()
        return c

    lax.fori_loop(0, tm, issue, 0, unroll=8)
    lax.fori_loop(0, tm, drain, 0, unroll=8)
    meta = meta_ref[...]
    y = meta[:, 4:5] * rbuf[0] + meta[:, 5:6] * rbuf[1]
    o_ref[...] = _ln_residual(h_ref[...], y, _mod_row(gate_ref, rid), lg_ref[0], lb_ref[0], alpha)


def _combine_call(dest_t, rows_out, meta, h, mods, layer, ln_g, ln_b, ln_idx, *, tm, row_base, tiles_per_row, alpha):
    t, d = h.shape
    body = functools.partial(_combine_body, row_base=row_base, tiles_per_row=tiles_per_row, alpha=alpha)
    return pl.pallas_call(
        body,
        grid=(t // tm,),
        in_specs=[pl.BlockSpec((1, TOP_K, tm), lambda i: (i, 0, 0), memory_space=pltpu.SMEM),
                  pl.BlockSpec(memory_space=pl.ANY),
                  pl.BlockSpec((tm, LANES), lambda i: (i, 0)),
                  pl.BlockSpec((tm, d), lambda i: (i, 0)),
                  pl.BlockSpec((1, 8, d), lambda i: (layer, 0, 5)),
                  pl.BlockSpec((1, 1, d), lambda i: (ln_idx, 0, 0)),
                  pl.BlockSpec((1, 1, d), lambda i: (ln_idx, 0, 0))],
        out_specs=pl.BlockSpec((tm, d), lambda i: (i, 0)),
        out_shape=jax.ShapeDtypeStruct((t, d), F32),
        scratch_shapes=[pltpu.VMEM((TOP_K, tm, d), F32),
                        pltpu.SemaphoreType.DMA((TOP_K,))],
        compiler_params=_cparams(("arbitrary",)),
        name="moe_combine",
    )(dest_t, rows_out, meta, h, mods, ln_g, ln_b)


def _chunk_cols(w, fc):
    k, f = w.shape[-2:]
    lead = w.shape[:-2]
    w = w.astype(BF16).reshape(lead + (k, f // fc, fc))
    return jnp.swapaxes(w, -3, -2)


def _chunk_rows(w, fc):
    f, d = w.shape[-2:]
    return w.astype(BF16).reshape(w.shape[:-2] + (f // fc, fc, d))


def _moe_layer(h, mods, layer, wr_pad, w1c, w3c, w2c, ln_g, ln_b, ln_idx, *, tm, row_base, tiles_per_row, alpha):
    t, d = h.shape
    u, meta, cnt = _router_call(h, mods, layer, wr_pad, tm=tm, row_base=row_base, tiles_per_row=tiles_per_row)
    n_assign = t * TOP_K
    n_tiles = n_assign // MOE_TILE + N_EXPERTS
    counts = cnt[0, :N_EXPERTS].astype(jnp.int32)
    padded = (counts + MOE_TILE - 1) // MOE_TILE * MOE_TILE
    pad_ends = jnp.cumsum(padded)
    pad_starts = pad_ends - padded
    top_e = meta[:, 0:TOP_K].astype(jnp.int32)
    rank = meta[:, 2:2 + TOP_K].astype(jnp.int32)
    dest = pad_starts[top_e] + rank
    dest_t = jnp.swapaxes(dest.reshape(t // tm, tm, TOP_K), 1, 2)
    tile_start = jnp.arange(n_tiles, dtype=jnp.int32) * MOE_TILE
    tile_expert = jnp.minimum(jnp.searchsorted(pad_ends, tile_start, side='right'), N_EXPERTS - 1).astype(jnp.int32)
    seg_end = jnp.where(tile_start < pad_ends[-1], (pad_starts + counts)[tile_expert], 0)
    tile_valid = jnp.clip(seg_end - tile_start, 0, MOE_TILE).astype(jnp.int32)
    rows_in = _dispatch_call(u, dest_t, n_tiles * MOE_TILE, tile_valid, tm=tm)
    rows_out = _moe_call(rows_in, tile_expert, tile_valid, w1c, w3c, w2c)
    return _combine_call(dest_t, rows_out, meta, h, mods, layer, ln_g, ln_b, ln_idx,
                         tm=tm, row_base=row_base, tiles_per_row=tiles_per_row, alpha=alpha)


def kernel(x, c, ctx, c_ctx, ada_w, ada_b, ln_g, ln_b, even_w_in, na_rpb, ssm_conv_w, ssm_conv_b, ssm_dt_bias,
           ssm_a_log, ssm_d, ssm_norm_w, even_w_out, ffn_w1, ffn_w3, ffn_w2, sc_w_in, sc_conv_w, sc_w_out,
           moe_router, moe_w1, moe_w3, moe_w2):
    batch, seq, d = x.shape
    lc = ctx.shape[1]
    depth = ada_w.shape[0]
    alpha = (2 * depth) ** 0.25
    inner = SSM_HEADS * SSM_HEAD_DIM
    gn = SSM_GROUPS * SSM_STATE
    n_main = 3 * d + inner + inner + 2 * gn
    assert batch + 1 <= 8

    cvec = jnp.zeros((8, d), F32).at[:batch].set(c).at[batch].set(c_ctx)
    mods = _mods_call(cvec, ada_w, ada_b)
    lng = ln_g.reshape(depth * 2, 1, d)
    lnb = ln_b.reshape(depth * 2, 1, d)

    h_lat = x.reshape(batch * seq, d)
    h_ctx = ctx.reshape(batch * lc, d)
    lat = dict(tm=TM_MLP, row_base=0, tiles_per_row=seq // TM_MLP, alpha=alpha)
    cx = dict(tm=lc, row_base=batch, tiles_per_row=1 << 20, alpha=alpha)

    for i in range(depth):
        j = i // 2
        ctx_live = any(m % 2 == 0 for m in range(i + 1, depth))
        if i % 2 == 0:
            w_in = even_w_in[j]
            w_main = w_in[:, :n_main].astype(BF16)
            w_dt = jnp.zeros((d, 2 * LANES), F32)
            w_dt = w_dt.at[:, :SSM_HEADS].set(w_in[:, n_main:n_main + SSM_HEADS])
            w_dt = w_dt.at[:, LANES:LANES + SSM_HEADS].set(w_in[:, n_main + SSM_HEADS:]).astype(BF16)
            pad16 = ((0, 0), (0, 0), (0, LANES - SSM_HEADS))
            dtb = jnp.pad(ssm_dt_bias[j][:, None, :], pad16)
            alog = jnp.pad(ssm_a_log[j][:, None, :], pad16)
            dskip = jnp.repeat(ssm_d[j], SSM_HEAD_DIM)[None, :]
            norm_w = ssm_norm_w[j][None, :]
            w_out = even_w_out[j].astype(BF16)
            e2 = _rpb_table(na_rpb[j])

            proj_l, dt_l = _inproj_call(h_lat, mods, i, w_main, w_dt, tm=TM_PROJ, row_base=0,
                                        tiles_per_row=seq // TM_PROJ, name="even_in_lat")
            proj_c, dt_c = _inproj_call(h_ctx, mods, i, w_main, w_dt, tm=lc, row_base=batch,
                                        tiles_per_row=1 << 20, name="even_in_ctx")

            attn_l = _natten_call(proj_l, proj_c, e2, batch, seq, lc)
            xbc_l = _dwconv_silu_call(proj_l, 3 * d + inner, inner + 2 * gn, ssm_conv_w[j], ssm_conv_b[j], seq,
                                      tb=512, tc=512)
            xbc_c = _dwconv_silu_call(proj_c, 3 * d + inner, inner + 2 * gn, ssm_conv_w[j], ssm_conv_b[j], lc,
                                      tb=lc, tc=512)
            h0 = jnp.zeros((batch, 2, SSM_HEADS // 2, SSM_STATE, LANES), F32)
            y_c, h_c = _ssd_call(xbc_c, dt_c, dtb, alog, h0, batch, lc)
            y_l, _ = _ssd_call(xbc_l, dt_l, dtb, alog, h_c, batch, seq)

            z_cb = 3 * d // inner
            h_lat = _evenout_call(attn_l, y_l, xbc_l, proj_l, z_cb, h_lat, mods, i, dskip, norm_w, w_out,
                                  lng, lnb, 2 * i, **lat)
            w1c = _chunk_cols(ffn_w1[j], FFN_CHUNK)
            w3c = _chunk_cols(ffn_w3[j], FFN_CHUNK)
            w2c = _chunk_rows(ffn_w2[j], FFN_CHUNK)
            h_lat = _ffn_call(h_lat, mods, i, w1c, w3c, w2c, lng, lnb, 2 * i + 1, **lat)
            if ctx_live:
                attn_c = _ctxattn_call(proj_c, batch, lc, d)
                h_ctx = _evenout_call(attn_c, y_c, xbc_c, proj_c, z_cb, h_ctx, mods, i, dskip, norm_w, w_out,
                                      lng, lnb, 2 * i, **cx)
                h_ctx = _ffn_call(h_ctx, mods, i, w1c, w3c, w2c, lng, lnb, 2 * i + 1, **cx)
        else:
            w_in = sc_w_in[j].astype(BF16)
            w_out = sc_w_out[j].astype(BF16)
            wr_pad = jnp.pad(moe_router[j], ((0, 0), (0, LANES - N_EXPERTS)))
            w1c = moe_w1[j].astype(BF16)
            w3c = moe_w3[j].astype(BF16)
            w2c = _chunk_rows(moe_w2[j], MOE_CHUNK)
            proj_l = _inproj_call(h_lat, mods, i, w_in, None, tm=TM_PROJ, row_base=0,
                                  tiles_per_row=seq // TM_PROJ, name="odd_in_lat")
            h_lat = _scout_call(proj_l, h_lat, mods, i, sc_conv_w[j], w_out, lng, lnb, 2 * i, seq, **lat)
            h_lat = _moe_layer(h_lat, mods, i, wr_pad, w1c, w3c, w2c, lng, lnb, 2 * i + 1, **lat)
            if ctx_live:
                proj_c = _inproj_call(h_ctx, mods, i, w_in, None, tm=lc, row_base=batch,
                                      tiles_per_row=1 << 20, name="odd_in_ctx")
                h_ctx = _scout_call(proj_c, h_ctx, mods, i, sc_conv_w[j], w_out, lng, lnb, 2 * i, lc, **cx)
                h_ctx = _moe_layer(h_ctx, mods, i, wr_pad, w1c, w3c, w2c, lng, lnb, 2 * i + 1, **cx)
    return h_lat.reshape(batch, seq, d)
```

```python
import functools

import numpy as np
import jax
import jax.numpy as jnp
from jax import lax
from jax.experimental import pallas as pl
from jax.experimental.pallas import tpu as pltpu

F32 = jnp.float32
BF16 = jnp.bfloat16
HIGHEST = lax.Precision.HIGHEST

GRID_W = 64
NA_HEAD_DIM = 64
NA_ROWS = 8
NA_COLS = 16
SSM_HEAD_DIM = 64
SSM_HEADS = 16
SSM_GROUPS = 4
SSM_STATE = 128
SSM_CONV = 5
SSM_CHUNK = 128
SC_CONV = 3
N_EXPERTS = 8
TOP_K = 2
LN_EPS = 1e-5
RMS_EPS = 1e-5

LANES = 128
MXU_DIM = 256
BF16_SUBLANES = 16
NEG = -1e30
VMEM_LIMIT = 56 * 1024 * 1024

TM_PROJ = 512
TN_PROJ = 1024
TM_MLP = 512
NA_QR = 4
NA_WIN = NA_QR + NA_ROWS
NA_DMIN = 1 - NA_QR
NA_ND = 2 * (NA_QR + NA_ROWS - 1)
NA_BLOCKS = 4
SSD_PREP_CHUNKS = 4
MOE_TILE = 512


def _cparams(sem, vmem=None):
    return pltpu.CompilerParams(dimension_semantics=sem, vmem_limit_bytes=vmem or VMEM_LIMIT)


def _silu(x):
    return x * jax.nn.sigmoid(x)


def _softplus(x):
    return jnp.maximum(x, 0.0) + jnp.log(1.0 + jnp.exp(-jnp.abs(x)))


def _ln_residual(h, y, gate, g, b, alpha):
    v = alpha * h + gate * y
    mu = jnp.mean(v, axis=-1, keepdims=True)
    d = v - mu
    var = jnp.mean(d * d, axis=-1, keepdims=True)
    return d * lax.rsqrt(var + LN_EPS) * g + b


def _mod_row(ref, rid):
    return ref[0, pl.ds(rid, 1), :]


def _mods_body(c_ref, w_ref, b_ref, o_ref):
    s = _silu(c_ref[...])
    o_ref[0] = jnp.dot(s, w_ref[0], precision=HIGHEST, preferred_element_type=F32) + b_ref[0]


def _mods_call(cvec, ada_w, ada_b):
    depth, d, n = ada_w.shape
    tn = 1024
    return pl.pallas_call(
        _mods_body,
        grid=(depth, n // tn),
        in_specs=[pl.BlockSpec((8, d), lambda l, j: (0, 0)),
                  pl.BlockSpec((1, d, tn), lambda l, j: (l, 0, j)),
                  pl.BlockSpec((1, 1, tn), lambda l, j: (l, 0, j))],
        out_specs=pl.BlockSpec((1, 8, tn), lambda l, j: (l, 0, j)),
        out_shape=jax.ShapeDtypeStruct((depth, 8, n), F32),
        compiler_params=_cparams(("parallel", "parallel")),
        name="mods",
    )(cvec, ada_w, ada_b.reshape(depth, 1, n))


def _inproj_body(*refs, row_base, tiles_per_row, tn, with_dt):
    if with_dt:
        h_ref, sh_ref, sc_ref, w_ref, wdt_ref, o_ref, odt_ref = refs
    else:
        h_ref, sh_ref, sc_ref, w_ref, o_ref = refs
    rid = row_base + pl.program_id(0) // tiles_per_row
    u = (h_ref[...] * (1.0 + _mod_row(sc_ref, rid)) + _mod_row(sh_ref, rid)).astype(BF16)
    for n0 in range(0, w_ref.shape[1], tn):
        o_ref[:, n0:n0 + tn] = jnp.dot(u, w_ref[:, n0:n0 + tn], preferred_element_type=F32).astype(o_ref.dtype)
    if with_dt:
        odt_ref[...] = jnp.dot(u, wdt_ref[...], preferred_element_type=F32)


def _inproj_call(h, mods, layer, w, w_dt, *, tm, row_base, tiles_per_row, name):
    t, d = h.shape
    n = w.shape[1]
    with_dt = w_dt is not None
    body = functools.partial(_inproj_body, row_base=row_base, tiles_per_row=tiles_per_row, tn=TN_PROJ,
                             with_dt=with_dt)
    resident = lambda a: pl.BlockSpec(a.shape, lambda i: (0, 0), pipeline_mode=pl.Buffered(1))
    in_specs = [pl.BlockSpec((tm, d), lambda i: (i, 0)),
                pl.BlockSpec((1, 8, d), lambda i: (layer, 0, 0)),
                pl.BlockSpec((1, 8, d), lambda i: (layer, 0, 1)),
                resident(w)]
    out_specs = [pl.BlockSpec((tm, n), lambda i: (i, 0))]
    out_shape = [jax.ShapeDtypeStruct((t, n), BF16)]
    args = [h, mods, mods, w]
    if with_dt:
        in_specs.append(resident(w_dt))
        out_specs.append(pl.BlockSpec((tm, w_dt.shape[1]), lambda i: (i, 0)))
        out_shape.append(jax.ShapeDtypeStruct((t, w_dt.shape[1]), F32))
        args.append(w_dt)
    outs = pl.pallas_call(
        body,
        grid=(t // tm,),
        in_specs=in_specs,
        out_specs=out_specs,
        out_shape=out_shape,
        compiler_params=_cparams(("parallel",)),
        name=name,
    )(*args)
    return (outs[0], outs[1]) if with_dt else outs[0]


def _rpb_onehot():
    qc = np.arange(GRID_W)[:, None]
    kc = np.arange(GRID_W)[None, :]
    c0 = np.clip(qc - NA_COLS // 2, 0, GRID_W - NA_COLS)
    inside = (kc >= c0) & (kc < c0 + NA_COLS)
    dc = kc - qc + NA_COLS - 1
    oh = np.zeros((LANES, GRID_W, GRID_W), np.float32)
    for d in range(2 * NA_COLS - 1):
        oh[d] = ((dc == d) & inside).astype(np.float32)
    mask = np.where(inside, 0.0, NEG).astype(np.float32)
    return oh.reshape(LANES, GRID_W * GRID_W), mask.reshape(1, GRID_W * GRID_W)


def _rpb_body(r_ref, oh_ref, m_ref, o_ref):
    o_ref[...] = jnp.dot(r_ref[...], oh_ref[...], precision=HIGHEST,
                         preferred_element_type=F32) + m_ref[...]


def _rpb_table(rpb):
    h = rpb.shape[0]
    nd = 2 * NA_ROWS - 1
    oh, mask = _rpb_onehot()
    r2 = jnp.pad(rpb.reshape(h * nd, 2 * NA_COLS - 1), ((0, 0), (0, LANES - 2 * NA_COLS + 1)))
    e = pl.pallas_call(
        _rpb_body,
        out_shape=jax.ShapeDtypeStruct((h * nd, GRID_W * GRID_W), F32),
        name="rpb_table",
    )(r2, jnp.asarray(oh), jnp.asarray(mask))
    e = jnp.swapaxes(e.reshape(h, nd, GRID_W, GRID_W), 2, 3)
    e = jnp.pad(e, ((0, 0), (1 - NA_DMIN, NA_DMIN + NA_ND - nd), (0, 0), (0, 0)),
                constant_values=NEG)
    first, second = e[:, 1:], e[:, :-1]
    neg = jnp.full_like(first, NEG)
    return jnp.concatenate([jnp.concatenate([first, second], axis=-1),
                            jnp.concatenate([first, neg], axis=-1),
                            jnp.concatenate([neg, second], axis=-1),
                            jnp.concatenate([neg, neg], axis=-1)], axis=1)


def _natten_window(r_first, rows):
    return jnp.clip(r_first - NA_ROWS // 2, 0, rows - NA_WIN)


def _natten_scores(q, r_first, k_ref, kc_ref, e_ref, s_scr, rows):
    nk = NA_WIN * GRID_W
    ws = _natten_window(r_first, rows)
    kwin = k_ref[pl.ds(pl.multiple_of(ws * GRID_W, LANES), nk), :]
    kc = kc_ref[...]
    idx = []
    for g in range(NA_QR // 2):
        r = r_first + 2 * g
        r0a = jnp.clip(r - NA_ROWS // 2, 0, rows - NA_ROWS)
        r0b = jnp.clip(r + 1 - NA_ROWS // 2, 0, rows - NA_ROWS)
        col = []
        for t in range(NA_WIN):
            kr = ws + t
            d = kr - r + (NA_ROWS - 1)
            out_first = jnp.logical_or(kr < r0a, kr >= r0a + NA_ROWS).astype(jnp.int32)
            out_second = jnp.logical_or(kr < r0b, kr >= r0b + NA_ROWS).astype(jnp.int32)
            col.append((2 * out_first + out_second) * NA_ND + d - NA_DMIN)
        idx.append(col)
    lo = lax.broadcasted_iota(jnp.int32, (1, LANES), 1) < NA_HEAD_DIM
    scale = jnp.asarray(NA_HEAD_DIM ** -0.5, BF16)
    nt = (((1,), (1,)), ((), ()))
    for a in range(2):
        sel = lo if a == 0 else jnp.logical_not(lo)
        qa = jnp.where(sel, q, jnp.zeros_like(q)) * scale
        bias = jnp.concatenate(
            [jnp.concatenate([e_ref[a, idx[g][t]] for g in range(NA_QR // 2)], axis=1) for t in range(NA_WIN)],
            axis=0)
        s_scr[a, :nk, :] = lax.dot_general(kwin, qa, nt, preferred_element_type=F32) + bias
        s_scr[a, nk:, :] = lax.dot_general(kc, qa, nt, preferred_element_type=F32)


def _natten_softmax_pv(r_first, s_scr, p_scr, vt_ref, vct_ref, rows):
    nk = NA_WIN * GRID_W
    nkb = s_scr.shape[1] // GRID_W
    ws = _natten_window(r_first, rows)
    vt_win = vt_ref[:, pl.ds(pl.multiple_of(ws * GRID_W, LANES), nk)]
    vct = vct_ref[...]
    outs = []
    for a in range(2):
        linv = []
        for g in range(NA_QR // 2):
            cols = slice(g * LANES, (g + 1) * LANES)
            mx = None
            for t in range(nkb):
                blk = s_scr[a, t * GRID_W:(t + 1) * GRID_W, cols]
                mx = blk if mx is None else jnp.maximum(mx, blk)
            m = jnp.max(mx, axis=0, keepdims=True)
            ls = None
            for t in range(nkb):
                p = jnp.exp(s_scr[a, t * GRID_W:(t + 1) * GRID_W, cols] - m)
                p_scr[a, t * GRID_W:(t + 1) * GRID_W, cols] = p.astype(BF16)
                ls = p if ls is None else ls + p
            linv.append(1.0 / jnp.sum(ls, axis=0, keepdims=True))
        o_t = (jnp.dot(vt_win, p_scr[a, :nk, :], preferred_element_type=F32)
               + jnp.dot(vct, p_scr[a, nk:, :], preferred_element_type=F32))
        outs.append(o_t * jnp.concatenate(linv, axis=-1))
    o_t = jnp.concatenate([outs[0][:NA_HEAD_DIM], outs[1][NA_HEAD_DIM:]], axis=0)
    return o_t.T


def _natten_body(q0_ref, q_ref, qn_ref, k_ref, vt_ref, kc_ref, vct_ref, e_ref, o_ref,
                 sa_scr, sb_scr, pa_scr, pb_scr, *, rows, nrb):
    nq = NA_QR * GRID_W
    i = pl.program_id(2)
    b0 = NA_BLOCKS * i
    bufs = ((sa_scr, pa_scr), (sb_scr, pb_scr))

    @pl.when(i == 0)
    def _():
        _natten_scores(q0_ref[...], 0, k_ref, kc_ref, e_ref, sa_scr, rows)

    for k in range(NA_BLOCKS):
        s_cur, p_cur = bufs[k % 2]
        s_nxt = bufs[(k + 1) % 2][0]
        if k + 1 < NA_BLOCKS:
            q_nxt, b_nxt = q_ref[(k + 1) * nq:(k + 2) * nq, :], b0 + k + 1
        else:
            q_nxt, b_nxt = qn_ref[...], jnp.minimum(b0 + NA_BLOCKS, nrb - 1)
        _natten_scores(q_nxt, b_nxt * NA_QR, k_ref, kc_ref, e_ref, s_nxt, rows)
        o_ref[k * nq:(k + 1) * nq, :] = _natten_softmax_pv(
            (b0 + k) * NA_QR, s_cur, p_cur, vt_ref, vct_ref, rows).astype(o_ref.dtype)


def _natten_call(proj_l, proj_c, e2, batch, seq, lc):
    d = e2.shape[0] * NA_HEAD_DIM
    npair = d // LANES
    rows = seq // GRID_W
    nrb = rows // NA_QR
    nq = NA_QR * GRID_W
    nkeys = NA_WIN * GRID_W + lc
    vt_l = jnp.swapaxes(proj_l[:, 2 * d:3 * d].reshape(batch, seq, d), 1, 2).reshape(batch * d, seq)
    vt_c = jnp.swapaxes(proj_c[:, 2 * d:3 * d].reshape(batch, lc, d), 1, 2).reshape(batch * d, lc)
    body = functools.partial(_natten_body, rows=rows, nrb=nrb)
    nst = nrb // NA_BLOCKS
    return pl.pallas_call(
        body,
        grid=(batch, npair, nst),
        in_specs=[pl.BlockSpec((nq, LANES), lambda b, p, i: (b * nrb, p)),
                  pl.BlockSpec((NA_BLOCKS * nq, LANES), lambda b, p, i: (b * nst + i, p)),
                  pl.BlockSpec((nq, LANES),
                               lambda b, p, i: (b * nrb + jnp.minimum(NA_BLOCKS * (i + 1), nrb - 1), p)),
                  pl.BlockSpec((seq, LANES), lambda b, p, i: (b, npair + p)),
                  pl.BlockSpec((LANES, seq), lambda b, p, i: (b * npair + p, 0)),
                  pl.BlockSpec((lc, LANES), lambda b, p, i: (b, npair + p)),
                  pl.BlockSpec((LANES, lc), lambda b, p, i: (b * npair + p, 0)),
                  pl.BlockSpec((2, e2.shape[1], GRID_W, LANES), lambda b, p, i: (p, 0, 0, 0))],
        out_specs=pl.BlockSpec((NA_BLOCKS * nq, LANES), lambda b, p, i: (b * nst + i, p)),
        out_shape=jax.ShapeDtypeStruct((batch * seq, d), BF16),
        scratch_shapes=[pltpu.VMEM((2, nkeys, nq), F32), pltpu.VMEM((2, nkeys, nq), F32),
                        pltpu.VMEM((2, nkeys, nq), BF16), pltpu.VMEM((2, nkeys, nq), BF16)],
        compiler_params=_cparams(("parallel", "parallel", "arbitrary")),
        name="natten",
    )(proj_l, proj_l, proj_l, proj_l, vt_l, proj_c, vt_c, e2)


def _ctxattn_body(q_ref, k_ref, v_ref, o_ref):
    q = q_ref[...]
    k = k_ref[...]
    v = v_ref[...]
    lane = lax.broadcasted_iota(jnp.int32, (1, LANES), 1)
    lo = lane < NA_HEAD_DIM
    scale = NA_HEAD_DIM ** -0.5
    outs = []
    for a in range(2):
        sel = lo if a == 0 else jnp.logical_not(lo)
        qa = jnp.where(sel, q, jnp.zeros_like(q)) * jnp.asarray(scale, BF16)
        s = lax.dot_general(qa, k, (((1,), (1,)), ((), ())), preferred_element_type=F32)
        m = jnp.max(s, axis=-1, keepdims=True)
        p = jnp.exp(s - m)
        l = jnp.sum(p, axis=-1, keepdims=True)
        outs.append(jnp.dot(p.astype(BF16), v, preferred_element_type=F32) / l)
    o_ref[...] = jnp.where(lo, outs[0], outs[1]).astype(o_ref.dtype)


def _ctxattn_call(proj_c, batch, lc, d):
    npair = d // LANES
    return pl.pallas_call(
        _ctxattn_body,
        grid=(batch, npair),
        in_specs=[pl.BlockSpec((lc, LANES), lambda b, p: (b, p)),
                  pl.BlockSpec((lc, LANES), lambda b, p: (b, npair + p)),
                  pl.BlockSpec((lc, LANES), lambda b, p: (b, 2 * npair + p))],
        out_specs=pl.BlockSpec((lc, LANES), lambda b, p: (b, p)),
        out_shape=jax.ShapeDtypeStruct((batch * lc, d), BF16),
        compiler_params=_cparams(("parallel", "parallel")),
        name="ctx_attn",
    )(proj_c, proj_c, proj_c)


def _dwconv_silu_body(xp_ref, x_ref, xn_ref, w_ref, b_ref, o_ref, *, tiles_per_seq):
    i = pl.program_id(0)
    k = w_ref.shape[0]
    half = k // 2
    tb = x_ref.shape[0]
    hr = xp_ref.shape[0]
    first = (i % tiles_per_seq) == 0
    last = (i % tiles_per_seq) == tiles_per_seq - 1
    prev = jnp.where(first, 0.0, xp_ref[...].astype(F32))
    nxt = jnp.where(last, 0.0, xn_ref[...].astype(F32))
    ext = jnp.concatenate([prev, x_ref[...].astype(F32), nxt], axis=0)
    acc = jnp.zeros((tb, x_ref.shape[1]), F32) + b_ref[...]
    for t in range(k):
        off = hr - half + t
        acc = acc + ext[off:off + tb, :] * w_ref[pl.ds(t, 1), :]
    o_ref[...] = _silu(acc).astype(o_ref.dtype)


def _dwconv_silu_call(proj, col0, width, w, b, seq, *, tb, tc):
    t = proj.shape[0]
    hr = BF16_SUBLANES
    nrt = t // tb
    cb0 = col0 // tc
    body = functools.partial(_dwconv_silu_body, tiles_per_seq=seq // tb)
    return pl.pallas_call(
        body,
        grid=(nrt, width // tc),
        in_specs=[pl.BlockSpec((hr, tc), lambda i, j: (jnp.maximum(i * (tb // hr) - 1, 0), cb0 + j)),
                  pl.BlockSpec((tb, tc), lambda i, j: (i, cb0 + j)),
                  pl.BlockSpec((hr, tc), lambda i, j: (jnp.minimum((i + 1) * (tb // hr), t // hr - 1), cb0 + j)),
                  pl.BlockSpec((w.shape[0], tc), lambda i, j: (0, j)),
                  pl.BlockSpec((1, tc), lambda i, j: (0, j))],
        out_specs=pl.BlockSpec((tb, tc), lambda i, j: (i, j)),
        out_shape=jax.ShapeDtypeStruct((t, width), BF16),
        compiler_params=_cparams(("parallel", "parallel")),
        name="dwconv_silu",
    )(proj, proj, proj, w, b.reshape(1, width))


def _ssd_prep_body(dt_ref, dtb_ref, alog_ref, dtt_ref, cum_ref, cumt_ref, *, chunks):
    q = SSM_CHUNK
    fwd = pl.program_id(1) == 0
    ii = lax.broadcasted_iota(jnp.int32, (q, q), 0)
    jj = lax.broadcasted_iota(jnp.int32, (q, q), 1)
    tri = ((jj - ii) * jnp.where(fwd, 1, -1) <= 0).astype(F32)
    a = -jnp.exp(alog_ref[0])
    for c in range(chunks):
        rws = slice(c * q, (c + 1) * q)
        dtv = _softplus(dt_ref[rws, :] + dtb_ref[0])
        cum = jnp.dot(tri, dtv * a, precision=HIGHEST, preferred_element_type=F32)
        dtt_ref[0, rws, :] = dtv.T
        cum_ref[0, rws, :] = cum
        cumt_ref[0, rws, :] = cum.T


def _ssd_prep_call(dt, dtb, alog, batch, seq):
    nc = seq // SSM_CHUNK
    chunks = min(SSD_PREP_CHUNKS, nc)
    rows = chunks * SSM_CHUNK
    t = batch * seq
    out = jax.ShapeDtypeStruct((2, t, LANES), F32)
    ospec = pl.BlockSpec((1, rows, LANES), lambda i, d: (d, i, 0))
    return pl.pallas_call(
        functools.partial(_ssd_prep_body, chunks=chunks),
        grid=(t // rows, 2),
        in_specs=[pl.BlockSpec((rows, LANES), lambda i, d: (i, d)),
                  pl.BlockSpec((1, 1, LANES), lambda i, d: (d, 0, 0)),
                  pl.BlockSpec((1, 1, LANES), lambda i, d: (d, 0, 0))],
        out_specs=[ospec, ospec, ospec],
        out_shape=[out, out, out],
        compiler_params=_cparams(("parallel", "parallel")),
        name="ssd_prep",
    )(dt, dtb, alog)


def _ssd_body(x_ref, b_ref, c_ref, dtt_ref, cum_ref, cumt_ref, h0_ref, y_ref, hT_ref, st_scr, *, nsteps):
    q = SSM_CHUNK
    dr = pl.program_id(1)
    s = pl.program_id(2)

    @pl.when(s == 0)
    def _():
        st_scr[...] = h0_ref[0, 0]

    fwd = dr == 0
    cum = cum_ref[0]
    cum_t = cumt_ref[0]
    dt_t = dtt_ref[0]
    ii = lax.broadcasted_iota(jnp.int32, (q, q), 0)
    jj = lax.broadcasted_iota(jnp.int32, (q, q), 1)
    tri = (jj - ii) * jnp.where(fwd, 1, -1) <= 0
    tot = jnp.where(fwd, cum[q - 1:q, :], cum[0:1, :])
    tot_c = jnp.where(fwd, cum_t[:, q - 1:q], cum_t[:, 0:1])
    upd_t = dt_t * jnp.exp(tot_c - cum_t)
    lane = lax.broadcasted_iota(jnp.int32, (1, LANES), 1)
    lo = lane < SSM_HEAD_DIM
    heads_per_group = SSM_HEADS // SSM_GROUPS
    for g in range(SSM_GROUPS):
        bg = b_ref[:, g * SSM_STATE:(g + 1) * SSM_STATE]
        cg = c_ref[:, g * SSM_STATE:(g + 1) * SSM_STATE]
        cb = lax.dot_general(cg, bg, (((1,), (1,)), ((), ())), preferred_element_type=F32)
        bg_t = bg.astype(F32).T
        for hp in range(g * heads_per_group // 2, (g + 1) * heads_per_group // 2):
            ha, hb = 2 * hp, 2 * hp + 1
            xb = x_ref[:, hp * LANES:(hp + 1) * LANES]
            cca = jnp.broadcast_to(cum[:, ha:ha + 1], (q, q))
            ccb = jnp.broadcast_to(cum[:, hb:hb + 1], (q, q))
            l_a = jnp.exp(jnp.where(tri, cca - cum_t[ha:ha + 1, :], NEG))
            l_b = jnp.exp(jnp.where(tri, ccb - cum_t[hb:hb + 1, :], NEG))
            m_a = (cb * l_a * dt_t[ha:ha + 1, :]).astype(BF16)
            m_b = (cb * l_b * dt_t[hb:hb + 1, :]).astype(BF16)
            y_intra = jnp.where(lo, jnp.dot(m_a, xb, preferred_element_type=F32),
                                jnp.dot(m_b, xb, preferred_element_type=F32))
            ccp = jnp.where(lo, cca, ccb)
            st = st_scr[hp]
            y_inter = jnp.dot(cg, st.astype(BF16), preferred_element_type=F32) * jnp.exp(ccp)
            y_ref[0, :, hp * LANES:(hp + 1) * LANES] = y_intra + y_inter
            totp = jnp.where(lo, jnp.broadcast_to(tot[:, ha:ha + 1], (1, LANES)),
                             jnp.broadcast_to(tot[:, hb:hb + 1], (1, LANES)))
            bw_a = (bg_t * upd_t[ha:ha + 1, :]).astype(BF16)
            bw_b = (bg_t * upd_t[hb:hb + 1, :]).astype(BF16)
            st_scr[hp] = st * jnp.exp(totp) + jnp.where(lo, jnp.dot(bw_a, xb, preferred_element_type=F32),
                                                        jnp.dot(bw_b, xb, preferred_element_type=F32))

    @pl.when(s == nsteps - 1)
    def _():
        hT_ref[0, 0] = st_scr[...]


def _ssd_call(xbc, dt, dtb, alog, h0, batch, seq):
    inner = SSM_HEADS * SSM_HEAD_DIM
    gn = SSM_GROUPS * SSM_STATE
    nc = seq // SSM_CHUNK
    npair = SSM_HEADS // 2

    def rblk(b, d, s):
        return b * nc + jnp.where(d == 0, s, nc - 1 - s)

    dt_t, cum, cum_t = _ssd_prep_call(dt, dtb, alog, batch, seq)
    pspec = pl.BlockSpec((1, SSM_CHUNK, LANES), lambda b, d, s: (d, rblk(b, d, s), 0))
    body = functools.partial(_ssd_body, nsteps=nc)
    return pl.pallas_call(
        body,
        grid=(batch, 2, nc),
        in_specs=[pl.BlockSpec((SSM_CHUNK, inner), lambda b, d, s: (rblk(b, d, s), 0)),
                  pl.BlockSpec((SSM_CHUNK, gn), lambda b, d, s: (rblk(b, d, s), inner // gn)),
                  pl.BlockSpec((SSM_CHUNK, gn), lambda b, d, s: (rblk(b, d, s), inner // gn + 1)),
                  pspec, pspec, pspec,
                  pl.BlockSpec((1, 1, npair, SSM_STATE, LANES), lambda b, d, s: (b, d, 0, 0, 0))],
        out_specs=[pl.BlockSpec((1, SSM_CHUNK, inner), lambda b, d, s: (d, rblk(b, d, s), 0)),
                   pl.BlockSpec((1, 1, npair, SSM_STATE, LANES), lambda b, d, s: (b, d, 0, 0, 0))],
        out_shape=[jax.ShapeDtypeStruct((2, batch * seq, inner), F32),
                   jax.ShapeDtypeStruct((batch, 2, npair, SSM_STATE, LANES), F32)],
        scratch_shapes=[pltpu.VMEM((npair, SSM_STATE, LANES), F32)],
        compiler_params=_cparams(("parallel", "parallel", "arbitrary")),
        name="ssd",
    )(xbc, xbc, xbc, dt_t, cum, cum_t, h0)


def _evenout_body(attn_ref, y_ref, xs_ref, z_ref, h_ref, gate_ref, dsk_ref, nw_ref, w_ref, lg_ref, lb_ref,
                  o_ref, ssm_scr, *, row_base, tiles_per_row, alpha):
    i = pl.program_id(0)
    rid = row_base + i // tiles_per_row
    inner = xs_ref.shape[1]
    gw = inner // SSM_GROUPS
    z = z_ref[...].astype(F32)
    y = y_ref[0] + y_ref[1] + xs_ref[...].astype(F32) * dsk_ref[...]
    u = y * _silu(z)
    for g in range(SSM_GROUPS):
        ug = u[:, g * gw:(g + 1) * gw]
        ms = jnp.mean(ug * ug, axis=-1, keepdims=True)
        ssm_scr[:, g * gw:(g + 1) * gw] = (ug * lax.rsqrt(ms + RMS_EPS) * nw_ref[:, g * gw:(g + 1) * gw]).astype(BF16)
    d_attn = attn_ref.shape[1]
    acc = jnp.dot(attn_ref[...], w_ref[:d_attn, :], preferred_element_type=F32)
    acc = acc + jnp.dot(ssm_scr[...], w_ref[d_attn:, :], preferred_element_type=F32)
    o_ref[...] = _ln_residual(h_ref[...], acc, _mod_row(gate_ref, rid), lg_ref[0], lb_ref[0], alpha)


def _evenout_call(attn, y2, xbc, proj, z_cb, h, mods, layer, dskip_row, norm_w, w_out, ln_g, ln_b, ln_idx,
                  *, tm, row_base, tiles_per_row, alpha):
    t, d = h.shape
    inner = y2.shape[2]
    body = functools.partial(_evenout_body, row_base=row_base, tiles_per_row=tiles_per_row, alpha=alpha)
    return pl.pallas_call(
        body,
        grid=(t // tm,),
        in_specs=[pl.BlockSpec((tm, d), lambda i: (i, 0)),
                  pl.BlockSpec((2, tm, inner), lambda i: (0, i, 0)),
                  pl.BlockSpec((tm, inner), lambda i: (i, 0)),
                  pl.BlockSpec((tm, inner), lambda i: (i, z_cb)),
                  pl.BlockSpec((tm, d), lambda i: (i, 0)),
                  pl.BlockSpec((1, 8, d), lambda i: (layer, 0, 2)),
                  pl.BlockSpec((1, inner), lambda i: (0, 0)),
                  pl.BlockSpec((1, inner), lambda i: (0, 0)),
                  pl.BlockSpec(w_out.shape, lambda i: (0, 0)),
                  pl.BlockSpec((1, 1, d), lambda i: (ln_idx, 0, 0)),
                  pl.BlockSpec((1, 1, d), lambda i: (ln_idx, 0, 0))],
        out_specs=pl.BlockSpec((tm, d), lambda i: (i, 0)),
        out_shape=jax.ShapeDtypeStruct((t, d), F32),
        scratch_shapes=[pltpu.VMEM((tm, inner), BF16)],
        compiler_params=_cparams(("parallel",)),
        name="even_out",
    )(attn, y2, xbc, proj, h, mods, dskip_row, norm_w, w_out, ln_g, ln_b)


def _hidden_chunks(f):
    half = (f // 2 + MXU_DIM - 1) // MXU_DIM * MXU_DIM
    return ((0, half), (half, f - half))


def _swiglu(u, w1_ref, w3_ref, w2_ref):
    acc = None
    for f0, fc in _hidden_chunks(w1_ref.shape[1]):
        a = jnp.dot(u, w1_ref[:, f0:f0 + fc], preferred_element_type=F32)
        b = jnp.dot(u, w3_ref[:, f0:f0 + fc], preferred_element_type=F32)
        t = (_silu(a) * b).astype(BF16)
        y = jnp.dot(t, w2_ref[f0:f0 + fc, :], preferred_element_type=F32)
        acc = y if acc is None else acc + y
    return acc


def _ffn_body(h_ref, sh_ref, sc_ref, gate_ref, w1_ref, w3_ref, w2_ref, lg_ref, lb_ref, o_ref,
              *, row_base, tiles_per_row, alpha):
    i = pl.program_id(0)
    rid = row_base + i // tiles_per_row
    h = h_ref[...]
    u = (h * (1.0 + _mod_row(sc_ref, rid)) + _mod_row(sh_ref, rid)).astype(BF16)
    y = _swiglu(u, w1_ref, w3_ref, w2_ref)
    o_ref[...] = _ln_residual(h, y, _mod_row(gate_ref, rid), lg_ref[0], lb_ref[0], alpha)


def _ffn_call(h, mods, layer, w1c, w3c, w2c, ln_g, ln_b, ln_idx, *, tm, row_base, tiles_per_row, alpha):
    t, d = h.shape
    body = functools.partial(_ffn_body, row_base=row_base, tiles_per_row=tiles_per_row, alpha=alpha)
    wspec = lambda w: pl.BlockSpec(w.shape, lambda i: (0, 0), pipeline_mode=pl.Buffered(1))
    return pl.pallas_call(
        body,
        grid=(t // tm,),
        in_specs=[pl.BlockSpec((tm, d), lambda i: (i, 0)),
                  pl.BlockSpec((1, 8, d), lambda i: (layer, 0, 3)),
                  pl.BlockSpec((1, 8, d), lambda i: (layer, 0, 4)),
                  pl.BlockSpec((1, 8, d), lambda i: (layer, 0, 5)),
                  wspec(w1c), wspec(w3c), wspec(w2c),
                  pl.BlockSpec((1, 1, d), lambda i: (ln_idx, 0, 0)),
                  pl.BlockSpec((1, 1, d), lambda i: (ln_idx, 0, 0))],
        out_specs=pl.BlockSpec((tm, d), lambda i: (i, 0)),
        out_shape=jax.ShapeDtypeStruct((t, d), F32),
        compiler_params=_cparams(("parallel",)),
        name="ffn",
    )(h, mods, mods, mods, w1c, w3c, w2c, ln_g, ln_b)


def _scout_body(pb_ref, pcp_ref, pc_ref, pcn_ref, php_ref, ph_ref, phn_ref, h_ref, gate_ref, cw_ref, w_ref,
                lg_ref, lb_ref, o_ref, *, row_base, tiles_per_row, tiles_per_seq, alpha):
    i = pl.program_id(0)
    rid = row_base + i // tiles_per_row
    tm = ph_ref.shape[0]
    hr = php_ref.shape[0]
    first = (i % tiles_per_seq) == 0
    last = (i % tiles_per_seq) == tiles_per_seq - 1
    prev = jnp.where(first, 0.0, pcp_ref[...].astype(F32) * php_ref[...].astype(F32))
    cur = pc_ref[...].astype(F32) * ph_ref[...].astype(F32)
    nxt = jnp.where(last, 0.0, pcn_ref[...].astype(F32) * phn_ref[...].astype(F32))
    ext = jnp.concatenate([prev, cur, nxt], axis=0)
    acc = jnp.zeros_like(cur)
    for t in range(SC_CONV):
        off = hr - SC_CONV // 2 + t
        acc = acc + ext[off:off + tm, :] * cw_ref[pl.ds(t, 1), :]
    a = (pb_ref[...].astype(F32) * acc).astype(BF16)
    y = jnp.dot(a, w_ref[...], preferred_element_type=F32)
    o_ref[...] = _ln_residual(h_ref[...], y, _mod_row(gate_ref, rid), lg_ref[0], lb_ref[0], alpha)


def _scout_call(proj, h, mods, layer, conv_w, w_out, ln_g, ln_b, ln_idx, seq,
                *, tm, row_base, tiles_per_row, alpha):
    t, d = h.shape
    hr = BF16_SUBLANES
    r = tm // hr
    nh = t // hr
    body = functools.partial(_scout_body, row_base=row_base, tiles_per_row=tiles_per_row,
                             tiles_per_seq=seq // tm, alpha=alpha)
    prev_map = lambda cb: (lambda i: (jnp.maximum(i * r - 1, 0), cb))
    next_map = lambda cb: (lambda i: (jnp.minimum((i + 1) * r, nh - 1), cb))
    return pl.pallas_call(
        body,
        grid=(t // tm,),
        in_specs=[pl.BlockSpec((tm, d), lambda i: (i, 0)),
                  pl.BlockSpec((hr, d), prev_map(1)),
                  pl.BlockSpec((tm, d), lambda i: (i, 1)),
                  pl.BlockSpec((hr, d), next_map(1)),
                  pl.BlockSpec((hr, d), prev_map(2)),
                  pl.BlockSpec((tm, d), lambda i: (i, 2)),
                  pl.BlockSpec((hr, d), next_map(2)),
                  pl.BlockSpec((tm, d), lambda i: (i, 0)),
                  pl.BlockSpec((1, 8, d), lambda i: (layer, 0, 2)),
                  pl.BlockSpec(conv_w.shape, lambda i: (0, 0)),
                  pl.BlockSpec(w_out.shape, lambda i: (0, 0)),
                  pl.BlockSpec((1, 1, d), lambda i: (ln_idx, 0, 0)),
                  pl.BlockSpec((1, 1, d), lambda i: (ln_idx, 0, 0))],
        out_specs=pl.BlockSpec((tm, d), lambda i: (i, 0)),
        out_shape=jax.ShapeDtypeStruct((t, d), F32),
        compiler_params=_cparams(("parallel",)),
        name="shortconv_out",
    )(proj, proj, proj, proj, proj, proj, proj, h, mods, conv_w, w_out, ln_g, ln_b)


def _router_body(h_ref, sh_ref, sc_ref, wr_ref, u_ref, meta_ref, cnt_ref, cnt_scr, *, row_base, tiles_per_row):
    i = pl.program_id(0)
    rid = row_base + i // tiles_per_row
    tm = h_ref.shape[0]

    @pl.when(i == 0)
    def _():
        cnt_scr[...] = jnp.zeros_like(cnt_scr)

    u = h_ref[...] * (1.0 + _mod_row(sc_ref, rid)) + _mod_row(sh_ref, rid)
    u_ref[...] = u
    logits = jnp.dot(u, wr_ref[...], precision=HIGHEST, preferred_element_type=F32)
    lane = lax.broadcasted_iota(jnp.int32, (tm, LANES), 1).astype(F32)
    lg = jnp.where(lane < N_EXPERTS, logits, NEG)
    v1 = jnp.max(lg, axis=-1, keepdims=True)
    e1 = jnp.min(jnp.where(lg == v1, lane, float(LANES)), axis=-1, keepdims=True)
    lg2 = jnp.where(lane == e1, 2 * NEG, lg)
    v2 = jnp.max(lg2, axis=-1, keepdims=True)
    e2 = jnp.min(jnp.where(lg2 == v2, lane, float(LANES)), axis=-1, keepdims=True)
    g2 = 1.0 / (1.0 + jnp.exp(v1 - v2))
    g1 = 1.0 - g2
    oh1 = (lane == e1).astype(F32)
    oh2 = (lane == e2).astype(F32)
    both = (oh1 + oh2).astype(BF16)
    ii = lax.broadcasted_iota(jnp.int32, (tm, tm), 0)
    jj = lax.broadcasted_iota(jnp.int32, (tm, tm), 1)
    strict = (jj < ii).astype(BF16)
    before = jnp.dot(strict, both, preferred_element_type=F32) + cnt_scr[...]
    p1 = jnp.sum(before * oh1, axis=-1, keepdims=True)
    p2 = jnp.sum(before * oh2, axis=-1, keepdims=True)
    cnt_scr[...] = cnt_scr[...] + jnp.sum(oh1 + oh2, axis=0, keepdims=True)
    meta = jnp.where(lane == 0, e1,
           jnp.where(lane == 1, e2,
           jnp.where(lane == 2, p1,
           jnp.where(lane == 3, p2,
           jnp.where(lane == 4, g1,
           jnp.where(lane == 5, g2, 0.0))))))
    meta_ref[...] = meta
    cnt_ref[...] = jnp.broadcast_to(cnt_scr[...], cnt_ref.shape)


def _router_call(h, mods, layer, wr_pad, *, tm, row_base, tiles_per_row):
    t, d = h.shape
    body = functools.partial(_router_body, row_base=row_base, tiles_per_row=tiles_per_row)
    return pl.pallas_call(
        body,
        grid=(t // tm,),
        in_specs=[pl.BlockSpec((tm, d), lambda i: (i, 0)),
                  pl.BlockSpec((1, 8, d), lambda i: (layer, 0, 3)),
                  pl.BlockSpec((1, 8, d), lambda i: (layer, 0, 4)),
                  pl.BlockSpec(wr_pad.shape, lambda i: (0, 0))],
        out_specs=[pl.BlockSpec((tm, d), lambda i: (i, 0)),
                   pl.BlockSpec((tm, LANES), lambda i: (i, 0)),
                   pl.BlockSpec((8, LANES), lambda i: (0, 0))],
        out_shape=[jax.ShapeDtypeStruct((t, d), F32),
                   jax.ShapeDtypeStruct((t, LANES), F32),
                   jax.ShapeDtypeStruct((8, LANES), F32)],
        scratch_shapes=[pltpu.VMEM((1, LANES), F32)],
        compiler_params=_cparams(("arbitrary",)),
        name="moe_router",
    )(h, mods, mods, wr_pad)


def _dispatch_body(tv_ref, dest_ref, u_ref, rows_hbm, zbuf, sem, zsem):
    i = pl.program_id(0)
    tm = u_ref.shape[0]
    tile = zbuf.shape[0]

    @pl.when(i == 0)
    def _():
        zbuf[...] = jnp.zeros_like(zbuf)

        def fill_copy(t):
            return pltpu.make_async_copy(zbuf, rows_hbm.at[pl.ds(t * tile, tile)], zsem.at[0])

        def fill(t, c):
            @pl.when(tv_ref[t] < tile)
            def _():
                fill_copy(t).start()
            return c

        def fill_wait(t, c):
            @pl.when(tv_ref[t] < tile)
            def _():
                fill_copy(t).wait()
            return c

        lax.fori_loop(0, tv_ref.shape[0], fill, 0)
        lax.fori_loop(0, tv_ref.shape[0], fill_wait, 0)

    def row_copy(r, k):
        return pltpu.make_async_copy(u_ref.at[pl.ds(r, 1)], rows_hbm.at[pl.ds(dest_ref[0, k, r], 1)], sem.at[k])

    def issue(r, c):
        for k in range(TOP_K):
            row_copy(r, k).start()
        return c

    def drain(r, c):
        for k in range(TOP_K):
            row_copy(r, k).wait()
        return c

    lax.fori_loop(0, tm, issue, 0, unroll=8)
    lax.fori_loop(0, tm, drain, 0, unroll=8)


def _dispatch_call(u, dest_t, n_rows, tile_valid, *, tm):
    t, d = u.shape
    grid_spec = pltpu.PrefetchScalarGridSpec(
        num_scalar_prefetch=1,
        grid=(t // tm,),
        in_specs=[pl.BlockSpec((1, TOP_K, tm), lambda i, tv: (i, 0, 0), memory_space=pltpu.SMEM),
                  pl.BlockSpec((tm, d), lambda i, tv: (i, 0))],
        out_specs=pl.BlockSpec(memory_space=pl.ANY),
        scratch_shapes=[pltpu.VMEM((MOE_TILE, d), F32),
                        pltpu.SemaphoreType.DMA((TOP_K,)),
                        pltpu.SemaphoreType.DMA((1,))],
    )
    return pl.pallas_call(
        _dispatch_body,
        grid_spec=grid_spec,
        out_shape=jax.ShapeDtypeStruct((n_rows, d), F32),
        compiler_params=_cparams(("arbitrary",)),
        name="moe_dispatch",
    )(tile_valid, dest_t, u)


def _moe_body(te_ref, tv_ref, x_ref, w1_hbm, w3_hbm, w2_hbm, o_ref, w1s, w3s, w2s, wsem):
    t = pl.program_id(0)
    e = te_ref[t]
    e_prev = te_ref[jnp.maximum(t - 1, 0)]

    @pl.when(jnp.logical_or(t == 0, e != e_prev))
    def _():
        copies = [pltpu.make_async_copy(w1_hbm.at[e], w1s, wsem.at[0]),
                  pltpu.make_async_copy(w3_hbm.at[e], w3s, wsem.at[1]),
                  pltpu.make_async_copy(w2_hbm.at[e], w2s, wsem.at[2])]
        for cp in copies:
            cp.start()
        for cp in copies:
            cp.wait()

    @pl.when(tv_ref[t] > 0)
    def _():
        o_ref[...] = _swiglu(x_ref[...].astype(BF16), w1s, w3s, w2s)

    @pl.when(tv_ref[t] == 0)
    def _():
        o_ref[...] = jnp.zeros_like(o_ref)


def _moe_call(rows_in, tile_expert, tile_valid, w1c, w3c, w2c):
    n_rows, d = rows_in.shape
    tr = MOE_TILE
    grid_spec = pltpu.PrefetchScalarGridSpec(
        num_scalar_prefetch=2,
        grid=(n_rows // tr,),
        in_specs=[pl.BlockSpec((tr, d), lambda i, te, tv: (i, 0)),
                  pl.BlockSpec(memory_space=pl.ANY),
                  pl.BlockSpec(memory_space=pl.ANY),
                  pl.BlockSpec(memory_space=pl.ANY)],
        out_specs=pl.BlockSpec((tr, d), lambda i, te, tv: (i, 0)),
        scratch_shapes=[pltpu.VMEM(w1c.shape[1:], BF16),
                        pltpu.VMEM(w3c.shape[1:], BF16),
                        pltpu.VMEM(w2c.shape[1:], BF16),
                        pltpu.SemaphoreType.DMA((3,))],
    )
    return pl.pallas_call(
        _moe_body,
        grid_spec=grid_spec,
        out_shape=jax.ShapeDtypeStruct((n_rows, d), F32),
        compiler_params=_cparams(("arbitrary",)),
        name="moe_experts",
    )(tile_expert, tile_valid, rows_in, w1c, w3c, w2c)


def _combine_body(dest_ref, rows_hbm, meta_ref, h_ref, gate_ref, lg_ref, lb_ref, o_ref, rbuf, gsem,
                  *, row_base, tiles_per_row, alpha):
    i = pl.program_id(0)
    rid = row_base + i // tiles_per_row
    tm = h_ref.shape[0]

    def row_copy(r, k):
        return pltpu.make_async_copy(rows_hbm.at[pl.ds(dest_ref[0, k, r], 1)], rbuf.at[k, pl.ds(r, 1)], gsem.at[k])

    def issue(r, c):
        for k in range(TOP_K):
            row_copy(r, k).start()
        return c

    def drain(r, c):
        for k in range(TOP_K):
            row_copy(r, k).wait()
        return c

    lax.fori_loop(0, tm, issue, 0, unroll=8)
    lax.fori_loop(0, tm, drain, 0, unroll=8)
    meta = meta_ref[...]
    y = meta[:, 4:5] * rbuf[0] + meta[:, 5:6] * rbuf[1]
    o_ref[...] = _ln_residual(h_ref[...], y, _mod_row(gate_ref, rid), lg_ref[0], lb_ref[0], alpha)


def _combine_call(dest_t, rows_out, meta, h, mods, layer, ln_g, ln_b, ln_idx, *, tm, row_base, tiles_per_row, alpha):
    t, d = h.shape
    body = functools.partial(_combine_body, row_base=row_base, tiles_per_row=tiles_per_row, alpha=alpha)
    return pl.pallas_call(
        body,
        grid=(t // tm,),
        in_specs=[pl.BlockSpec((1, TOP_K, tm), lambda i: (i, 0, 0), memory_space=pltpu.SMEM),
                  pl.BlockSpec(memory_space=pl.ANY),
                  pl.BlockSpec((tm, LANES), lambda i: (i, 0)),
                  pl.BlockSpec((tm, d), lambda i: (i, 0)),
                  pl.BlockSpec((1, 8, d), lambda i: (layer, 0, 5)),
                  pl.BlockSpec((1, 1, d), lambda i: (ln_idx, 0, 0)),
                  pl.BlockSpec((1, 1, d), lambda i: (ln_idx, 0, 0))],
        out_specs=pl.BlockSpec((tm, d), lambda i: (i, 0)),
        out_shape=jax.ShapeDtypeStruct((t, d), F32),
        scratch_shapes=[pltpu.VMEM((TOP_K, tm, d), F32),
                        pltpu.SemaphoreType.DMA((TOP_K,))],
        compiler_params=_cparams(("arbitrary",)),
        name="moe_combine",
    )(dest_t, rows_out, meta, h, mods, ln_g, ln_b)


def _moe_layer(h, mods, layer, wr_pad, w1c, w3c, w2c, ln_g, ln_b, ln_idx, *, tm, row_base, tiles_per_row, alpha):
    t, d = h.shape
    u, meta, cnt = _router_call(h, mods, layer, wr_pad, tm=tm, row_base=row_base, tiles_per_row=tiles_per_row)
    n_assign = t * TOP_K
    n_tiles = n_assign // MOE_TILE + N_EXPERTS
    counts = cnt[0, :N_EXPERTS].astype(jnp.int32)
    padded = (counts + MOE_TILE - 1) // MOE_TILE * MOE_TILE
    pad_ends = jnp.cumsum(padded)
    pad_starts = pad_ends - padded
    top_e = meta[:, 0:TOP_K].astype(jnp.int32)
    rank = meta[:, 2:2 + TOP_K].astype(jnp.int32)
    dest = pad_starts[top_e] + rank
    dest_t = jnp.swapaxes(dest.reshape(t // tm, tm, TOP_K), 1, 2)
    tile_start = jnp.arange(n_tiles, dtype=jnp.int32) * MOE_TILE
    tile_expert = jnp.minimum(jnp.searchsorted(pad_ends, tile_start, side='right'), N_EXPERTS - 1).astype(jnp.int32)
    seg_end = jnp.where(tile_start < pad_ends[-1], (pad_starts + counts)[tile_expert], 0)
    tile_valid = jnp.clip(seg_end - tile_start, 0, MOE_TILE).astype(jnp.int32)
    rows_in = _dispatch_call(u, dest_t, n_tiles * MOE_TILE, tile_valid, tm=tm)
    rows_out = _moe_call(rows_in, tile_expert, tile_valid, w1c, w3c, w2c)
    return _combine_call(dest_t, rows_out, meta, h, mods, layer, ln_g, ln_b, ln_idx,
                         tm=tm, row_base=row_base, tiles_per_row=tiles_per_row, alpha=alpha)


def kernel(x, c, ctx, c_ctx, ada_w, ada_b, ln_g, ln_b, even_w_in, na_rpb, ssm_conv_w, ssm_conv_b, ssm_dt_bias,
           ssm_a_log, ssm_d, ssm_norm_w, even_w_out, ffn_w1, ffn_w3, ffn_w2, sc_w_in, sc_conv_w, sc_w_out,
           moe_router, moe_w1, moe_w3, moe_w2):
    batch, seq, d = x.shape
    lc = ctx.shape[1]
    depth = ada_w.shape[0]
    alpha = (2 * depth) ** 0.25
    inner = SSM_HEADS * SSM_HEAD_DIM
    gn = SSM_GROUPS * SSM_STATE
    n_main = 3 * d + inner + inner + 2 * gn
    assert batch + 1 <= 8

    cvec = jnp.zeros((8, d), F32).at[:batch].set(c).at[batch].set(c_ctx)
    mods = _mods_call(cvec, ada_w, ada_b)
    lng = ln_g.reshape(depth * 2, 1, d)
    lnb = ln_b.reshape(depth * 2, 1, d)

    h_lat = x.reshape(batch * seq, d)
    h_ctx = ctx.reshape(batch * lc, d)
    lat = dict(tm=TM_MLP, row_base=0, tiles_per_row=seq // TM_MLP, alpha=alpha)
    cx = dict(tm=lc, row_base=batch, tiles_per_row=1 << 20, alpha=alpha)

    for i in range(depth):
        j = i // 2
        ctx_live = any(m % 2 == 0 for m in range(i + 1, depth))
        if i % 2 == 0:
            w_in = even_w_in[j]
            w_main = w_in[:, :n_main].astype(BF16)
            w_dt = jnp.zeros((d, 2 * LANES), F32)
            w_dt = w_dt.at[:, :SSM_HEADS].set(w_in[:, n_main:n_main + SSM_HEADS])
            w_dt = w_dt.at[:, LANES:LANES + SSM_HEADS].set(w_in[:, n_main + SSM_HEADS:]).astype(BF16)
            pad16 = ((0, 0), (0, 0), (0, LANES - SSM_HEADS))
            dtb = jnp.pad(ssm_dt_bias[j][:, None, :], pad16)
            alog = jnp.pad(ssm_a_log[j][:, None, :], pad16)
            dskip = jnp.repeat(ssm_d[j], SSM_HEAD_DIM)[None, :]
            norm_w = ssm_norm_w[j][None, :]
            w_out = even_w_out[j].astype(BF16)
            e2 = _rpb_table(na_rpb[j])

            proj_l, dt_l = _inproj_call(h_lat, mods, i, w_main, w_dt, tm=TM_PROJ, row_base=0,
                                        tiles_per_row=seq // TM_PROJ, name="even_in_lat")
            proj_c, dt_c = _inproj_call(h_ctx, mods, i, w_main, w_dt, tm=lc, row_base=batch,
                                        tiles_per_row=1 << 20, name="even_in_ctx")

            attn_l = _natten_call(proj_l, proj_c, e2, batch, seq, lc)
            xbc_l = _dwconv_silu_call(proj_l, 3 * d + inner, inner + 2 * gn, ssm_conv_w[j], ssm_conv_b[j], seq,
                                      tb=512, tc=512)
            xbc_c = _dwconv_silu_call(proj_c, 3 * d + inner, inner + 2 * gn, ssm_conv_w[j], ssm_conv_b[j], lc,
                                      tb=lc, tc=512)
            h0 = jnp.zeros((batch, 2, SSM_HEADS // 2, SSM_STATE, LANES), F32)
            y_c, h_c = _ssd_call(xbc_c, dt_c, dtb, alog, h0, batch, lc)
            y_l, _ = _ssd_call(xbc_l, dt_l, dtb, alog, h_c, batch, seq)

            z_cb = 3 * d // inner
            h_lat = _evenout_call(attn_l, y_l, xbc_l, proj_l, z_cb, h_lat, mods, i, dskip, norm_w, w_out,
                                  lng, lnb, 2 * i, **lat)
            w1c = ffn_w1[j].astype(BF16)
            w3c = ffn_w3[j].astype(BF16)
            w2c = ffn_w2[j].astype(BF16)
            h_lat = _ffn_call(h_lat, mods, i, w1c, w3c, w2c, lng, lnb, 2 * i + 1, **lat)
            if ctx_live:
                attn_c = _ctxattn_call(proj_c, batch, lc, d)
                h_ctx = _evenout_call(attn_c, y_c, xbc_c, proj_c, z_cb, h_ctx, mods, i, dskip, norm_w, w_out,
                                      lng, lnb, 2 * i, **cx)
                h_ctx = _ffn_call(h_ctx, mods, i, w1c, w3c, w2c, lng, lnb, 2 * i + 1, **cx)
        else:
            w_in = sc_w_in[j].astype(BF16)
            w_out = sc_w_out[j].astype(BF16)
            wr_pad = jnp.pad(moe_router[j], ((0, 0), (0, LANES - N_EXPERTS)))
            w1c = moe_w1[j].astype(BF16)
            w3c = moe_w3[j].astype(BF16)
            w2c = moe_w2[j].astype(BF16)
            proj_l = _inproj_call(h_lat, mods, i, w_in, None, tm=TM_PROJ, row_base=0,
                                  tiles_per_row=seq // TM_PROJ, name="odd_in_lat")
            h_lat = _scout_call(proj_l, h_lat, mods, i, sc_conv_w[j], w_out, lng, lnb, 2 * i, seq, **lat)
            h_lat = _moe_layer(h_lat, mods, i, wr_pad, w1c, w3c, w2c, lng, lnb, 2 * i + 1, **lat)
            if ctx_live:
                proj_c = _inproj_call(h_ctx, mods, i, w_in, None, tm=lc, row_base=batch,
                                      tiles_per_row=1 << 20, name="odd_in_ctx")
                h_ctx = _scout_call(proj_c, h_ctx, mods, i, sc_conv_w[j], w_out, lng, lnb, 2 * i, lc, **cx)
                h_ctx = _moe_layer(h_ctx, mods, i, wr_pad, w1c, w3c, w2c, lng, lnb, 2 * i + 1, **cx)
    return h_lat.reshape(batch, seq, d)
```

```python
import functools

import numpy as np
import jax
import jax.numpy as jnp
from jax import lax
from jax.experimental import pallas as pl
from jax.experimental.pallas import tpu as pltpu

F32 = jnp.float32
BF16 = jnp.bfloat16
HIGHEST = lax.Precision.HIGHEST

GRID_W = 64
NA_HEAD_DIM = 64
NA_ROWS = 8
NA_COLS = 16
SSM_HEAD_DIM = 64
SSM_HEADS = 16
SSM_GROUPS = 4
SSM_STATE = 128
SSM_CONV = 5
SSM_CHUNK = 128
SC_CONV = 3
N_EXPERTS = 8
TOP_K = 2
LN_EPS = 1e-5
RMS_EPS = 1e-5

LANES = 128
MXU_DIM = 256
BF16_SUBLANES = 16
NEG = -1e30
VMEM_LIMIT = 56 * 1024 * 1024

TM_PROJ = 512
TN_PROJ = 1024
TM_MLP = 512
NA_QR = 4
NA_WIN = NA_QR + NA_ROWS
NA_DMIN = 1 - NA_QR
NA_ND = 2 * (NA_QR + NA_ROWS - 1)
NA_BLOCKS = 4
SSD_PREP_CHUNKS = 16
MOE_TILE = 512


def _cparams(sem, vmem=None):
    return pltpu.CompilerParams(dimension_semantics=sem, vmem_limit_bytes=vmem or VMEM_LIMIT)


def _silu(x):
    return x * jax.nn.sigmoid(x)


def _softplus(x):
    return jnp.maximum(x, 0.0) + jnp.log(1.0 + jnp.exp(-jnp.abs(x)))


def _ln_residual(h, y, gate, g, b, alpha):
    v = alpha * h + gate * y
    mu = jnp.mean(v, axis=-1, keepdims=True)
    d = v - mu
    var = jnp.mean(d * d, axis=-1, keepdims=True)
    return d * lax.rsqrt(var + LN_EPS) * g + b


def _mod_row(ref, rid):
    return ref[0, pl.ds(rid, 1), :]


def _mods_body(c_ref, w_ref, b_ref, o_ref):
    s = _silu(c_ref[...])
    o_ref[0] = jnp.dot(s, w_ref[0], precision=HIGHEST, preferred_element_type=F32) + b_ref[0]


def _mods_call(cvec, ada_w, ada_b):
    depth, d, n = ada_w.shape
    tn = 1024
    return pl.pallas_call(
        _mods_body,
        grid=(depth, n // tn),
        in_specs=[pl.BlockSpec((8, d), lambda l, j: (0, 0)),
                  pl.BlockSpec((1, d, tn), lambda l, j: (l, 0, j)),
                  pl.BlockSpec((1, 1, tn), lambda l, j: (l, 0, j))],
        out_specs=pl.BlockSpec((1, 8, tn), lambda l, j: (l, 0, j)),
        out_shape=jax.ShapeDtypeStruct((depth, 8, n), F32),
        compiler_params=_cparams(("parallel", "parallel")),
        name="mods",
    )(cvec, ada_w, ada_b.reshape(depth, 1, n))


def _inproj_body(*refs, row_base, tiles_per_row, tn, even):
    if even:
        h_ref, sh_ref, sc_ref, w_ref, wvt_ref, wdt_ref, o_ref, ovt_ref, odt_ref = refs
    else:
        h_ref, sh_ref, sc_ref, w_ref, o_ref = refs
    rid = row_base + pl.program_id(0) // tiles_per_row
    u = (h_ref[...] * (1.0 + _mod_row(sc_ref, rid)) + _mod_row(sh_ref, rid)).astype(BF16)
    for n0 in range(0, w_ref.shape[1], tn):
        o_ref[:, n0:n0 + tn] = jnp.dot(u, w_ref[:, n0:n0 + tn], preferred_element_type=F32).astype(o_ref.dtype)
    if even:
        ovt_ref[...] = lax.dot_general(wvt_ref[...], u, (((1,), (1,)), ((), ())),
                                       preferred_element_type=F32).astype(ovt_ref.dtype)
        odt_ref[...] = jnp.dot(u, wdt_ref[...], preferred_element_type=F32)


def _inproj_call(h, mods, layer, w, w_vt, w_dt, seq, *, tm, row_base, tiles_per_row, name):
    t, d = h.shape
    n = w.shape[1]
    even = w_vt is not None
    body = functools.partial(_inproj_body, row_base=row_base, tiles_per_row=tiles_per_row, tn=TN_PROJ, even=even)
    resident = lambda a: pl.BlockSpec(a.shape, lambda i: (0, 0), pipeline_mode=pl.Buffered(1))
    in_specs = [pl.BlockSpec((tm, d), lambda i: (i, 0)),
                pl.BlockSpec((1, 8, d), lambda i: (layer, 0, 0)),
                pl.BlockSpec((1, 8, d), lambda i: (layer, 0, 1)),
                resident(w)]
    out_specs = [pl.BlockSpec((tm, n), lambda i: (i, 0))]
    out_shape = [jax.ShapeDtypeStruct((t, n), BF16)]
    args = [h, mods, mods, w]
    if even:
        dv = w_vt.shape[0]
        tps = seq // tm
        in_specs += [resident(w_vt), resident(w_dt)]
        out_specs += [pl.BlockSpec((dv, tm), lambda i: (i // tps, i % tps)),
                      pl.BlockSpec((tm, w_dt.shape[1]), lambda i: (i, 0))]
        out_shape += [jax.ShapeDtypeStruct((t // seq * dv, seq), BF16),
                      jax.ShapeDtypeStruct((t, w_dt.shape[1]), F32)]
        args += [w_vt, w_dt]
    outs = pl.pallas_call(
        body,
        grid=(t // tm,),
        in_specs=in_specs,
        out_specs=out_specs,
        out_shape=out_shape,
        compiler_params=_cparams(("parallel",)),
        name=name,
    )(*args)
    return tuple(outs) if even else outs[0]


def _rpb_onehot():
    qc = np.arange(GRID_W)[:, None]
    kc = np.arange(GRID_W)[None, :]
    c0 = np.clip(qc - NA_COLS // 2, 0, GRID_W - NA_COLS)
    inside = (kc >= c0) & (kc < c0 + NA_COLS)
    dc = kc - qc + NA_COLS - 1
    oh = np.zeros((LANES, GRID_W, GRID_W), np.float32)
    for d in range(2 * NA_COLS - 1):
        oh[d] = ((dc == d) & inside).astype(np.float32)
    mask = np.where(inside, 0.0, NEG).astype(np.float32)
    return oh.reshape(LANES, GRID_W * GRID_W), mask.reshape(1, GRID_W * GRID_W)


def _rpb_body(r_ref, oh_ref, m_ref, o_ref):
    o_ref[...] = jnp.dot(r_ref[...], oh_ref[...], precision=HIGHEST,
                         preferred_element_type=F32) + m_ref[...]


def _rpb_table(rpb):
    h = rpb.shape[0]
    nd = 2 * NA_ROWS - 1
    oh, mask = _rpb_onehot()
    r2 = jnp.pad(rpb.reshape(h * nd, 2 * NA_COLS - 1), ((0, 0), (0, LANES - 2 * NA_COLS + 1)))
    e = pl.pallas_call(
        _rpb_body,
        out_shape=jax.ShapeDtypeStruct((h * nd, GRID_W * GRID_W), F32),
        name="rpb_table",
    )(r2, jnp.asarray(oh), jnp.asarray(mask))
    e = jnp.swapaxes(e.reshape(h, nd, GRID_W, GRID_W), 2, 3)
    e = jnp.pad(e, ((0, 0), (1 - NA_DMIN, NA_DMIN + NA_ND - nd), (0, 0), (0, 0)),
                constant_values=NEG)
    first, second = e[:, 1:], e[:, :-1]
    neg = jnp.full_like(first, NEG)
    return jnp.concatenate([jnp.concatenate([first, second], axis=-1),
                            jnp.concatenate([first, neg], axis=-1),
                            jnp.concatenate([neg, second], axis=-1),
                            jnp.concatenate([neg, neg], axis=-1)], axis=1)


def _natten_window(r_first, rows):
    return jnp.clip(r_first - NA_ROWS // 2, 0, rows - NA_WIN)


def _natten_scores(q, r_first, k_ref, kc_ref, e_ref, s_scr, rows):
    nk = NA_WIN * GRID_W
    ws = _natten_window(r_first, rows)
    kwin = k_ref[pl.ds(pl.multiple_of(ws * GRID_W, LANES), nk), :]
    kc = kc_ref[...]
    idx = []
    for g in range(NA_QR // 2):
        r = r_first + 2 * g
        r0a = jnp.clip(r - NA_ROWS // 2, 0, rows - NA_ROWS)
        r0b = jnp.clip(r + 1 - NA_ROWS // 2, 0, rows - NA_ROWS)
        col = []
        for t in range(NA_WIN):
            kr = ws + t
            d = kr - r + (NA_ROWS - 1)
            out_first = jnp.logical_or(kr < r0a, kr >= r0a + NA_ROWS).astype(jnp.int32)
            out_second = jnp.logical_or(kr < r0b, kr >= r0b + NA_ROWS).astype(jnp.int32)
            col.append((2 * out_first + out_second) * NA_ND + d - NA_DMIN)
        idx.append(col)
    lo = lax.broadcasted_iota(jnp.int32, (1, LANES), 1) < NA_HEAD_DIM
    scale = jnp.asarray(NA_HEAD_DIM ** -0.5, BF16)
    nt = (((1,), (1,)), ((), ()))
    for a in range(2):
        sel = lo if a == 0 else jnp.logical_not(lo)
        qa = jnp.where(sel, q, jnp.zeros_like(q)) * scale
        bias = jnp.concatenate(
            [jnp.concatenate([e_ref[a, idx[g][t]] for g in range(NA_QR // 2)], axis=1) for t in range(NA_WIN)],
            axis=0)
        s_scr[a, :nk, :] = lax.dot_general(kwin, qa, nt, preferred_element_type=F32) + bias
        s_scr[a, nk:, :] = lax.dot_general(kc, qa, nt, preferred_element_type=F32)


def _natten_softmax_pv(r_first, s_scr, p_scr, vt_ref, vct_ref, rows):
    nk = NA_WIN * GRID_W
    nkb = s_scr.shape[1] // GRID_W
    ws = _natten_window(r_first, rows)
    vt_win = vt_ref[:, pl.ds(pl.multiple_of(ws * GRID_W, LANES), nk)]
    vct = vct_ref[...]
    outs = []
    for a in range(2):
        linv = []
        for g in range(NA_QR // 2):
            cols = slice(g * LANES, (g + 1) * LANES)
            mx = None
            for t in range(nkb):
                blk = s_scr[a, t * GRID_W:(t + 1) * GRID_W, cols]
                mx = blk if mx is None else jnp.maximum(mx, blk)
            m = jnp.max(mx, axis=0, keepdims=True)
            ls = None
            for t in range(nkb):
                p = jnp.exp(s_scr[a, t * GRID_W:(t + 1) * GRID_W, cols] - m)
                p_scr[a, t * GRID_W:(t + 1) * GRID_W, cols] = p.astype(BF16)
                ls = p if ls is None else ls + p
            linv.append(1.0 / jnp.sum(ls, axis=0, keepdims=True))
        hd = slice(a * NA_HEAD_DIM, (a + 1) * NA_HEAD_DIM)
        o_t = (jnp.dot(vt_win[hd, :], p_scr[a, :nk, :], preferred_element_type=F32)
               + jnp.dot(vct[hd, :], p_scr[a, nk:, :], preferred_element_type=F32))
        outs.append(o_t * jnp.concatenate(linv, axis=-1))
    return jnp.concatenate(outs, axis=0).T


def _natten_body(q0_ref, q_ref, qn_ref, k_ref, vt_ref, kc_ref, vct_ref, e_ref, o_ref,
                 sa_scr, sb_scr, pa_scr, pb_scr, *, rows, nrb):
    nq = NA_QR * GRID_W
    i = pl.program_id(2)
    b0 = NA_BLOCKS * i
    bufs = ((sa_scr, pa_scr), (sb_scr, pb_scr))

    @pl.when(i == 0)
    def _():
        _natten_scores(q0_ref[...], 0, k_ref, kc_ref, e_ref, sa_scr, rows)

    for k in range(NA_BLOCKS):
        s_cur, p_cur = bufs[k % 2]
        s_nxt = bufs[(k + 1) % 2][0]
        if k + 1 < NA_BLOCKS:
            q_nxt, b_nxt = q_ref[(k + 1) * nq:(k + 2) * nq, :], b0 + k + 1
        else:
            q_nxt, b_nxt = qn_ref[...], jnp.minimum(b0 + NA_BLOCKS, nrb - 1)
        _natten_scores(q_nxt, b_nxt * NA_QR, k_ref, kc_ref, e_ref, s_nxt, rows)
        o_ref[k * nq:(k + 1) * nq, :] = _natten_softmax_pv(
            (b0 + k) * NA_QR, s_cur, p_cur, vt_ref, vct_ref, rows).astype(o_ref.dtype)


def _natten_call(proj_l, vt_l, proj_c, vt_c, e2, batch, seq, lc):
    d = e2.shape[0] * NA_HEAD_DIM
    npair = d // LANES
    rows = seq // GRID_W
    nrb = rows // NA_QR
    nq = NA_QR * GRID_W
    nkeys = NA_WIN * GRID_W + lc
    body = functools.partial(_natten_body, rows=rows, nrb=nrb)
    nst = nrb // NA_BLOCKS
    return pl.pallas_call(
        body,
        grid=(batch, npair, nst),
        in_specs=[pl.BlockSpec((nq, LANES), lambda b, p, i: (b * nrb, p)),
                  pl.BlockSpec((NA_BLOCKS * nq, LANES), lambda b, p, i: (b * nst + i, p)),
                  pl.BlockSpec((nq, LANES),
                               lambda b, p, i: (b * nrb + jnp.minimum(NA_BLOCKS * (i + 1), nrb - 1), p)),
                  pl.BlockSpec((seq, LANES), lambda b, p, i: (b, npair + p)),
                  pl.BlockSpec((LANES, seq), lambda b, p, i: (b * npair + p, 0)),
                  pl.BlockSpec((lc, LANES), lambda b, p, i: (b, npair + p)),
                  pl.BlockSpec((LANES, lc), lambda b, p, i: (b * npair + p, 0)),
                  pl.BlockSpec((2, e2.shape[1], GRID_W, LANES), lambda b, p, i: (p, 0, 0, 0))],
        out_specs=pl.BlockSpec((NA_BLOCKS * nq, LANES), lambda b, p, i: (b * nst + i, p)),
        out_shape=jax.ShapeDtypeStruct((batch * seq, d), BF16),
        scratch_shapes=[pltpu.VMEM((2, nkeys, nq), F32), pltpu.VMEM((2, nkeys, nq), F32),
                        pltpu.VMEM((2, nkeys, nq), BF16), pltpu.VMEM((2, nkeys, nq), BF16)],
        compiler_params=_cparams(("parallel", "parallel", "arbitrary")),
        name="natten",
    )(proj_l, proj_l, proj_l, proj_l, vt_l, proj_c, vt_c, e2)


def _ctxattn_body(q_ref, k_ref, vt_ref, o_ref):
    q = q_ref[...]
    k = k_ref[...]
    vt = vt_ref[...]
    lane = lax.broadcasted_iota(jnp.int32, (1, LANES), 1)
    lo = lane < NA_HEAD_DIM
    scale = NA_HEAD_DIM ** -0.5
    nt = (((1,), (1,)), ((), ()))
    outs = []
    for a in range(2):
        sel = lo if a == 0 else jnp.logical_not(lo)
        qa = jnp.where(sel, q, jnp.zeros_like(q)) * jnp.asarray(scale, BF16)
        s = lax.dot_general(qa, k, nt, preferred_element_type=F32)
        m = jnp.max(s, axis=-1, keepdims=True)
        p = jnp.exp(s - m)
        l = jnp.sum(p, axis=-1, keepdims=True)
        outs.append(lax.dot_general(p.astype(BF16), vt, nt, preferred_element_type=F32) / l)
    o_ref[...] = jnp.where(lo, outs[0], outs[1]).astype(o_ref.dtype)


def _ctxattn_call(proj_c, vt_c, batch, lc, d):
    npair = d // LANES
    return pl.pallas_call(
        _ctxattn_body,
        grid=(batch, npair),
        in_specs=[pl.BlockSpec((lc, LANES), lambda b, p: (b, p)),
                  pl.BlockSpec((lc, LANES), lambda b, p: (b, npair + p)),
                  pl.BlockSpec((LANES, lc), lambda b, p: (b * npair + p, 0))],
        out_specs=pl.BlockSpec((lc, LANES), lambda b, p: (b, p)),
        out_shape=jax.ShapeDtypeStruct((batch * lc, d), BF16),
        compiler_params=_cparams(("parallel", "parallel")),
        name="ctx_attn",
    )(proj_c, proj_c, vt_c)


def _dwconv_silu_body(xp_ref, x_ref, xn_ref, w_ref, b_ref, o_ref, *, tiles_per_seq):
    i = pl.program_id(0)
    k = w_ref.shape[0]
    half = k // 2
    tb = x_ref.shape[0]
    hr = xp_ref.shape[0]
    first = (i % tiles_per_seq) == 0
    last = (i % tiles_per_seq) == tiles_per_seq - 1
    prev = jnp.where(first, 0.0, xp_ref[...].astype(F32))
    nxt = jnp.where(last, 0.0, xn_ref[...].astype(F32))
    ext = jnp.concatenate([prev, x_ref[...].astype(F32), nxt], axis=0)
    acc = jnp.zeros((tb, x_ref.shape[1]), F32) + b_ref[...]
    for t in range(k):
        off = hr - half + t
        acc = acc + ext[off:off + tb, :] * w_ref[pl.ds(t, 1), :]
    o_ref[...] = _silu(acc).astype(o_ref.dtype)


def _dwconv_silu_call(proj, col0, width, w, b, seq, *, tb, tc):
    t = proj.shape[0]
    hr = BF16_SUBLANES
    nrt = t // tb
    cb0 = col0 // tc
    body = functools.partial(_dwconv_silu_body, tiles_per_seq=seq // tb)
    return pl.pallas_call(
        body,
        grid=(nrt, width // tc),
        in_specs=[pl.BlockSpec((hr, tc), lambda i, j: (jnp.maximum(i * (tb // hr) - 1, 0), cb0 + j)),
                  pl.BlockSpec((tb, tc), lambda i, j: (i, cb0 + j)),
                  pl.BlockSpec((hr, tc), lambda i, j: (jnp.minimum((i + 1) * (tb // hr), t // hr - 1), cb0 + j)),
                  pl.BlockSpec((w.shape[0], tc), lambda i, j: (0, j)),
                  pl.BlockSpec((1, tc), lambda i, j: (0, j))],
        out_specs=pl.BlockSpec((tb, tc), lambda i, j: (i, j)),
        out_shape=jax.ShapeDtypeStruct((t, width), BF16),
        compiler_params=_cparams(("parallel", "parallel")),
        name="dwconv_silu",
    )(proj, proj, proj, w, b.reshape(1, width))


def _ssd_prep_body(dt_ref, dtb_ref, alog_ref, dtt_ref, cum_ref, cumt_ref, *, chunks):
    q = SSM_CHUNK
    fwd = pl.program_id(1) == 0
    ii = lax.broadcasted_iota(jnp.int32, (q, q), 0)
    jj = lax.broadcasted_iota(jnp.int32, (q, q), 1)
    tri = ((jj - ii) * jnp.where(fwd, 1, -1) <= 0).astype(F32)
    a = -jnp.exp(alog_ref[0])
    for c in range(chunks):
        rws = slice(c * q, (c + 1) * q)
        dtv = _softplus(dt_ref[rws, :] + dtb_ref[0])
        cum = jnp.dot(tri, dtv * a, precision=HIGHEST, preferred_element_type=F32)
        dtt_ref[0, rws, :] = dtv.T
        cum_ref[0, rws, :] = cum
        cumt_ref[0, rws, :] = cum.T


def _ssd_prep_call(dt, dtb, alog, batch, seq):
    nc = seq // SSM_CHUNK
    chunks = min(SSD_PREP_CHUNKS, nc)
    rows = chunks * SSM_CHUNK
    t = batch * seq
    out = jax.ShapeDtypeStruct((2, t, LANES), F32)
    ospec = pl.BlockSpec((1, rows, LANES), lambda i, d: (d, i, 0))
    return pl.pallas_call(
        functools.partial(_ssd_prep_body, chunks=chunks),
        grid=(t // rows, 2),
        in_specs=[pl.BlockSpec((rows, LANES), lambda i, d: (i, d)),
                  pl.BlockSpec((1, 1, LANES), lambda i, d: (d, 0, 0)),
                  pl.BlockSpec((1, 1, LANES), lambda i, d: (d, 0, 0))],
        out_specs=[ospec, ospec, ospec],
        out_shape=[out, out, out],
        compiler_params=_cparams(("parallel", "parallel")),
        name="ssd_prep",
    )(dt, dtb, alog)


def _ssd_body(x_ref, b_ref, c_ref, dtt_ref, cum_ref, cumt_ref, h0_ref, y_ref, hT_ref, st_scr, *, nsteps):
    q = SSM_CHUNK
    dr = pl.program_id(1)
    s = pl.program_id(2)

    @pl.when(s == 0)
    def _():
        st_scr[...] = h0_ref[0, 0]

    fwd = dr == 0
    cum = cum_ref[0]
    cum_t = cumt_ref[0]
    dt_t = dtt_ref[0]
    ii = lax.broadcasted_iota(jnp.int32, (q, q), 0)
    jj = lax.broadcasted_iota(jnp.int32, (q, q), 1)
    tri = (jj - ii) * jnp.where(fwd, 1, -1) <= 0
    tot = jnp.where(fwd, cum[q - 1:q, :], cum[0:1, :])
    tot_c = jnp.where(fwd, cum_t[:, q - 1:q], cum_t[:, 0:1])
    upd_t = dt_t * jnp.exp(tot_c - cum_t)
    lane = lax.broadcasted_iota(jnp.int32, (1, LANES), 1)
    lo = lane < SSM_HEAD_DIM
    heads_per_group = SSM_HEADS // SSM_GROUPS
    for g in range(SSM_GROUPS):
        bg = b_ref[:, g * SSM_STATE:(g + 1) * SSM_STATE]
        cg = c_ref[:, g * SSM_STATE:(g + 1) * SSM_STATE]
        cb = lax.dot_general(cg, bg, (((1,), (1,)), ((), ())), preferred_element_type=F32)
        bg_t = bg.astype(F32).T
        for hp in range(g * heads_per_group // 2, (g + 1) * heads_per_group // 2):
            ha, hb = 2 * hp, 2 * hp + 1
            xb = x_ref[:, hp * LANES:(hp + 1) * LANES]
            cca = jnp.broadcast_to(cum[:, ha:ha + 1], (q, q))
            ccb = jnp.broadcast_to(cum[:, hb:hb + 1], (q, q))
            l_a = jnp.exp(jnp.where(tri, cca - cum_t[ha:ha + 1, :], NEG))
            l_b = jnp.exp(jnp.where(tri, ccb - cum_t[hb:hb + 1, :], NEG))
            m_a = (cb * l_a * dt_t[ha:ha + 1, :]).astype(BF16)
            m_b = (cb * l_b * dt_t[hb:hb + 1, :]).astype(BF16)
            y_intra = jnp.where(lo, jnp.dot(m_a, xb, preferred_element_type=F32),
                                jnp.dot(m_b, xb, preferred_element_type=F32))
            ccp = jnp.where(lo, cca, ccb)
            st = st_scr[hp]
            y_inter = jnp.dot(cg, st.astype(BF16), preferred_element_type=F32) * jnp.exp(ccp)
            y_ref[0, :, hp * LANES:(hp + 1) * LANES] = y_intra + y_inter
            totp = jnp.where(lo, jnp.broadcast_to(tot[:, ha:ha + 1], (1, LANES)),
                             jnp.broadcast_to(tot[:, hb:hb + 1], (1, LANES)))
            bw_a = (bg_t * upd_t[ha:ha + 1, :]).astype(BF16)
            bw_b = (bg_t * upd_t[hb:hb + 1, :]).astype(BF16)
            st_scr[hp] = st * jnp.exp(totp) + jnp.where(lo, jnp.dot(bw_a, xb, preferred_element_type=F32),
                                                        jnp.dot(bw_b, xb, preferred_element_type=F32))

    @pl.when(s == nsteps - 1)
    def _():
        hT_ref[0, 0] = st_scr[...]


def _ssd_call(xbc, dt, dtb, alog, h0, batch, seq):
    inner = SSM_HEADS * SSM_HEAD_DIM
    gn = SSM_GROUPS * SSM_STATE
    nc = seq // SSM_CHUNK
    npair = SSM_HEADS // 2

    def rblk(b, d, s):
        return b * nc + jnp.where(d == 0, s, nc - 1 - s)

    dt_t, cum, cum_t = _ssd_prep_call(dt, dtb, alog, batch, seq)
    pspec = pl.BlockSpec((1, SSM_CHUNK, LANES), lambda b, d, s: (d, rblk(b, d, s), 0))
    body = functools.partial(_ssd_body, nsteps=nc)
    return pl.pallas_call(
        body,
        grid=(batch, 2, nc),
        in_specs=[pl.BlockSpec((SSM_CHUNK, inner), lambda b, d, s: (rblk(b, d, s), 0)),
                  pl.BlockSpec((SSM_CHUNK, gn), lambda b, d, s: (rblk(b, d, s), inner // gn)),
                  pl.BlockSpec((SSM_CHUNK, gn), lambda b, d, s: (rblk(b, d, s), inner // gn + 1)),
                  pspec, pspec, pspec,
                  pl.BlockSpec((1, 1, npair, SSM_STATE, LANES), lambda b, d, s: (b, d, 0, 0, 0))],
        out_specs=[pl.BlockSpec((1, SSM_CHUNK, inner), lambda b, d, s: (d, rblk(b, d, s), 0)),
                   pl.BlockSpec((1, 1, npair, SSM_STATE, LANES), lambda b, d, s: (b, d, 0, 0, 0))],
        out_shape=[jax.ShapeDtypeStruct((2, batch * seq, inner), F32),
                   jax.ShapeDtypeStruct((batch, 2, npair, SSM_STATE, LANES), F32)],
        scratch_shapes=[pltpu.VMEM((npair, SSM_STATE, LANES), F32)],
        compiler_params=_cparams(("parallel", "parallel", "arbitrary")),
        name="ssd",
    )(xbc, xbc, xbc, dt_t, cum, cum_t, h0)


def _evenout_body(attn_ref, y_ref, xs_ref, z_ref, h_ref, gate_ref, dsk_ref, nw_ref, w_ref, lg_ref, lb_ref,
                  o_ref, ssm_scr, *, row_base, tiles_per_row, alpha):
    i = pl.program_id(0)
    rid = row_base + i // tiles_per_row
    inner = xs_ref.shape[1]
    gw = inner // SSM_GROUPS
    z = z_ref[...].astype(F32)
    y = y_ref[0] + y_ref[1] + xs_ref[...].astype(F32) * dsk_ref[...]
    u = y * _silu(z)
    for g in range(SSM_GROUPS):
        ug = u[:, g * gw:(g + 1) * gw]
        ms = jnp.mean(ug * ug, axis=-1, keepdims=True)
        ssm_scr[:, g * gw:(g + 1) * gw] = (ug * lax.rsqrt(ms + RMS_EPS) * nw_ref[:, g * gw:(g + 1) * gw]).astype(BF16)
    d_attn = attn_ref.shape[1]
    acc = jnp.dot(attn_ref[...], w_ref[:d_attn, :], preferred_element_type=F32)
    acc = acc + jnp.dot(ssm_scr[...], w_ref[d_attn:, :], preferred_element_type=F32)
    o_ref[...] = _ln_residual(h_ref[...], acc, _mod_row(gate_ref, rid), lg_ref[0], lb_ref[0], alpha)


def _evenout_call(attn, y2, xbc, proj, z_cb, h, mods, layer, dskip_row, norm_w, w_out, ln_g, ln_b, ln_idx,
                  *, tm, row_base, tiles_per_row, alpha):
    t, d = h.shape
    inner = y2.shape[2]
    body = functools.partial(_evenout_body, row_base=row_base, tiles_per_row=tiles_per_row, alpha=alpha)
    return pl.pallas_call(
        body,
        grid=(t // tm,),
        in_specs=[pl.BlockSpec((tm, d), lambda i: (i, 0)),
                  pl.BlockSpec((2, tm, inner), lambda i: (0, i, 0)),
                  pl.BlockSpec((tm, inner), lambda i: (i, 0)),
                  pl.BlockSpec((tm, inner), lambda i: (i, z_cb)),
                  pl.BlockSpec((tm, d), lambda i: (i, 0)),
                  pl.BlockSpec((1, 8, d), lambda i: (layer, 0, 2)),
                  pl.BlockSpec((1, inner), lambda i: (0, 0)),
                  pl.BlockSpec((1, inner), lambda i: (0, 0)),
                  pl.BlockSpec(w_out.shape, lambda i: (0, 0)),
                  pl.BlockSpec((1, 1, d), lambda i: (ln_idx, 0, 0)),
                  pl.BlockSpec((1, 1, d), lambda i: (ln_idx, 0, 0))],
        out_specs=pl.BlockSpec((tm, d), lambda i: (i, 0)),
        out_shape=jax.ShapeDtypeStruct((t, d), F32),
        scratch_shapes=[pltpu.VMEM((tm, inner), BF16)],
        compiler_params=_cparams(("parallel",)),
        name="even_out",
    )(attn, y2, xbc, proj, h, mods, dskip_row, norm_w, w_out, ln_g, ln_b)


def _hidden_chunks(f):
    half = (f // 2 + MXU_DIM - 1) // MXU_DIM * MXU_DIM
    return ((0, half), (half, f - half))


def _swiglu(u, w1_ref, w3_ref, w2_ref):
    acc = None
    for f0, fc in _hidden_chunks(w1_ref.shape[1]):
        a = jnp.dot(u, w1_ref[:, f0:f0 + fc], preferred_element_type=F32)
        b = jnp.dot(u, w3_ref[:, f0:f0 + fc], preferred_element_type=F32)
        t = (_silu(a) * b).astype(BF16)
        y = jnp.dot(t, w2_ref[f0:f0 + fc, :], preferred_element_type=F32)
        acc = y if acc is None else acc + y
    return acc


def _ffn_body(h_ref, sh_ref, sc_ref, gate_ref, w1_ref, w3_ref, w2_ref, lg_ref, lb_ref, o_ref,
              *, row_base, tiles_per_row, alpha):
    i = pl.program_id(0)
    rid = row_base + i // tiles_per_row
    h = h_ref[...]
    u = (h * (1.0 + _mod_row(sc_ref, rid)) + _mod_row(sh_ref, rid)).astype(BF16)
    y = _swiglu(u, w1_ref, w3_ref, w2_ref)
    o_ref[...] = _ln_residual(h, y, _mod_row(gate_ref, rid), lg_ref[0], lb_ref[0], alpha)


def _ffn_call(h, mods, layer, w1c, w3c, w2c, ln_g, ln_b, ln_idx, *, tm, row_base, tiles_per_row, alpha):
    t, d = h.shape
    body = functools.partial(_ffn_body, row_base=row_base, tiles_per_row=tiles_per_row, alpha=alpha)
    wspec = lambda w: pl.BlockSpec(w.shape, lambda i: (0, 0), pipeline_mode=pl.Buffered(1))
    return pl.pallas_call(
        body,
        grid=(t // tm,),
        in_specs=[pl.BlockSpec((tm, d), lambda i: (i, 0)),
                  pl.BlockSpec((1, 8, d), lambda i: (layer, 0, 3)),
                  pl.BlockSpec((1, 8, d), lambda i: (layer, 0, 4)),
                  pl.BlockSpec((1, 8, d), lambda i: (layer, 0, 5)),
                  wspec(w1c), wspec(w3c), wspec(w2c),
                  pl.BlockSpec((1, 1, d), lambda i: (ln_idx, 0, 0)),
                  pl.BlockSpec((1, 1, d), lambda i: (ln_idx, 0, 0))],
        out_specs=pl.BlockSpec((tm, d), lambda i: (i, 0)),
        out_shape=jax.ShapeDtypeStruct((t, d), F32),
        compiler_params=_cparams(("parallel",)),
        name="ffn",
    )(h, mods, mods, mods, w1c, w3c, w2c, ln_g, ln_b)


def _scout_body(pb_ref, pcp_ref, pc_ref, pcn_ref, php_ref, ph_ref, phn_ref, h_ref, gate_ref, cw_ref, w_ref,
                lg_ref, lb_ref, o_ref, *, row_base, tiles_per_row, tiles_per_seq, alpha):
    i = pl.program_id(0)
    rid = row_base + i // tiles_per_row
    tm = ph_ref.shape[0]
    hr = php_ref.shape[0]
    first = (i % tiles_per_seq) == 0
    last = (i % tiles_per_seq) == tiles_per_seq - 1
    prev = jnp.where(first, 0.0, pcp_ref[...].astype(F32) * php_ref[...].astype(F32))
    cur = pc_ref[...].astype(F32) * ph_ref[...].astype(F32)
    nxt = jnp.where(last, 0.0, pcn_ref[...].astype(F32) * phn_ref[...].astype(F32))
    ext = jnp.concatenate([prev, cur, nxt], axis=0)
    acc = jnp.zeros_like(cur)
    for t in range(SC_CONV):
        off = hr - SC_CONV // 2 + t
        acc = acc + ext[off:off + tm, :] * cw_ref[pl.ds(t, 1), :]
    a = (pb_ref[...].astype(F32) * acc).astype(BF16)
    y = jnp.dot(a, w_ref[...], preferred_element_type=F32)
    o_ref[...] = _ln_residual(h_ref[...], y, _mod_row(gate_ref, rid), lg_ref[0], lb_ref[0], alpha)


def _scout_call(proj, h, mods, layer, conv_w, w_out, ln_g, ln_b, ln_idx, seq,
                *, tm, row_base, tiles_per_row, alpha):
    t, d = h.shape
    hr = BF16_SUBLANES
    r = tm // hr
    nh = t // hr
    body = functools.partial(_scout_body, row_base=row_base, tiles_per_row=tiles_per_row,
                             tiles_per_seq=seq // tm, alpha=alpha)
    prev_map = lambda cb: (lambda i: (jnp.maximum(i * r - 1, 0), cb))
    next_map = lambda cb: (lambda i: (jnp.minimum((i + 1) * r, nh - 1), cb))
    return pl.pallas_call(
        body,
        grid=(t // tm,),
        in_specs=[pl.BlockSpec((tm, d), lambda i: (i, 0)),
                  pl.BlockSpec((hr, d), prev_map(1)),
                  pl.BlockSpec((tm, d), lambda i: (i, 1)),
                  pl.BlockSpec((hr, d), next_map(1)),
                  pl.BlockSpec((hr, d), prev_map(2)),
                  pl.BlockSpec((tm, d), lambda i: (i, 2)),
                  pl.BlockSpec((hr, d), next_map(2)),
                  pl.BlockSpec((tm, d), lambda i: (i, 0)),
                  pl.BlockSpec((1, 8, d), lambda i: (layer, 0, 2)),
                  pl.BlockSpec(conv_w.shape, lambda i: (0, 0)),
                  pl.BlockSpec(w_out.shape, lambda i: (0, 0)),
                  pl.BlockSpec((1, 1, d), lambda i: (ln_idx, 0, 0)),
                  pl.BlockSpec((1, 1, d), lambda i: (ln_idx, 0, 0))],
        out_specs=pl.BlockSpec((tm, d), lambda i: (i, 0)),
        out_shape=jax.ShapeDtypeStruct((t, d), F32),
        compiler_params=_cparams(("parallel",)),
        name="shortconv_out",
    )(proj, proj, proj, proj, proj, proj, proj, h, mods, conv_w, w_out, ln_g, ln_b)


def _router_body(h_ref, sh_ref, sc_ref, wr_ref, u_ref, meta_ref, cnt_ref, cnt_scr, *, row_base, tiles_per_row):
    i = pl.program_id(0)
    rid = row_base + i // tiles_per_row
    tm = h_ref.shape[0]

    @pl.when(i == 0)
    def _():
        cnt_scr[...] = jnp.zeros_like(cnt_scr)

    u = h_ref[...] * (1.0 + _mod_row(sc_ref, rid)) + _mod_row(sh_ref, rid)
    u_ref[...] = u
    logits = jnp.dot(u, wr_ref[...], precision=HIGHEST, preferred_element_type=F32)
    lane = lax.broadcasted_iota(jnp.int32, (tm, LANES), 1).astype(F32)
    lg = jnp.where(lane < N_EXPERTS, logits, NEG)
    v1 = jnp.max(lg, axis=-1, keepdims=True)
    e1 = jnp.min(jnp.where(lg == v1, lane, float(LANES)), axis=-1, keepdims=True)
    lg2 = jnp.where(lane == e1, 2 * NEG, lg)
    v2 = jnp.max(lg2, axis=-1, keepdims=True)
    e2 = jnp.min(jnp.where(lg2 == v2, lane, float(LANES)), axis=-1, keepdims=True)
    g2 = 1.0 / (1.0 + jnp.exp(v1 - v2))
    g1 = 1.0 - g2
    oh1 = (lane == e1).astype(F32)
    oh2 = (lane == e2).astype(F32)
    both = (oh1 + oh2).astype(BF16)
    ii = lax.broadcasted_iota(jnp.int32, (tm, tm), 0)
    jj = lax.broadcasted_iota(jnp.int32, (tm, tm), 1)
    strict = (jj < ii).astype(BF16)
    before = jnp.dot(strict, both, preferred_element_type=F32) + cnt_scr[...]
    p1 = jnp.sum(before * oh1, axis=-1, keepdims=True)
    p2 = jnp.sum(before * oh2, axis=-1, keepdims=True)
    cnt_scr[...] = cnt_scr[...] + jnp.sum(oh1 + oh2, axis=0, keepdims=True)
    meta = jnp.where(lane == 0, e1,
           jnp.where(lane == 1, e2,
           jnp.where(lane == 2, p1,
           jnp.where(lane == 3, p2,
           jnp.where(lane == 4, g1,
           jnp.where(lane == 5, g2, 0.0))))))
    meta_ref[...] = meta
    cnt_ref[...] = jnp.broadcast_to(cnt_scr[...], cnt_ref.shape)


def _router_call(h, mods, layer, wr_pad, *, tm, row_base, tiles_per_row):
    t, d = h.shape
    body = functools.partial(_router_body, row_base=row_base, tiles_per_row=tiles_per_row)
    return pl.pallas_call(
        body,
        grid=(t // tm,),
        in_specs=[pl.BlockSpec((tm, d), lambda i: (i, 0)),
                  pl.BlockSpec((1, 8, d), lambda i: (layer, 0, 3)),
                  pl.BlockSpec((1, 8, d), lambda i: (layer, 0, 4)),
                  pl.BlockSpec(wr_pad.shape, lambda i: (0, 0))],
        out_specs=[pl.BlockSpec((tm, d), lambda i: (i, 0)),
                   pl.BlockSpec((tm, LANES), lambda i: (i, 0)),
                   pl.BlockSpec((8, LANES), lambda i: (0, 0))],
        out_shape=[jax.ShapeDtypeStruct((t, d), F32),
                   jax.ShapeDtypeStruct((t, LANES), F32),
                   jax.ShapeDtypeStruct((8, LANES), F32)],
        scratch_shapes=[pltpu.VMEM((1, LANES), F32)],
        compiler_params=_cparams(("arbitrary",)),
        name="moe_router",
    )(h, mods, mods, wr_pad)


def _dispatch_body(tv_ref, dest_ref, u_ref, rows_hbm, zbuf, sem, zsem):
    i = pl.program_id(0)
    tm = u_ref.shape[0]
    tile = zbuf.shape[0]

    @pl.when(i == 0)
    def _():
        zbuf[...] = jnp.zeros_like(zbuf)

        def fill_copy(t):
            return pltpu.make_async_copy(zbuf, rows_hbm.at[pl.ds(t * tile, tile)], zsem.at[0])

        def fill(t, c):
            @pl.when(tv_ref[t] < tile)
            def _():
                fill_copy(t).start()
            return c

        def fill_wait(t, c):
            @pl.when(tv_ref[t] < tile)
            def _():
                fill_copy(t).wait()
            return c

        lax.fori_loop(0, tv_ref.shape[0], fill, 0)
        lax.fori_loop(0, tv_ref.shape[0], fill_wait, 0)

    def row_copy(r, k):
        return pltpu.make_async_copy(u_ref.at[pl.ds(r, 1)], rows_hbm.at[pl.ds(dest_ref[0, k, r], 1)], sem.at[k])

    def issue(r, c):
        for k in range(TOP_K):
            row_copy(r, k).start()
        return c

    def drain(r, c):
        for k in range(TOP_K):
            row_copy(r, k).wait()
        return c

    lax.fori_loop(0, tm, issue, 0, unroll=8)
    lax.fori_loop(0, tm, drain, 0, unroll=8)


def _dispatch_call(u, dest_t, n_rows, tile_valid, *, tm):
    t, d = u.shape
    grid_spec = pltpu.PrefetchScalarGridSpec(
        num_scalar_prefetch=1,
        grid=(t // tm,),
        in_specs=[pl.BlockSpec((1, TOP_K, tm), lambda i, tv: (i, 0, 0), memory_space=pltpu.SMEM),
                  pl.BlockSpec((tm, d), lambda i, tv: (i, 0))],
        out_specs=pl.BlockSpec(memory_space=pl.ANY),
        scratch_shapes=[pltpu.VMEM((MOE_TILE, d), F32),
                        pltpu.SemaphoreType.DMA((TOP_K,)),
                        pltpu.SemaphoreType.DMA((1,))],
    )
    return pl.pallas_call(
        _dispatch_body,
        grid_spec=grid_spec,
        out_shape=jax.ShapeDtypeStruct((n_rows, d), F32),
        compiler_params=_cparams(("arbitrary",)),
        name="moe_dispatch",
    )(tile_valid, dest_t, u)


def _moe_body(te_ref, tv_ref, x_ref, w1_hbm, w3_hbm, w2_hbm, o_ref, w1s, w3s, w2s, wsem):
    t = pl.program_id(0)
    e = te_ref[t]
    e_prev = te_ref[jnp.maximum(t - 1, 0)]

    @pl.when(jnp.logical_or(t == 0, e != e_prev))
    def _():
        copies = [pltpu.make_async_copy(w1_hbm.at[e], w1s, wsem.at[0]),
                  pltpu.make_async_copy(w3_hbm.at[e], w3s, wsem.at[1]),
                  pltpu.make_async_copy(w2_hbm.at[e], w2s, wsem.at[2])]
        for cp in copies:
            cp.start()
        for cp in copies:
            cp.wait()

    @pl.when(tv_ref[t] > 0)
    def _():
        o_ref[...] = _swiglu(x_ref[...].astype(BF16), w1s, w3s, w2s)

    @pl.when(tv_ref[t] == 0)
    def _():
        o_ref[...] = jnp.zeros_like(o_ref)


def _moe_call(rows_in, tile_expert, tile_valid, w1c, w3c, w2c):
    n_rows, d = rows_in.shape
    tr = MOE_TILE
    grid_spec = pltpu.PrefetchScalarGridSpec(
        num_scalar_prefetch=2,
        grid=(n_rows // tr,),
        in_specs=[pl.BlockSpec((tr, d), lambda i, te, tv: (i, 0)),
                  pl.BlockSpec(memory_space=pl.ANY),
                  pl.BlockSpec(memory_space=pl.ANY),
                  pl.BlockSpec(memory_space=pl.ANY)],
        out_specs=pl.BlockSpec((tr, d), lambda i, te, tv: (i, 0)),
        scratch_shapes=[pltpu.VMEM(w1c.shape[1:], BF16),
                        pltpu.VMEM(w3c.shape[1:], BF16),
                        pltpu.VMEM(w2c.shape[1:], BF16),
                        pltpu.SemaphoreType.DMA((3,))],
    )
    return pl.pallas_call(
        _moe_body,
        grid_spec=grid_spec,
        out_shape=jax.ShapeDtypeStruct((n_rows, d), F32),
        compiler_params=_cparams(("arbitrary",)),
        name="moe_experts",
    )(tile_expert, tile_valid, rows_in, w1c, w3c, w2c)


def _combine_body(dest_ref, rows_hbm, meta_ref, h_ref, gate_ref, lg_ref, lb_ref, o_ref, rbuf, gsem,
                  *, row_base, tiles_per_row, alpha):
    i = pl.program_id(0)
    rid = row_base + i // tiles_per_row
    tm = h_ref.shape[0]

    def row_copy(r, k):
        return pltpu.make_async_copy(rows_hbm.at[pl.ds(dest_ref[0, k, r], 1)], rbuf.at[k, pl.ds(r, 1)], gsem.at[k])

    def issue(r, c):
        for k in range(TOP_K):
            row_copy(r, k).start()
        return c

    def drain(r, c):
        for k in range(TOP_K):
            row_copy(r, k).wait()
        return c

    lax.fori_loop(0, tm, issue, 0, unroll=8)
    lax.fori_loop(0, tm, drain, 0, unroll=8)
    meta = meta_ref[...]
    y = meta[:, 4:5] * rbuf[0] + meta[:, 5:6] * rbuf[1]
    o_ref[...] = _ln_residual(h_ref[...], y, _mod_row(gate_ref, rid), lg_ref[0], lb_ref[0], alpha)


def _combine_call(dest_t, rows_out, meta, h, mods, layer, ln_g, ln_b, ln_idx, *, tm, row_base, tiles_per_row, alpha):
    t, d = h.shape
    body = functools.partial(_combine_body, row_base=row_base, tiles_per_row=tiles_per_row, alpha=alpha)
    return pl.pallas_call(
        body,
        grid=(t // tm,),
        in_specs=[pl.BlockSpec((1, TOP_K, tm), lambda i: (i, 0, 0), memory_space=pltpu.SMEM),
                  pl.BlockSpec(memory_space=pl.ANY),
                  pl.BlockSpec((tm, LANES), lambda i: (i, 0)),
                  pl.BlockSpec((tm, d), lambda i: (i, 0)),
                  pl.BlockSpec((1, 8, d), lambda i: (layer, 0, 5)),
                  pl.BlockSpec((1, 1, d), lambda i: (ln_idx, 0, 0)),
                  pl.BlockSpec((1, 1, d), lambda i: (ln_idx, 0, 0))],
        out_specs=pl.BlockSpec((tm, d), lambda i: (i, 0)),
        out_shape=jax.ShapeDtypeStruct((t, d), F32),
        scratch_shapes=[pltpu.VMEM((TOP_K, tm, d), F32),
                        pltpu.SemaphoreType.DMA((TOP_K,))],
        compiler_params=_cparams(("arbitrary",)),
        name="moe_combine",
    )(dest_t, rows_out, meta, h, mods, ln_g, ln_b)


def _moe_layer(h, mods, layer, wr_pad, w1c, w3c, w2c, ln_g, ln_b, ln_idx, *, tm, row_base, tiles_per_row, alpha):
    t, d = h.shape
    u, meta, cnt = _router_call(h, mods, layer, wr_pad, tm=tm, row_base=row_base, tiles_per_row=tiles_per_row)
    n_assign = t * TOP_K
    n_tiles = n_assign // MOE_TILE + N_EXPERTS
    counts = cnt[0, :N_EXPERTS].astype(jnp.int32)
    padded = (counts + MOE_TILE - 1) // MOE_TILE * MOE_TILE
    pad_ends = jnp.cumsum(padded)
    pad_starts = pad_ends - padded
    top_e = meta[:, 0:TOP_K].astype(jnp.int32)
    rank = meta[:, 2:2 + TOP_K].astype(jnp.int32)
    dest = pad_starts[top_e] + rank
    dest_t = jnp.swapaxes(dest.reshape(t // tm, tm, TOP_K), 1, 2)
    tile_start = jnp.arange(n_tiles, dtype=jnp.int32) * MOE_TILE
    tile_expert = jnp.minimum(jnp.sum(tile_start[:, None] >= pad_ends[None, :], axis=1),
                              N_EXPERTS - 1).astype(jnp.int32)
    seg_end = jnp.where(tile_start < pad_ends[-1], (pad_starts + counts)[tile_expert], 0)
    tile_valid = jnp.clip(seg_end - tile_start, 0, MOE_TILE).astype(jnp.int32)
    rows_in = _dispatch_call(u, dest_t, n_tiles * MOE_TILE, tile_valid, tm=tm)
    rows_out = _moe_call(rows_in, tile_expert, tile_valid, w1c, w3c, w2c)
    return _combine_call(dest_t, rows_out, meta, h, mods, layer, ln_g, ln_b, ln_idx,
                         tm=tm, row_base=row_base, tiles_per_row=tiles_per_row, alpha=alpha)


def kernel(x, c, ctx, c_ctx, ada_w, ada_b, ln_g, ln_b, even_w_in, na_rpb, ssm_conv_w, ssm_conv_b, ssm_dt_bias,
           ssm_a_log, ssm_d, ssm_norm_w, even_w_out, ffn_w1, ffn_w3, ffn_w2, sc_w_in, sc_conv_w, sc_w_out,
           moe_router, moe_w1, moe_w3, moe_w2):
    batch, seq, d = x.shape
    lc = ctx.shape[1]
    depth = ada_w.shape[0]
    alpha = (2 * depth) ** 0.25
    inner = SSM_HEADS * SSM_HEAD_DIM
    gn = SSM_GROUPS * SSM_STATE
    n_main = 3 * d + inner + inner + 2 * gn
    assert batch + 1 <= 8

    cvec = jnp.zeros((8, d), F32).at[:batch].set(c).at[batch].set(c_ctx)
    mods = _mods_call(cvec, ada_w, ada_b)
    lng = ln_g.reshape(depth * 2, 1, d)
    lnb = ln_b.reshape(depth * 2, 1, d)

    h_lat = x.reshape(batch * seq, d)
    h_ctx = ctx.reshape(batch * lc, d)
    lat = dict(tm=TM_MLP, row_base=0, tiles_per_row=seq // TM_MLP, alpha=alpha)
    cx = dict(tm=lc, row_base=batch, tiles_per_row=1 << 20, alpha=alpha)

    for i in range(depth):
        j = i // 2
        ctx_live = any(m % 2 == 0 for m in range(i + 1, depth))
        if i % 2 == 0:
            w_in = even_w_in[j]
            w_main = jnp.concatenate([w_in[:, :2 * d], w_in[:, 3 * d:n_main]], axis=1).astype(BF16)
            w_vt = w_in[:, 2 * d:3 * d].T.astype(BF16)
            w_dt = jnp.zeros((d, 2 * LANES), F32)
            w_dt = w_dt.at[:, :SSM_HEADS].set(w_in[:, n_main:n_main + SSM_HEADS])
            w_dt = w_dt.at[:, LANES:LANES + SSM_HEADS].set(w_in[:, n_main + SSM_HEADS:]).astype(BF16)
            pad16 = ((0, 0), (0, 0), (0, LANES - SSM_HEADS))
            dtb = jnp.pad(ssm_dt_bias[j][:, None, :], pad16)
            alog = jnp.pad(ssm_a_log[j][:, None, :], pad16)
            dskip = jnp.repeat(ssm_d[j], SSM_HEAD_DIM)[None, :]
            norm_w = ssm_norm_w[j][None, :]
            w_out = even_w_out[j].astype(BF16)
            e2 = _rpb_table(na_rpb[j])

            proj_l, vt_l, dt_l = _inproj_call(h_lat, mods, i, w_main, w_vt, w_dt, seq, tm=TM_PROJ, row_base=0,
                                              tiles_per_row=seq // TM_PROJ, name="even_in_lat")
            proj_c, vt_c, dt_c = _inproj_call(h_ctx, mods, i, w_main, w_vt, w_dt, lc, tm=lc, row_base=batch,
                                              tiles_per_row=1 << 20, name="even_in_ctx")

            attn_l = _natten_call(proj_l, vt_l, proj_c, vt_c, e2, batch, seq, lc)
            xbc_col = 2 * d + inner
            xbc_l = _dwconv_silu_call(proj_l, xbc_col, inner + 2 * gn, ssm_conv_w[j], ssm_conv_b[j], seq,
                                      tb=512, tc=512)
            xbc_c = _dwconv_silu_call(proj_c, xbc_col, inner + 2 * gn, ssm_conv_w[j], ssm_conv_b[j], lc,
                                      tb=lc, tc=512)
            h0 = jnp.zeros((batch, 2, SSM_HEADS // 2, SSM_STATE, LANES), F32)
            y_c, h_c = _ssd_call(xbc_c, dt_c, dtb, alog, h0, batch, lc)
            y_l, _ = _ssd_call(xbc_l, dt_l, dtb, alog, h_c, batch, seq)

            z_cb = 2 * d // inner
            h_lat = _evenout_call(attn_l, y_l, xbc_l, proj_l, z_cb, h_lat, mods, i, dskip, norm_w, w_out,
                                  lng, lnb, 2 * i, **lat)
            w1c = ffn_w1[j].astype(BF16)
            w3c = ffn_w3[j].astype(BF16)
            w2c = ffn_w2[j].astype(BF16)
            h_lat = _ffn_call(h_lat, mods, i, w1c, w3c, w2c, lng, lnb, 2 * i + 1, **lat)
            if ctx_live:
                attn_c = _ctxattn_call(proj_c, vt_c, batch, lc, d)
                h_ctx = _evenout_call(attn_c, y_c, xbc_c, proj_c, z_cb, h_ctx, mods, i, dskip, norm_w, w_out,
                                      lng, lnb, 2 * i, **cx)
                h_ctx = _ffn_call(h_ctx, mods, i, w1c, w3c, w2c, lng, lnb, 2 * i + 1, **cx)
        else:
            w_in = sc_w_in[j].astype(BF16)
            w_out = sc_w_out[j].astype(BF16)
            wr_pad = jnp.pad(moe_router[j], ((0, 0), (0, LANES - N_EXPERTS)))
            w1c = moe_w1[j].astype(BF16)
            w3c = moe_w3[j].astype(BF16)
            w2c = moe_w2[j].astype(BF16)
            proj_l = _inproj_call(h_lat, mods, i, w_in, None, None, seq, tm=TM_PROJ, row_base=0,
                                  tiles_per_row=seq // TM_PROJ, name="odd_in_lat")
            h_lat = _scout_call(proj_l, h_lat, mods, i, sc_conv_w[j], w_out, lng, lnb, 2 * i, seq, **lat)
            h_lat = _moe_layer(h_lat, mods, i, wr_pad, w1c, w3c, w2c, lng, lnb, 2 * i + 1, **lat)
            if ctx_live:
                proj_c = _inproj_call(h_ctx, mods, i, w_in, None, None, lc, tm=lc, row_base=batch,
                                      tiles_per_row=1 << 20, name="odd_in_ctx")
                h_ctx = _scout_call(proj_c, h_ctx, mods, i, sc_conv_w[j], w_out, lng, lnb, 2 * i, lc, **cx)
                h_ctx = _moe_layer(h_ctx, mods, i, wr_pad, w1c, w3c, w2c, lng, lnb, 2 * i + 1, **cx)
    return h_lat.reshape(batch, seq, d)
```

```python
import functools

import numpy as np
import jax
import jax.numpy as jnp
from jax import lax
from jax.experimental import pallas as pl
from jax.experimental.pallas import tpu as pltpu

F32 = jnp.float32
BF16 = jnp.bfloat16
HIGHEST = lax.Precision.HIGHEST

GRID_W = 64
NA_HEAD_DIM = 64
NA_ROWS = 8
NA_COLS = 16
SSM_HEAD_DIM = 64
SSM_HEADS = 16
SSM_GROUPS = 4
SSM_STATE = 128
SSM_CONV = 5
SSM_CHUNK = 128
SC_CONV = 3
N_EXPERTS = 8
TOP_K = 2
LN_EPS = 1e-5
RMS_EPS = 1e-5

LANES = 128
MXU_DIM = 256
BF16_SUBLANES = 16
NEG = -1e30
VMEM_LIMIT = 56 * 1024 * 1024

TM_PROJ = 512
TN_PROJ = 1024
TM_MLP = 512
NA_QR = 4
NA_WIN = NA_QR + NA_ROWS
NA_DMIN = 1 - NA_QR
NA_ND = 2 * (NA_QR + NA_ROWS - 1)
NA_BLOCKS = 4
SSD_PREP_CHUNKS = 16
MOE_TILE = 512
MOE_STAGE = 512


def _cparams(sem, vmem=None):
    return pltpu.CompilerParams(dimension_semantics=sem, vmem_limit_bytes=vmem or VMEM_LIMIT)


def _silu(x):
    return x * jax.nn.sigmoid(x)


def _softplus(x):
    return jnp.maximum(x, 0.0) + jnp.log(1.0 + jnp.exp(-jnp.abs(x)))


def _ln_residual(h, y, gate, g, b, alpha):
    v = alpha * h + gate * y
    mu = jnp.mean(v, axis=-1, keepdims=True)
    d = v - mu
    var = jnp.mean(d * d, axis=-1, keepdims=True)
    return d * lax.rsqrt(var + LN_EPS) * g + b


def _mod_row(ref, rid):
    return ref[0, pl.ds(rid, 1), :]


def _mods_body(c_ref, w_ref, b_ref, o_ref):
    s = _silu(c_ref[...])
    o_ref[0] = jnp.dot(s, w_ref[0], precision=HIGHEST, preferred_element_type=F32) + b_ref[0]


def _mods_call(cvec, ada_w, ada_b):
    depth, d, n = ada_w.shape
    tn = 1024
    return pl.pallas_call(
        _mods_body,
        grid=(depth, n // tn),
        in_specs=[pl.BlockSpec((8, d), lambda l, j: (0, 0)),
                  pl.BlockSpec((1, d, tn), lambda l, j: (l, 0, j)),
                  pl.BlockSpec((1, 1, tn), lambda l, j: (l, 0, j))],
        out_specs=pl.BlockSpec((1, 8, tn), lambda l, j: (l, 0, j)),
        out_shape=jax.ShapeDtypeStruct((depth, 8, n), F32),
        compiler_params=_cparams(("parallel", "parallel")),
        name="mods",
    )(cvec, ada_w, ada_b.reshape(depth, 1, n))


def _inproj_body(*refs, row_base, tiles_per_row, tn, even):
    if even:
        h_ref, sh_ref, sc_ref, w_ref, wvt_ref, wdt_ref, o_ref, ovt_ref, odt_ref = refs
    else:
        h_ref, sh_ref, sc_ref, w_ref, o_ref = refs
    rid = row_base + pl.program_id(0) // tiles_per_row
    u = (h_ref[...] * (1.0 + _mod_row(sc_ref, rid)) + _mod_row(sh_ref, rid)).astype(BF16)
    for n0 in range(0, w_ref.shape[1], tn):
        o_ref[:, n0:n0 + tn] = jnp.dot(u, w_ref[:, n0:n0 + tn], preferred_element_type=F32).astype(o_ref.dtype)
    if even:
        ovt_ref[...] = lax.dot_general(wvt_ref[...], u, (((1,), (1,)), ((), ())),
                                       preferred_element_type=F32).astype(ovt_ref.dtype)
        odt_ref[...] = jnp.dot(u, wdt_ref[...], preferred_element_type=F32)


def _inproj_call(h, mods, layer, w, w_vt, w_dt, seq, *, tm, row_base, tiles_per_row, name):
    t, d = h.shape
    n = w.shape[1]
    even = w_vt is not None
    body = functools.partial(_inproj_body, row_base=row_base, tiles_per_row=tiles_per_row, tn=TN_PROJ, even=even)
    resident = lambda a: pl.BlockSpec(a.shape, lambda i: (0, 0), pipeline_mode=pl.Buffered(1))
    in_specs = [pl.BlockSpec((tm, d), lambda i: (i, 0)),
                pl.BlockSpec((1, 8, d), lambda i: (layer, 0, 0)),
                pl.BlockSpec((1, 8, d), lambda i: (layer, 0, 1)),
                resident(w)]
    out_specs = [pl.BlockSpec((tm, n), lambda i: (i, 0))]
    out_shape = [jax.ShapeDtypeStruct((t, n), BF16)]
    args = [h, mods, mods, w]
    if even:
        dv = w_vt.shape[0]
        tps = seq // tm
        in_specs += [resident(w_vt), resident(w_dt)]
        out_specs += [pl.BlockSpec((dv, tm), lambda i: (i // tps, i % tps)),
                      pl.BlockSpec((tm, w_dt.shape[1]), lambda i: (i, 0))]
        out_shape += [jax.ShapeDtypeStruct((t // seq * dv, seq), BF16),
                      jax.ShapeDtypeStruct((t, w_dt.shape[1]), F32)]
        args += [w_vt, w_dt]
    outs = pl.pallas_call(
        body,
        grid=(t // tm,),
        in_specs=in_specs,
        out_specs=out_specs,
        out_shape=out_shape,
        compiler_params=_cparams(("parallel",)),
        name=name,
    )(*args)
    return tuple(outs) if even else outs[0]


def _rpb_onehot():
    qc = np.arange(GRID_W)[:, None]
    kc = np.arange(GRID_W)[None, :]
    c0 = np.clip(qc - NA_COLS // 2, 0, GRID_W - NA_COLS)
    inside = (kc >= c0) & (kc < c0 + NA_COLS)
    dc = kc - qc + NA_COLS - 1
    oh = np.zeros((LANES, GRID_W, GRID_W), np.float32)
    for d in range(2 * NA_COLS - 1):
        oh[d] = ((dc == d) & inside).astype(np.float32)
    mask = np.where(inside, 0.0, NEG).astype(np.float32)
    return oh.reshape(LANES, GRID_W * GRID_W), mask.reshape(1, GRID_W * GRID_W)


def _rpb_body(r_ref, oh_ref, m_ref, o_ref):
    o_ref[...] = jnp.dot(r_ref[...], oh_ref[...], precision=HIGHEST,
                         preferred_element_type=F32) + m_ref[...]


def _rpb_table(rpb):
    h = rpb.shape[0]
    nd = 2 * NA_ROWS - 1
    oh, mask = _rpb_onehot()
    r2 = jnp.pad(rpb.reshape(h * nd, 2 * NA_COLS - 1), ((0, 0), (0, LANES - 2 * NA_COLS + 1)))
    e = pl.pallas_call(
        _rpb_body,
        out_shape=jax.ShapeDtypeStruct((h * nd, GRID_W * GRID_W), F32),
        name="rpb_table",
    )(r2, jnp.asarray(oh), jnp.asarray(mask))
    e = jnp.swapaxes(e.reshape(h, nd, GRID_W, GRID_W), 2, 3)
    e = jnp.pad(e, ((0, 0), (1 - NA_DMIN, NA_DMIN + NA_ND - nd), (0, 0), (0, 0)),
                constant_values=NEG)
    first, second = e[:, 1:], e[:, :-1]
    neg = jnp.full_like(first, NEG)
    return jnp.concatenate([jnp.concatenate([first, second], axis=-1),
                            jnp.concatenate([first, neg], axis=-1),
                            jnp.concatenate([neg, second], axis=-1),
                            jnp.concatenate([neg, neg], axis=-1)], axis=1)


def _natten_window(r_first, rows):
    return jnp.clip(r_first - NA_ROWS // 2, 0, rows - NA_WIN)


def _natten_scores(q, r_first, k_ref, kc_ref, e_ref, s_scr, rows):
    nk = NA_WIN * GRID_W
    ws = _natten_window(r_first, rows)
    kwin = k_ref[pl.ds(pl.multiple_of(ws * GRID_W, LANES), nk), :]
    kc = kc_ref[...]
    idx = []
    for g in range(NA_QR // 2):
        r = r_first + 2 * g
        r0a = jnp.clip(r - NA_ROWS // 2, 0, rows - NA_ROWS)
        r0b = jnp.clip(r + 1 - NA_ROWS // 2, 0, rows - NA_ROWS)
        col = []
        for t in range(NA_WIN):
            kr = ws + t
            d = kr - r + (NA_ROWS - 1)
            out_first = jnp.logical_or(kr < r0a, kr >= r0a + NA_ROWS).astype(jnp.int32)
            out_second = jnp.logical_or(kr < r0b, kr >= r0b + NA_ROWS).astype(jnp.int32)
            col.append((2 * out_first + out_second) * NA_ND + d - NA_DMIN)
        idx.append(col)
    lo = lax.broadcasted_iota(jnp.int32, (1, LANES), 1) < NA_HEAD_DIM
    scale = jnp.asarray(NA_HEAD_DIM ** -0.5, BF16)
    nt = (((1,), (1,)), ((), ()))
    for a in range(2):
        sel = lo if a == 0 else jnp.logical_not(lo)
        qa = jnp.where(sel, q, jnp.zeros_like(q)) * scale
        bias = jnp.concatenate(
            [jnp.concatenate([e_ref[a, idx[g][t]] for g in range(NA_QR // 2)], axis=1) for t in range(NA_WIN)],
            axis=0)
        s_scr[a, :nk, :] = lax.dot_general(kwin, qa, nt, preferred_element_type=F32) + bias
        s_scr[a, nk:, :] = lax.dot_general(kc, qa, nt, preferred_element_type=F32)


def _natten_softmax_pv(r_first, s_scr, p_scr, vt_ref, vct_ref, rows):
    nk = NA_WIN * GRID_W
    nkb = s_scr.shape[1] // GRID_W
    ws = _natten_window(r_first, rows)
    vt_win = vt_ref[:, pl.ds(pl.multiple_of(ws * GRID_W, LANES), nk)]
    vct = vct_ref[...]
    outs = []
    for a in range(2):
        linv = []
        for g in range(NA_QR // 2):
            cols = slice(g * LANES, (g + 1) * LANES)
            mx = None
            for t in range(nkb):
                blk = s_scr[a, t * GRID_W:(t + 1) * GRID_W, cols]
                mx = blk if mx is None else jnp.maximum(mx, blk)
            m = jnp.max(mx, axis=0, keepdims=True)
            ls = None
            for t in range(nkb):
                p = jnp.exp(s_scr[a, t * GRID_W:(t + 1) * GRID_W, cols] - m)
                p_scr[a, t * GRID_W:(t + 1) * GRID_W, cols] = p.astype(BF16)
                ls = p if ls is None else ls + p
            linv.append(1.0 / jnp.sum(ls, axis=0, keepdims=True))
        hd = slice(a * NA_HEAD_DIM, (a + 1) * NA_HEAD_DIM)
        o_t = (jnp.dot(vt_win[hd, :], p_scr[a, :nk, :], preferred_element_type=F32)
               + jnp.dot(vct[hd, :], p_scr[a, nk:, :], preferred_element_type=F32))
        outs.append(o_t * jnp.concatenate(linv, axis=-1))
    return jnp.concatenate(outs, axis=0).T


def _natten_body(q0_ref, q_ref, qn_ref, k_ref, vt_ref, kc_ref, vct_ref, e_ref, o_ref,
                 sa_scr, sb_scr, pa_scr, pb_scr, *, rows, nrb):
    nq = NA_QR * GRID_W
    i = pl.program_id(2)
    b0 = NA_BLOCKS * i
    bufs = ((sa_scr, pa_scr), (sb_scr, pb_scr))

    @pl.when(i == 0)
    def _():
        _natten_scores(q0_ref[...], 0, k_ref, kc_ref, e_ref, sa_scr, rows)

    for k in range(NA_BLOCKS):
        s_cur, p_cur = bufs[k % 2]
        s_nxt = bufs[(k + 1) % 2][0]
        if k + 1 < NA_BLOCKS:
            q_nxt, b_nxt = q_ref[(k + 1) * nq:(k + 2) * nq, :], b0 + k + 1
        else:
            q_nxt, b_nxt = qn_ref[...], jnp.minimum(b0 + NA_BLOCKS, nrb - 1)
        _natten_scores(q_nxt, b_nxt * NA_QR, k_ref, kc_ref, e_ref, s_nxt, rows)
        o_ref[k * nq:(k + 1) * nq, :] = _natten_softmax_pv(
            (b0 + k) * NA_QR, s_cur, p_cur, vt_ref, vct_ref, rows).astype(o_ref.dtype)


def _natten_call(proj_l, vt_l, proj_c, vt_c, e2, batch, seq, lc):
    d = e2.shape[0] * NA_HEAD_DIM
    npair = d // LANES
    rows = seq // GRID_W
    nrb = rows // NA_QR
    nq = NA_QR * GRID_W
    nkeys = NA_WIN * GRID_W + lc
    body = functools.partial(_natten_body, rows=rows, nrb=nrb)
    nst = nrb // NA_BLOCKS
    return pl.pallas_call(
        body,
        grid=(batch, npair, nst),
        in_specs=[pl.BlockSpec((nq, LANES), lambda b, p, i: (b * nrb, p)),
                  pl.BlockSpec((NA_BLOCKS * nq, LANES), lambda b, p, i: (b * nst + i, p)),
                  pl.BlockSpec((nq, LANES),
                               lambda b, p, i: (b * nrb + jnp.minimum(NA_BLOCKS * (i + 1), nrb - 1), p)),
                  pl.BlockSpec((seq, LANES), lambda b, p, i: (b, npair + p)),
                  pl.BlockSpec((LANES, seq), lambda b, p, i: (b * npair + p, 0)),
                  pl.BlockSpec((lc, LANES), lambda b, p, i: (b, npair + p)),
                  pl.BlockSpec((LANES, lc), lambda b, p, i: (b * npair + p, 0)),
                  pl.BlockSpec((2, e2.shape[1], GRID_W, LANES), lambda b, p, i: (p, 0, 0, 0))],
        out_specs=pl.BlockSpec((NA_BLOCKS * nq, LANES), lambda b, p, i: (b * nst + i, p)),
        out_shape=jax.ShapeDtypeStruct((batch * seq, d), BF16),
        scratch_shapes=[pltpu.VMEM((2, nkeys, nq), F32), pltpu.VMEM((2, nkeys, nq), F32),
                        pltpu.VMEM((2, nkeys, nq), BF16), pltpu.VMEM((2, nkeys, nq), BF16)],
        compiler_params=_cparams(("parallel", "parallel", "arbitrary")),
        name="natten",
    )(proj_l, proj_l, proj_l, proj_l, vt_l, proj_c, vt_c, e2)


def _ctxattn_body(q_ref, k_ref, vt_ref, o_ref):
    q = q_ref[...]
    k = k_ref[...]
    vt = vt_ref[...]
    lane = lax.broadcasted_iota(jnp.int32, (1, LANES), 1)
    lo = lane < NA_HEAD_DIM
    scale = NA_HEAD_DIM ** -0.5
    nt = (((1,), (1,)), ((), ()))
    outs = []
    for a in range(2):
        sel = lo if a == 0 else jnp.logical_not(lo)
        qa = jnp.where(sel, q, jnp.zeros_like(q)) * jnp.asarray(scale, BF16)
        s = lax.dot_general(qa, k, nt, preferred_element_type=F32)
        m = jnp.max(s, axis=-1, keepdims=True)
        p = jnp.exp(s - m)
        l = jnp.sum(p, axis=-1, keepdims=True)
        outs.append(lax.dot_general(p.astype(BF16), vt, nt, preferred_element_type=F32) / l)
    o_ref[...] = jnp.where(lo, outs[0], outs[1]).astype(o_ref.dtype)


def _ctxattn_call(proj_c, vt_c, batch, lc, d):
    npair = d // LANES
    return pl.pallas_call(
        _ctxattn_body,
        grid=(batch, npair),
        in_specs=[pl.BlockSpec((lc, LANES), lambda b, p: (b, p)),
                  pl.BlockSpec((lc, LANES), lambda b, p: (b, npair + p)),
                  pl.BlockSpec((LANES, lc), lambda b, p: (b * npair + p, 0))],
        out_specs=pl.BlockSpec((lc, LANES), lambda b, p: (b, p)),
        out_shape=jax.ShapeDtypeStruct((batch * lc, d), BF16),
        compiler_params=_cparams(("parallel", "parallel")),
        name="ctx_attn",
    )(proj_c, proj_c, vt_c)


def _dwconv_silu_body(xp_ref, x_ref, xn_ref, w_ref, b_ref, o_ref, *, tiles_per_seq):
    i = pl.program_id(0)
    k = w_ref.shape[0]
    half = k // 2
    tb = x_ref.shape[0]
    hr = xp_ref.shape[0]
    first = (i % tiles_per_seq) == 0
    last = (i % tiles_per_seq) == tiles_per_seq - 1
    prev = jnp.where(first, 0.0, xp_ref[...].astype(F32))
    nxt = jnp.where(last, 0.0, xn_ref[...].astype(F32))
    ext = jnp.concatenate([prev, x_ref[...].astype(F32), nxt], axis=0)
    acc = jnp.zeros((tb, x_ref.shape[1]), F32) + b_ref[...]
    for t in range(k):
        off = hr - half + t
        acc = acc + ext[off:off + tb, :] * w_ref[pl.ds(t, 1), :]
    o_ref[...] = _silu(acc).astype(o_ref.dtype)


def _dwconv_silu_call(proj, col0, width, w, b, seq, *, tb, tc):
    t = proj.shape[0]
    hr = BF16_SUBLANES
    nrt = t // tb
    cb0 = col0 // tc
    body = functools.partial(_dwconv_silu_body, tiles_per_seq=seq // tb)
    return pl.pallas_call(
        body,
        grid=(nrt, width // tc),
        in_specs=[pl.BlockSpec((hr, tc), lambda i, j: (jnp.maximum(i * (tb // hr) - 1, 0), cb0 + j)),
                  pl.BlockSpec((tb, tc), lambda i, j: (i, cb0 + j)),
                  pl.BlockSpec((hr, tc), lambda i, j: (jnp.minimum((i + 1) * (tb // hr), t // hr - 1), cb0 + j)),
                  pl.BlockSpec((w.shape[0], tc), lambda i, j: (0, j)),
                  pl.BlockSpec((1, tc), lambda i, j: (0, j))],
        out_specs=pl.BlockSpec((tb, tc), lambda i, j: (i, j)),
        out_shape=jax.ShapeDtypeStruct((t, width), BF16),
        compiler_params=_cparams(("parallel", "parallel")),
        name="dwconv_silu",
    )(proj, proj, proj, w, b.reshape(1, width))


def _ssd_prep_body(dt_ref, dtb_ref, alog_ref, dtt_ref, cum_ref, cumt_ref, *, chunks):
    q = SSM_CHUNK
    fwd = pl.program_id(1) == 0
    ii = lax.broadcasted_iota(jnp.int32, (q, q), 0)
    jj = lax.broadcasted_iota(jnp.int32, (q, q), 1)
    tri = ((jj - ii) * jnp.where(fwd, 1, -1) <= 0).astype(F32)
    a = -jnp.exp(alog_ref[0])
    for c in range(chunks):
        rws = slice(c * q, (c + 1) * q)
        dtv = _softplus(dt_ref[rws, :] + dtb_ref[0])
        cum = jnp.dot(tri, dtv * a, precision=HIGHEST, preferred_element_type=F32)
        dtt_ref[0, rws, :] = dtv.T
        cum_ref[0, rws, :] = cum
        cumt_ref[0, rws, :] = cum.T


def _ssd_prep_call(dt, dtb, alog, batch, seq):
    nc = seq // SSM_CHUNK
    chunks = min(SSD_PREP_CHUNKS, nc)
    rows = chunks * SSM_CHUNK
    t = batch * seq
    out = jax.ShapeDtypeStruct((2, t, LANES), F32)
    ospec = pl.BlockSpec((1, rows, LANES), lambda i, d: (d, i, 0))
    return pl.pallas_call(
        functools.partial(_ssd_prep_body, chunks=chunks),
        grid=(t // rows, 2),
        in_specs=[pl.BlockSpec((rows, LANES), lambda i, d: (i, d)),
                  pl.BlockSpec((1, 1, LANES), lambda i, d: (d, 0, 0)),
                  pl.BlockSpec((1, 1, LANES), lambda i, d: (d, 0, 0))],
        out_specs=[ospec, ospec, ospec],
        out_shape=[out, out, out],
        compiler_params=_cparams(("parallel", "parallel")),
        name="ssd_prep",
    )(dt, dtb, alog)


def _ssd_body(x_ref, b_ref, c_ref, dtt_ref, cum_ref, cumt_ref, h0_ref, y_ref, hT_ref, st_scr, *, nsteps):
    q = SSM_CHUNK
    dr = pl.program_id(1)
    s = pl.program_id(2)

    @pl.when(s == 0)
    def _():
        st_scr[...] = h0_ref[0, 0]

    fwd = dr == 0
    cum = cum_ref[0]
    cum_t = cumt_ref[0]
    dt_t = dtt_ref[0]
    ii = lax.broadcasted_iota(jnp.int32, (q, q), 0)
    jj = lax.broadcasted_iota(jnp.int32, (q, q), 1)
    tri = (jj - ii) * jnp.where(fwd, 1, -1) <= 0
    tot = jnp.where(fwd, cum[q - 1:q, :], cum[0:1, :])
    tot_c = jnp.where(fwd, cum_t[:, q - 1:q], cum_t[:, 0:1])
    upd_t = dt_t * jnp.exp(tot_c - cum_t)
    lane = lax.broadcasted_iota(jnp.int32, (1, LANES), 1)
    lo = lane < SSM_HEAD_DIM
    heads_per_group = SSM_HEADS // SSM_GROUPS
    for g in range(SSM_GROUPS):
        bg = b_ref[:, g * SSM_STATE:(g + 1) * SSM_STATE]
        cg = c_ref[:, g * SSM_STATE:(g + 1) * SSM_STATE]
        cb = lax.dot_general(cg, bg, (((1,), (1,)), ((), ())), preferred_element_type=F32)
        bg_t = bg.astype(F32).T
        for hp in range(g * heads_per_group // 2, (g + 1) * heads_per_group // 2):
            ha, hb = 2 * hp, 2 * hp + 1
            xb = x_ref[:, hp * LANES:(hp + 1) * LANES]
            cca = jnp.broadcast_to(cum[:, ha:ha + 1], (q, q))
            ccb = jnp.broadcast_to(cum[:, hb:hb + 1], (q, q))
            l_a = jnp.exp(jnp.where(tri, cca - cum_t[ha:ha + 1, :], NEG))
            l_b = jnp.exp(jnp.where(tri, ccb - cum_t[hb:hb + 1, :], NEG))
            m_a = (cb * l_a * dt_t[ha:ha + 1, :]).astype(BF16)
            m_b = (cb * l_b * dt_t[hb:hb + 1, :]).astype(BF16)
            y_intra = jnp.where(lo, jnp.dot(m_a, xb, preferred_element_type=F32),
                                jnp.dot(m_b, xb, preferred_element_type=F32))
            ccp = jnp.where(lo, cca, ccb)
            st = st_scr[hp]
            y_inter = jnp.dot(cg, st.astype(BF16), preferred_element_type=F32) * jnp.exp(ccp)
            y_ref[0, :, hp * LANES:(hp + 1) * LANES] = y_intra + y_inter
            totp = jnp.where(lo, jnp.broadcast_to(tot[:, ha:ha + 1], (1, LANES)),
                             jnp.broadcast_to(tot[:, hb:hb + 1], (1, LANES)))
            bw_a = (bg_t * upd_t[ha:ha + 1, :]).astype(BF16)
            bw_b = (bg_t * upd_t[hb:hb + 1, :]).astype(BF16)
            st_scr[hp] = st * jnp.exp(totp) + jnp.where(lo, jnp.dot(bw_a, xb, preferred_element_type=F32),
                                                        jnp.dot(bw_b, xb, preferred_element_type=F32))

    @pl.when(s == nsteps - 1)
    def _():
        hT_ref[0, 0] = st_scr[...]


def _ssd_call(xbc, dt, dtb, alog, h0, batch, seq):
    inner = SSM_HEADS * SSM_HEAD_DIM
    gn = SSM_GROUPS * SSM_STATE
    nc = seq // SSM_CHUNK
    npair = SSM_HEADS // 2

    def rblk(b, d, s):
        return b * nc + jnp.where(d == 0, s, nc - 1 - s)

    dt_t, cum, cum_t = _ssd_prep_call(dt, dtb, alog, batch, seq)
    pspec = pl.BlockSpec((1, SSM_CHUNK, LANES), lambda b, d, s: (d, rblk(b, d, s), 0))
    body = functools.partial(_ssd_body, nsteps=nc)
    return pl.pallas_call(
        body,
        grid=(batch, 2, nc),
        in_specs=[pl.BlockSpec((SSM_CHUNK, inner), lambda b, d, s: (rblk(b, d, s), 0)),
                  pl.BlockSpec((SSM_CHUNK, gn), lambda b, d, s: (rblk(b, d, s), inner // gn)),
                  pl.BlockSpec((SSM_CHUNK, gn), lambda b, d, s: (rblk(b, d, s), inner // gn + 1)),
                  pspec, pspec, pspec,
                  pl.BlockSpec((1, 1, npair, SSM_STATE, LANES), lambda b, d, s: (b, d, 0, 0, 0))],
        out_specs=[pl.BlockSpec((1, SSM_CHUNK, inner), lambda b, d, s: (d, rblk(b, d, s), 0)),
                   pl.BlockSpec((1, 1, npair, SSM_STATE, LANES), lambda b, d, s: (b, d, 0, 0, 0))],
        out_shape=[jax.ShapeDtypeStruct((2, batch * seq, inner), F32),
                   jax.ShapeDtypeStruct((batch, 2, npair, SSM_STATE, LANES), F32)],
        scratch_shapes=[pltpu.VMEM((npair, SSM_STATE, LANES), F32)],
        compiler_params=_cparams(("parallel", "parallel", "arbitrary")),
        name="ssd",
    )(xbc, xbc, xbc, dt_t, cum, cum_t, h0)


def _evenout_body(attn_ref, y_ref, xs_ref, z_ref, h_ref, gate_ref, dsk_ref, nw_ref, w_ref, lg_ref, lb_ref,
                  o_ref, ssm_scr, *, row_base, tiles_per_row, alpha):
    i = pl.program_id(0)
    rid = row_base + i // tiles_per_row
    inner = xs_ref.shape[1]
    gw = inner // SSM_GROUPS
    z = z_ref[...].astype(F32)
    y = y_ref[0] + y_ref[1] + xs_ref[...].astype(F32) * dsk_ref[...]
    u = y * _silu(z)
    for g in range(SSM_GROUPS):
        ug = u[:, g * gw:(g + 1) * gw]
        ms = jnp.mean(ug * ug, axis=-1, keepdims=True)
        ssm_scr[:, g * gw:(g + 1) * gw] = (ug * lax.rsqrt(ms + RMS_EPS) * nw_ref[:, g * gw:(g + 1) * gw]).astype(BF16)
    d_attn = attn_ref.shape[1]
    acc = jnp.dot(attn_ref[...], w_ref[:d_attn, :], preferred_element_type=F32)
    acc = acc + jnp.dot(ssm_scr[...], w_ref[d_attn:, :], preferred_element_type=F32)
    o_ref[...] = _ln_residual(h_ref[...], acc, _mod_row(gate_ref, rid), lg_ref[0], lb_ref[0], alpha)


def _evenout_call(attn, y2, xbc, proj, z_cb, h, mods, layer, dskip_row, norm_w, w_out, ln_g, ln_b, ln_idx,
                  *, tm, row_base, tiles_per_row, alpha):
    t, d = h.shape
    inner = y2.shape[2]
    body = functools.partial(_evenout_body, row_base=row_base, tiles_per_row=tiles_per_row, alpha=alpha)
    return pl.pallas_call(
        body,
        grid=(t // tm,),
        in_specs=[pl.BlockSpec((tm, d), lambda i: (i, 0)),
                  pl.BlockSpec((2, tm, inner), lambda i: (0, i, 0)),
                  pl.BlockSpec((tm, inner), lambda i: (i, 0)),
                  pl.BlockSpec((tm, inner), lambda i: (i, z_cb)),
                  pl.BlockSpec((tm, d), lambda i: (i, 0)),
                  pl.BlockSpec((1, 8, d), lambda i: (layer, 0, 2)),
                  pl.BlockSpec((1, inner), lambda i: (0, 0)),
                  pl.BlockSpec((1, inner), lambda i: (0, 0)),
                  pl.BlockSpec(w_out.shape, lambda i: (0, 0)),
                  pl.BlockSpec((1, 1, d), lambda i: (ln_idx, 0, 0)),
                  pl.BlockSpec((1, 1, d), lambda i: (ln_idx, 0, 0))],
        out_specs=pl.BlockSpec((tm, d), lambda i: (i, 0)),
        out_shape=jax.ShapeDtypeStruct((t, d), F32),
        scratch_shapes=[pltpu.VMEM((tm, inner), BF16)],
        compiler_params=_cparams(("parallel",)),
        name="even_out",
    )(attn, y2, xbc, proj, h, mods, dskip_row, norm_w, w_out, ln_g, ln_b)


def _hidden_chunks(f):
    half = (f // 2 + MXU_DIM - 1) // MXU_DIM * MXU_DIM
    return ((0, half), (half, f - half))


def _swiglu(u, w1_ref, w3_ref, w2_ref):
    acc = None
    for f0, fc in _hidden_chunks(w1_ref.shape[1]):
        a = jnp.dot(u, w1_ref[:, f0:f0 + fc], preferred_element_type=F32)
        b = jnp.dot(u, w3_ref[:, f0:f0 + fc], preferred_element_type=F32)
        t = (_silu(a) * b).astype(BF16)
        y = jnp.dot(t, w2_ref[f0:f0 + fc, :], preferred_element_type=F32)
        acc = y if acc is None else acc + y
    return acc


def _ffn_body(h_ref, sh_ref, sc_ref, gate_ref, w1_ref, w3_ref, w2_ref, lg_ref, lb_ref, o_ref,
              *, row_base, tiles_per_row, alpha):
    i = pl.program_id(0)
    rid = row_base + i // tiles_per_row
    h = h_ref[...]
    u = (h * (1.0 + _mod_row(sc_ref, rid)) + _mod_row(sh_ref, rid)).astype(BF16)
    y = _swiglu(u, w1_ref, w3_ref, w2_ref)
    o_ref[...] = _ln_residual(h, y, _mod_row(gate_ref, rid), lg_ref[0], lb_ref[0], alpha)


def _ffn_call(h, mods, layer, w1c, w3c, w2c, ln_g, ln_b, ln_idx, *, tm, row_base, tiles_per_row, alpha):
    t, d = h.shape
    body = functools.partial(_ffn_body, row_base=row_base, tiles_per_row=tiles_per_row, alpha=alpha)
    wspec = lambda w: pl.BlockSpec(w.shape, lambda i: (0, 0), pipeline_mode=pl.Buffered(1))
    return pl.pallas_call(
        body,
        grid=(t // tm,),
        in_specs=[pl.BlockSpec((tm, d), lambda i: (i, 0)),
                  pl.BlockSpec((1, 8, d), lambda i: (layer, 0, 3)),
                  pl.BlockSpec((1, 8, d), lambda i: (layer, 0, 4)),
                  pl.BlockSpec((1, 8, d), lambda i: (layer, 0, 5)),
                  wspec(w1c), wspec(w3c), wspec(w2c),
                  pl.BlockSpec((1, 1, d), lambda i: (ln_idx, 0, 0)),
                  pl.BlockSpec((1, 1, d), lambda i: (ln_idx, 0, 0))],
        out_specs=pl.BlockSpec((tm, d), lambda i: (i, 0)),
        out_shape=jax.ShapeDtypeStruct((t, d), F32),
        compiler_params=_cparams(("parallel",)),
        name="ffn",
    )(h, mods, mods, mods, w1c, w3c, w2c, ln_g, ln_b)


def _scout_body(pb_ref, pcp_ref, pc_ref, pcn_ref, php_ref, ph_ref, phn_ref, h_ref, gate_ref, cw_ref, w_ref,
                lg_ref, lb_ref, o_ref, *, row_base, tiles_per_row, tiles_per_seq, alpha):
    i = pl.program_id(0)
    rid = row_base + i // tiles_per_row
    tm = ph_ref.shape[0]
    hr = php_ref.shape[0]
    first = (i % tiles_per_seq) == 0
    last = (i % tiles_per_seq) == tiles_per_seq - 1
    prev = jnp.where(first, 0.0, pcp_ref[...].astype(F32) * php_ref[...].astype(F32))
    cur = pc_ref[...].astype(F32) * ph_ref[...].astype(F32)
    nxt = jnp.where(last, 0.0, pcn_ref[...].astype(F32) * phn_ref[...].astype(F32))
    ext = jnp.concatenate([prev, cur, nxt], axis=0)
    acc = jnp.zeros_like(cur)
    for t in range(SC_CONV):
        off = hr - SC_CONV // 2 + t
        acc = acc + ext[off:off + tm, :] * cw_ref[pl.ds(t, 1), :]
    a = (pb_ref[...].astype(F32) * acc).astype(BF16)
    y = jnp.dot(a, w_ref[...], preferred_element_type=F32)
    o_ref[...] = _ln_residual(h_ref[...], y, _mod_row(gate_ref, rid), lg_ref[0], lb_ref[0], alpha)


def _scout_call(proj, h, mods, layer, conv_w, w_out, ln_g, ln_b, ln_idx, seq,
                *, tm, row_base, tiles_per_row, alpha):
    t, d = h.shape
    hr = BF16_SUBLANES
    r = tm // hr
    nh = t // hr
    body = functools.partial(_scout_body, row_base=row_base, tiles_per_row=tiles_per_row,
                             tiles_per_seq=seq // tm, alpha=alpha)
    prev_map = lambda cb: (lambda i: (jnp.maximum(i * r - 1, 0), cb))
    next_map = lambda cb: (lambda i: (jnp.minimum((i + 1) * r, nh - 1), cb))
    return pl.pallas_call(
        body,
        grid=(t // tm,),
        in_specs=[pl.BlockSpec((tm, d), lambda i: (i, 0)),
                  pl.BlockSpec((hr, d), prev_map(1)),
                  pl.BlockSpec((tm, d), lambda i: (i, 1)),
                  pl.BlockSpec((hr, d), next_map(1)),
                  pl.BlockSpec((hr, d), prev_map(2)),
                  pl.BlockSpec((tm, d), lambda i: (i, 2)),
                  pl.BlockSpec((hr, d), next_map(2)),
                  pl.BlockSpec((tm, d), lambda i: (i, 0)),
                  pl.BlockSpec((1, 8, d), lambda i: (layer, 0, 2)),
                  pl.BlockSpec(conv_w.shape, lambda i: (0, 0)),
                  pl.BlockSpec(w_out.shape, lambda i: (0, 0)),
                  pl.BlockSpec((1, 1, d), lambda i: (ln_idx, 0, 0)),
                  pl.BlockSpec((1, 1, d), lambda i: (ln_idx, 0, 0))],
        out_specs=pl.BlockSpec((tm, d), lambda i: (i, 0)),
        out_shape=jax.ShapeDtypeStruct((t, d), F32),
        compiler_params=_cparams(("parallel",)),
        name="shortconv_out",
    )(proj, proj, proj, proj, proj, proj, proj, h, mods, conv_w, w_out, ln_g, ln_b)


def _router_body(h_ref, sh_ref, sc_ref, wr_ref, u_ref, meta_ref, cnt_ref, cnt_scr, *, row_base, tiles_per_row):
    i = pl.program_id(0)
    rid = row_base + i // tiles_per_row
    tm = h_ref.shape[0]

    @pl.when(i == 0)
    def _():
        cnt_scr[...] = jnp.zeros_like(cnt_scr)

    u = h_ref[...] * (1.0 + _mod_row(sc_ref, rid)) + _mod_row(sh_ref, rid)
    u_ref[...] = u
    logits = jnp.dot(u, wr_ref[...], precision=HIGHEST, preferred_element_type=F32)
    lane = lax.broadcasted_iota(jnp.int32, (tm, LANES), 1).astype(F32)
    lg = jnp.where(lane < N_EXPERTS, logits, NEG)
    v1 = jnp.max(lg, axis=-1, keepdims=True)
    e1 = jnp.min(jnp.where(lg == v1, lane, float(LANES)), axis=-1, keepdims=True)
    lg2 = jnp.where(lane == e1, 2 * NEG, lg)
    v2 = jnp.max(lg2, axis=-1, keepdims=True)
    e2 = jnp.min(jnp.where(lg2 == v2, lane, float(LANES)), axis=-1, keepdims=True)
    g2 = 1.0 / (1.0 + jnp.exp(v1 - v2))
    g1 = 1.0 - g2
    oh1 = (lane == e1).astype(F32)
    oh2 = (lane == e2).astype(F32)
    both = (oh1 + oh2).astype(BF16)
    ii = lax.broadcasted_iota(jnp.int32, (tm, tm), 0)
    jj = lax.broadcasted_iota(jnp.int32, (tm, tm), 1)
    strict = (jj < ii).astype(BF16)
    before = jnp.dot(strict, both, preferred_element_type=F32) + cnt_scr[...]
    p1 = jnp.sum(before * oh1, axis=-1, keepdims=True)
    p2 = jnp.sum(before * oh2, axis=-1, keepdims=True)
    cnt_scr[...] = cnt_scr[...] + jnp.sum(oh1 + oh2, axis=0, keepdims=True)
    meta = jnp.where(lane == 0, e1,
           jnp.where(lane == 1, e2,
           jnp.where(lane == 2, p1,
           jnp.where(lane == 3, p2,
           jnp.where(lane == 4, g1,
           jnp.where(lane == 5, g2, 0.0))))))
    meta_ref[...] = meta
    cnt_ref[...] = jnp.broadcast_to(cnt_scr[...], cnt_ref.shape)


def _router_call(h, mods, layer, wr_pad, *, tm, row_base, tiles_per_row):
    t, d = h.shape
    body = functools.partial(_router_body, row_base=row_base, tiles_per_row=tiles_per_row)
    return pl.pallas_call(
        body,
        grid=(t // tm,),
        in_specs=[pl.BlockSpec((tm, d), lambda i: (i, 0)),
                  pl.BlockSpec((1, 8, d), lambda i: (layer, 0, 3)),
                  pl.BlockSpec((1, 8, d), lambda i: (layer, 0, 4)),
                  pl.BlockSpec(wr_pad.shape, lambda i: (0, 0))],
        out_specs=[pl.BlockSpec((tm, d), lambda i: (i, 0)),
                   pl.BlockSpec((tm, LANES), lambda i: (i, 0)),
                   pl.BlockSpec((8, LANES), lambda i: (0, 0))],
        out_shape=[jax.ShapeDtypeStruct((t, d), F32),
                   jax.ShapeDtypeStruct((t, LANES), F32),
                   jax.ShapeDtypeStruct((8, LANES), F32)],
        scratch_shapes=[pltpu.VMEM((1, LANES), F32)],
        compiler_params=_cparams(("arbitrary",)),
        name="moe_router",
    )(h, mods, mods, wr_pad)


def _dispatch_body(tv_ref, dest_ref, u_ref, rows_hbm, zbuf, sem, zsem):
    i = pl.program_id(0)
    tm = u_ref.shape[0]
    tile = zbuf.shape[0]

    @pl.when(i == 0)
    def _():
        zbuf[...] = jnp.zeros_like(zbuf)

        def fill_copy(t):
            return pltpu.make_async_copy(zbuf, rows_hbm.at[pl.ds(t * tile, tile)], zsem.at[0])

        def fill(t, c):
            @pl.when(tv_ref[t] < tile)
            def _():
                fill_copy(t).start()
            return c

        def fill_wait(t, c):
            @pl.when(tv_ref[t] < tile)
            def _():
                fill_copy(t).wait()
            return c

        lax.fori_loop(0, tv_ref.shape[0], fill, 0)
        lax.fori_loop(0, tv_ref.shape[0], fill_wait, 0)

    def row_copy(r, k):
        return pltpu.make_async_copy(u_ref.at[pl.ds(r, 1)], rows_hbm.at[pl.ds(dest_ref[0, k, r], 1)], sem.at[k])

    def issue(r, c):
        for k in range(TOP_K):
            row_copy(r, k).start()
        return c

    def drain(r, c):
        for k in range(TOP_K):
            row_copy(r, k).wait()
        return c

    lax.fori_loop(0, tm, issue, 0, unroll=8)
    lax.fori_loop(0, tm, drain, 0, unroll=8)


def _dispatch_call(u, dest_t, n_rows, tile_valid, *, tm):
    t, d = u.shape
    grid_spec = pltpu.PrefetchScalarGridSpec(
        num_scalar_prefetch=1,
        grid=(t // tm,),
        in_specs=[pl.BlockSpec((1, TOP_K, tm), lambda i, tv: (i, 0, 0), memory_space=pltpu.SMEM),
                  pl.BlockSpec((tm, d), lambda i, tv: (i, 0))],
        out_specs=pl.BlockSpec(memory_space=pl.ANY),
        scratch_shapes=[pltpu.VMEM((MOE_TILE, d), F32),
                        pltpu.SemaphoreType.DMA((TOP_K,)),
                        pltpu.SemaphoreType.DMA((1,))],
    )
    return pl.pallas_call(
        _dispatch_body,
        grid_spec=grid_spec,
        out_shape=jax.ShapeDtypeStruct((n_rows, d), F32),
        compiler_params=_cparams(("arbitrary",)),
        name="moe_dispatch",
    )(tile_valid, dest_t, u)


def _moe_body(te_ref, tv_ref, x_ref, w1_hbm, w3_hbm, w2_hbm, o_ref, w1s, w3s, w2s, stage, wsem, *, layer):
    t = pl.program_id(0)
    e = te_ref[t]
    e_prev = te_ref[jnp.maximum(t - 1, 0)]

    @pl.when(jnp.logical_or(t == 0, e != e_prev))
    def _():
        cs = stage.shape[1]
        copies = []
        for w_hbm, ws in ((w1_hbm, w1s), (w3_hbm, w3s), (w2_hbm, w2s)):
            for r0 in range(0, ws.shape[0], cs):
                for c0 in range(0, ws.shape[1], cs):
                    copies.append((w_hbm.at[layer, e, pl.ds(r0, cs), pl.ds(c0, cs)],
                                   ws.at[pl.ds(r0, cs), pl.ds(c0, cs)]))

        def dma(n):
            return pltpu.make_async_copy(copies[n][0], stage.at[n % 2], wsem.at[n % 2])

        dma(0).start()
        for n in range(len(copies)):
            if n + 1 < len(copies):
                dma(n + 1).start()
            dma(n).wait()
            copies[n][1][...] = stage[n % 2].astype(BF16)

    @pl.when(tv_ref[t] > 0)
    def _():
        o_ref[...] = _swiglu(x_ref[...].astype(BF16), w1s, w3s, w2s)

    @pl.when(tv_ref[t] == 0)
    def _():
        o_ref[...] = jnp.zeros_like(o_ref)


def _moe_call(rows_in, tile_expert, tile_valid, w1, w3, w2, wlayer):
    n_rows, d = rows_in.shape
    tr = MOE_TILE
    w1c, w3c, w2c = (jax.ShapeDtypeStruct(w.shape[1:], BF16) for w in (w1, w3, w2))
    grid_spec = pltpu.PrefetchScalarGridSpec(
        num_scalar_prefetch=2,
        grid=(n_rows // tr,),
        in_specs=[pl.BlockSpec((tr, d), lambda i, te, tv: (i, 0)),
                  pl.BlockSpec(memory_space=pl.ANY),
                  pl.BlockSpec(memory_space=pl.ANY),
                  pl.BlockSpec(memory_space=pl.ANY)],
        out_specs=pl.BlockSpec((tr, d), lambda i, te, tv: (i, 0)),
        scratch_shapes=[pltpu.VMEM(w1c.shape[1:], BF16),
                        pltpu.VMEM(w3c.shape[1:], BF16),
                        pltpu.VMEM(w2c.shape[1:], BF16),
                        pltpu.VMEM((2, MOE_STAGE, MOE_STAGE), F32),
                        pltpu.SemaphoreType.DMA((2,))],
    )
    return pl.pallas_call(
        functools.partial(_moe_body, layer=wlayer),
        grid_spec=grid_spec,
        out_shape=jax.ShapeDtypeStruct((n_rows, d), F32),
        compiler_params=_cparams(("arbitrary",)),
        name="moe_experts",
    )(tile_expert, tile_valid, rows_in, w1, w3, w2)


def _combine_body(dest_ref, rows_hbm, meta_ref, h_ref, gate_ref, lg_ref, lb_ref, o_ref, rbuf, gsem,
                  *, row_base, tiles_per_row, alpha):
    i = pl.program_id(0)
    rid = row_base + i // tiles_per_row
    tm = h_ref.shape[0]

    def row_copy(r, k):
        return pltpu.make_async_copy(rows_hbm.at[pl.ds(dest_ref[0, k, r], 1)], rbuf.at[k, pl.ds(r, 1)], gsem.at[k])

    def issue(r, c):
        for k in range(TOP_K):
            row_copy(r, k).start()
        return c

    def drain(r, c):
        for k in range(TOP_K):
            row_copy(r, k).wait()
        return c

    lax.fori_loop(0, tm, issue, 0, unroll=8)
    lax.fori_loop(0, tm, drain, 0, unroll=8)
    meta = meta_ref[...]
    y = meta[:, 4:5] * rbuf[0] + meta[:, 5:6] * rbuf[1]
    o_ref[...] = _ln_residual(h_ref[...], y, _mod_row(gate_ref, rid), lg_ref[0], lb_ref[0], alpha)


def _combine_call(dest_t, rows_out, meta, h, mods, layer, ln_g, ln_b, ln_idx, *, tm, row_base, tiles_per_row, alpha):
    t, d = h.shape
    body = functools.partial(_combine_body, row_base=row_base, tiles_per_row=tiles_per_row, alpha=alpha)
    return pl.pallas_call(
        body,
        grid=(t // tm,),
        in_specs=[pl.BlockSpec((1, TOP_K, tm), lambda i: (i, 0, 0), memory_space=pltpu.SMEM),
                  pl.BlockSpec(memory_space=pl.ANY),
                  pl.BlockSpec((tm, LANES), lambda i: (i, 0)),
                  pl.BlockSpec((tm, d), lambda i: (i, 0)),
                  pl.BlockSpec((1, 8, d), lambda i: (layer, 0, 5)),
                  pl.BlockSpec((1, 1, d), lambda i: (ln_idx, 0, 0)),
                  pl.BlockSpec((1, 1, d), lambda i: (ln_idx, 0, 0))],
        out_specs=pl.BlockSpec((tm, d), lambda i: (i, 0)),
        out_shape=jax.ShapeDtypeStruct((t, d), F32),
        scratch_shapes=[pltpu.VMEM((TOP_K, tm, d), F32),
                        pltpu.SemaphoreType.DMA((TOP_K,))],
        compiler_params=_cparams(("arbitrary",)),
        name="moe_combine",
    )(dest_t, rows_out, meta, h, mods, ln_g, ln_b)


def _moe_layer(h, mods, layer, wr_pad, w1, w3, w2, wlayer, ln_g, ln_b, ln_idx,
               *, tm, row_base, tiles_per_row, alpha):
    t, d = h.shape
    u, meta, cnt = _router_call(h, mods, layer, wr_pad, tm=tm, row_base=row_base, tiles_per_row=tiles_per_row)
    n_assign = t * TOP_K
    n_tiles = n_assign // MOE_TILE + N_EXPERTS
    counts = cnt[0, :N_EXPERTS].astype(jnp.int32)
    padded = (counts + MOE_TILE - 1) // MOE_TILE * MOE_TILE
    pad_ends = jnp.cumsum(padded)
    pad_starts = pad_ends - padded
    top_e = meta[:, 0:TOP_K].astype(jnp.int32)
    rank = meta[:, 2:2 + TOP_K].astype(jnp.int32)
    dest = pad_starts[top_e] + rank
    dest_t = jnp.swapaxes(dest.reshape(t // tm, tm, TOP_K), 1, 2)
    tile_start = jnp.arange(n_tiles, dtype=jnp.int32) * MOE_TILE
    tile_expert = jnp.minimum(jnp.sum(tile_start[:, None] >= pad_ends[None, :], axis=1),
                              N_EXPERTS - 1).astype(jnp.int32)
    seg_end = jnp.where(tile_start < pad_ends[-1], (pad_starts + counts)[tile_expert], 0)
    tile_valid = jnp.clip(seg_end - tile_start, 0, MOE_TILE).astype(jnp.int32)
    rows_in = _dispatch_call(u, dest_t, n_tiles * MOE_TILE, tile_valid, tm=tm)
    rows_out = _moe_call(rows_in, tile_expert, tile_valid, w1, w3, w2, wlayer)
    return _combine_call(dest_t, rows_out, meta, h, mods, layer, ln_g, ln_b, ln_idx,
                         tm=tm, row_base=row_base, tiles_per_row=tiles_per_row, alpha=alpha)


def kernel(x, c, ctx, c_ctx, ada_w, ada_b, ln_g, ln_b, even_w_in, na_rpb, ssm_conv_w, ssm_conv_b, ssm_dt_bias,
           ssm_a_log, ssm_d, ssm_norm_w, even_w_out, ffn_w1, ffn_w3, ffn_w2, sc_w_in, sc_conv_w, sc_w_out,
           moe_router, moe_w1, moe_w3, moe_w2):
    batch, seq, d = x.shape
    lc = ctx.shape[1]
    depth = ada_w.shape[0]
    alpha = (2 * depth) ** 0.25
    inner = SSM_HEADS * SSM_HEAD_DIM
    gn = SSM_GROUPS * SSM_STATE
    n_main = 3 * d + inner + inner + 2 * gn
    assert batch + 1 <= 8

    cvec = jnp.zeros((8, d), F32).at[:batch].set(c).at[batch].set(c_ctx)
    mods = _mods_call(cvec, ada_w, ada_b)
    lng = ln_g.reshape(depth * 2, 1, d)
    lnb = ln_b.reshape(depth * 2, 1, d)

    h_lat = x.reshape(batch * seq, d)
    h_ctx = ctx.reshape(batch * lc, d)
    lat = dict(tm=TM_MLP, row_base=0, tiles_per_row=seq // TM_MLP, alpha=alpha)
    cx = dict(tm=lc, row_base=batch, tiles_per_row=1 << 20, alpha=alpha)

    for i in range(depth):
        j = i // 2
        ctx_live = any(m % 2 == 0 for m in range(i + 1, depth))
        if i % 2 == 0:
            w_in = even_w_in[j]
            w_main = jnp.concatenate([w_in[:, :2 * d], w_in[:, 3 * d:n_main]], axis=1).astype(BF16)
            w_vt = w_in[:, 2 * d:3 * d].T.astype(BF16)
            w_dt = jnp.zeros((d, 2 * LANES), F32)
            w_dt = w_dt.at[:, :SSM_HEADS].set(w_in[:, n_main:n_main + SSM_HEADS])
            w_dt = w_dt.at[:, LANES:LANES + SSM_HEADS].set(w_in[:, n_main + SSM_HEADS:]).astype(BF16)
            pad16 = ((0, 0), (0, 0), (0, LANES - SSM_HEADS))
            dtb = jnp.pad(ssm_dt_bias[j][:, None, :], pad16)
            alog = jnp.pad(ssm_a_log[j][:, None, :], pad16)
            dskip = jnp.repeat(ssm_d[j], SSM_HEAD_DIM)[None, :]
            norm_w = ssm_norm_w[j][None, :]
            w_out = even_w_out[j].astype(BF16)
            e2 = _rpb_table(na_rpb[j])

            proj_l, vt_l, dt_l = _inproj_call(h_lat, mods, i, w_main, w_vt, w_dt, seq, tm=TM_PROJ, row_base=0,
                                              tiles_per_row=seq // TM_PROJ, name="even_in_lat")
            proj_c, vt_c, dt_c = _inproj_call(h_ctx, mods, i, w_main, w_vt, w_dt, lc, tm=lc, row_base=batch,
                                              tiles_per_row=1 << 20, name="even_in_ctx")

            attn_l = _natten_call(proj_l, vt_l, proj_c, vt_c, e2, batch, seq, lc)
            xbc_col = 2 * d + inner
            xbc_l = _dwconv_silu_call(proj_l, xbc_col, inner + 2 * gn, ssm_conv_w[j], ssm_conv_b[j], seq,
                                      tb=512, tc=512)
            xbc_c = _dwconv_silu_call(proj_c, xbc_col, inner + 2 * gn, ssm_conv_w[j], ssm_conv_b[j], lc,
                                      tb=lc, tc=512)
            h0 = jnp.zeros((batch, 2, SSM_HEADS // 2, SSM_STATE, LANES), F32)
            y_c, h_c = _ssd_call(xbc_c, dt_c, dtb, alog, h0, batch, lc)
            y_l, _ = _ssd_call(xbc_l, dt_l, dtb, alog, h_c, batch, seq)

            z_cb = 2 * d // inner
            h_lat = _evenout_call(attn_l, y_l, xbc_l, proj_l, z_cb, h_lat, mods, i, dskip, norm_w, w_out,
                                  lng, lnb, 2 * i, **lat)
            w1c = ffn_w1[j].astype(BF16)
            w3c = ffn_w3[j].astype(BF16)
            w2c = ffn_w2[j].astype(BF16)
            h_lat = _ffn_call(h_lat, mods, i, w1c, w3c, w2c, lng, lnb, 2 * i + 1, **lat)
            if ctx_live:
                attn_c = _ctxattn_call(proj_c, vt_c, batch, lc, d)
                h_ctx = _evenout_call(attn_c, y_c, xbc_c, proj_c, z_cb, h_ctx, mods, i, dskip, norm_w, w_out,
                                      lng, lnb, 2 * i, **cx)
                h_ctx = _ffn_call(h_ctx, mods, i, w1c, w3c, w2c, lng, lnb, 2 * i + 1, **cx)
        else:
            w_in = sc_w_in[j].astype(BF16)
            w_out = sc_w_out[j].astype(BF16)
            wr_pad = jnp.pad(moe_router[j], ((0, 0), (0, LANES - N_EXPERTS)))
            proj_l = _inproj_call(h_lat, mods, i, w_in, None, None, seq, tm=TM_PROJ, row_base=0,
                                  tiles_per_row=seq // TM_PROJ, name="odd_in_lat")
            h_lat = _scout_call(proj_l, h_lat, mods, i, sc_conv_w[j], w_out, lng, lnb, 2 * i, seq, **lat)
            h_lat = _moe_layer(h_lat, mods, i, wr_pad, moe_w1, moe_w3, moe_w2, j, lng, lnb, 2 * i + 1, **lat)
            if ctx_live:
                proj_c = _inproj_call(h_ctx, mods, i, w_in, None, None, lc, tm=lc, row_base=batch,
                                      tiles_per_row=1 << 20, name="odd_in_ctx")
                h_ctx = _scout_call(proj_c, h_ctx, mods, i, sc_conv_w[j], w_out, lng, lnb, 2 * i, lc, **cx)
                h_ctx = _moe_layer(h_ctx, mods, i, wr_pad, moe_w1, moe_w3, moe_w2, j, lng, lnb, 2 * i + 1, **cx)
    return h_lat.reshape(batch, seq, d)
```

```python
import functools

import numpy as np
import jax
import jax.numpy as jnp
from jax import lax
from jax.experimental import pallas as pl
from jax.experimental.pallas import tpu as pltpu

F32 = jnp.float32
BF16 = jnp.bfloat16
HIGHEST = lax.Precision.HIGHEST

GRID_W = 64
NA_HEAD_DIM = 64
NA_ROWS = 8
NA_COLS = 16
SSM_HEAD_DIM = 64
SSM_HEADS = 16
SSM_GROUPS = 4
SSM_STATE = 128
SSM_CONV = 5
SSM_CHUNK = 128
SC_CONV = 3
N_EXPERTS = 8
TOP_K = 2
LN_EPS = 1e-5
RMS_EPS = 1e-5

LANES = 128
MXU_DIM = 256
BF16_SUBLANES = 16
NEG = -1e30
VMEM_LIMIT = 56 * 1024 * 1024

TM_PROJ = 512
TN_PROJ = 1024
TM_MLP = 512
NA_QR = 4
NA_WIN = NA_QR + NA_ROWS
NA_DMIN = 1 - NA_QR
NA_ND = 2 * (NA_QR + NA_ROWS - 1)
NA_BLOCKS = 4
SSD_PREP_CHUNKS = 16
MOE_TILE = 512
MOE_STAGE = 512
MOE_STAGE_SLOTS = 6


def _cparams(sem, vmem=None):
    return pltpu.CompilerParams(dimension_semantics=sem, vmem_limit_bytes=vmem or VMEM_LIMIT)


def _silu(x):
    return x * jax.nn.sigmoid(x)


def _softplus(x):
    return jnp.maximum(x, 0.0) + jnp.log(1.0 + jnp.exp(-jnp.abs(x)))


def _ln_residual(h, y, gate, g, b, alpha):
    v = alpha * h + gate * y
    mu = jnp.mean(v, axis=-1, keepdims=True)
    d = v - mu
    var = jnp.mean(d * d, axis=-1, keepdims=True)
    return d * lax.rsqrt(var + LN_EPS) * g + b


def _mod_row(ref, rid):
    return ref[0, pl.ds(rid, 1), :]


def _mods_body(c_ref, w_ref, b_ref, o_ref):
    s = _silu(c_ref[...])
    o_ref[0] = jnp.dot(s, w_ref[0], precision=HIGHEST, preferred_element_type=F32) + b_ref[0]


def _mods_call(cvec, ada_w, ada_b):
    depth, d, n = ada_w.shape
    tn = 1024
    return pl.pallas_call(
        _mods_body,
        grid=(depth, n // tn),
        in_specs=[pl.BlockSpec((8, d), lambda l, j: (0, 0)),
                  pl.BlockSpec((1, d, tn), lambda l, j: (l, 0, j)),
                  pl.BlockSpec((1, 1, tn), lambda l, j: (l, 0, j))],
        out_specs=pl.BlockSpec((1, 8, tn), lambda l, j: (l, 0, j)),
        out_shape=jax.ShapeDtypeStruct((depth, 8, n), F32),
        compiler_params=_cparams(("parallel", "parallel")),
        name="mods",
    )(cvec, ada_w, ada_b.reshape(depth, 1, n))


def _inproj_body(*refs, row_base, tiles_per_row, tn, even):
    if even:
        h_ref, sh_ref, sc_ref, w_ref, wvt_ref, wdt_ref, o_ref, ovt_ref, odt_ref = refs
    else:
        h_ref, sh_ref, sc_ref, w_ref, o_ref = refs
    rid = row_base + pl.program_id(0) // tiles_per_row
    u = (h_ref[...] * (1.0 + _mod_row(sc_ref, rid)) + _mod_row(sh_ref, rid)).astype(BF16)
    for n0 in range(0, w_ref.shape[1], tn):
        o_ref[:, n0:n0 + tn] = jnp.dot(u, w_ref[:, n0:n0 + tn], preferred_element_type=F32).astype(o_ref.dtype)
    if even:
        ovt_ref[...] = lax.dot_general(wvt_ref[...], u, (((1,), (1,)), ((), ())),
                                       preferred_element_type=F32).astype(ovt_ref.dtype)
        odt_ref[...] = jnp.dot(u, wdt_ref[...], preferred_element_type=F32)


def _inproj_call(h, mods, layer, w, w_vt, w_dt, seq, *, tm, row_base, tiles_per_row, name):
    t, d = h.shape
    n = w.shape[1]
    even = w_vt is not None
    body = functools.partial(_inproj_body, row_base=row_base, tiles_per_row=tiles_per_row, tn=TN_PROJ, even=even)
    resident = lambda a: pl.BlockSpec(a.shape, lambda i: (0, 0), pipeline_mode=pl.Buffered(1))
    in_specs = [pl.BlockSpec((tm, d), lambda i: (i, 0)),
                pl.BlockSpec((1, 8, d), lambda i: (layer, 0, 0)),
                pl.BlockSpec((1, 8, d), lambda i: (layer, 0, 1)),
                resident(w)]
    out_specs = [pl.BlockSpec((tm, n), lambda i: (i, 0))]
    out_shape = [jax.ShapeDtypeStruct((t, n), BF16)]
    args = [h, mods, mods, w]
    if even:
        dv = w_vt.shape[0]
        tps = seq // tm
        in_specs += [resident(w_vt), resident(w_dt)]
        out_specs += [pl.BlockSpec((dv, tm), lambda i: (i // tps, i % tps)),
                      pl.BlockSpec((tm, w_dt.shape[1]), lambda i: (i, 0))]
        out_shape += [jax.ShapeDtypeStruct((t // seq * dv, seq), BF16),
                      jax.ShapeDtypeStruct((t, w_dt.shape[1]), F32)]
        args += [w_vt, w_dt]
    outs = pl.pallas_call(
        body,
        grid=(t // tm,),
        in_specs=in_specs,
        out_specs=out_specs,
        out_shape=out_shape,
        compiler_params=_cparams(("parallel",)),
        name=name,
    )(*args)
    return tuple(outs) if even else outs[0]


def _rpb_onehot():
    qc = np.arange(GRID_W)[:, None]
    kc = np.arange(GRID_W)[None, :]
    c0 = np.clip(qc - NA_COLS // 2, 0, GRID_W - NA_COLS)
    inside = (kc >= c0) & (kc < c0 + NA_COLS)
    dc = kc - qc + NA_COLS - 1
    oh = np.zeros((LANES, GRID_W, GRID_W), np.float32)
    for d in range(2 * NA_COLS - 1):
        oh[d] = ((dc == d) & inside).astype(np.float32)
    mask = np.where(inside, 0.0, NEG).astype(np.float32)
    return oh.reshape(LANES, GRID_W * GRID_W), mask.reshape(1, GRID_W * GRID_W)


def _rpb_body(r_ref, oh_ref, m_ref, o_ref):
    o_ref[...] = jnp.dot(r_ref[...], oh_ref[...], precision=HIGHEST,
                         preferred_element_type=F32) + m_ref[...]


def _rpb_table(rpb):
    h = rpb.shape[0]
    nd = 2 * NA_ROWS - 1
    oh, mask = _rpb_onehot()
    r2 = jnp.pad(rpb.reshape(h * nd, 2 * NA_COLS - 1), ((0, 0), (0, LANES - 2 * NA_COLS + 1)))
    e = pl.pallas_call(
        _rpb_body,
        out_shape=jax.ShapeDtypeStruct((h * nd, GRID_W * GRID_W), F32),
        name="rpb_table",
    )(r2, jnp.asarray(oh), jnp.asarray(mask))
    e = jnp.swapaxes(e.reshape(h, nd, GRID_W, GRID_W), 2, 3)
    e = jnp.pad(e, ((0, 0), (1 - NA_DMIN, NA_DMIN + NA_ND - nd), (0, 0), (0, 0)),
                constant_values=NEG)
    first, second = e[:, 1:], e[:, :-1]
    neg = jnp.full_like(first, NEG)
    return jnp.concatenate([jnp.concatenate([first, second], axis=-1),
                            jnp.concatenate([first, neg], axis=-1),
                            jnp.concatenate([neg, second], axis=-1),
                            jnp.concatenate([neg, neg], axis=-1)], axis=1)


def _natten_window(r_first, rows):
    return jnp.clip(r_first - NA_ROWS // 2, 0, rows - NA_WIN)


def _natten_scores(q, r_first, k_ref, kc_ref, e_ref, s_scr, rows):
    nk = NA_WIN * GRID_W
    ws = _natten_window(r_first, rows)
    kwin = k_ref[pl.ds(pl.multiple_of(ws * GRID_W, LANES), nk), :]
    kc = kc_ref[...]
    idx = []
    for g in range(NA_QR // 2):
        r = r_first + 2 * g
        r0a = jnp.clip(r - NA_ROWS // 2, 0, rows - NA_ROWS)
        r0b = jnp.clip(r + 1 - NA_ROWS // 2, 0, rows - NA_ROWS)
        col = []
        for t in range(NA_WIN):
            kr = ws + t
            d = kr - r + (NA_ROWS - 1)
            out_first = jnp.logical_or(kr < r0a, kr >= r0a + NA_ROWS).astype(jnp.int32)
            out_second = jnp.logical_or(kr < r0b, kr >= r0b + NA_ROWS).astype(jnp.int32)
            col.append((2 * out_first + out_second) * NA_ND + d - NA_DMIN)
        idx.append(col)
    lo = lax.broadcasted_iota(jnp.int32, (1, LANES), 1) < NA_HEAD_DIM
    scale = jnp.asarray(NA_HEAD_DIM ** -0.5, BF16)
    nt = (((1,), (1,)), ((), ()))
    for a in range(2):
        sel = lo if a == 0 else jnp.logical_not(lo)
        qa = jnp.where(sel, q, jnp.zeros_like(q)) * scale
        bias = jnp.concatenate(
            [jnp.concatenate([e_ref[a, idx[g][t]] for g in range(NA_QR // 2)], axis=1) for t in range(NA_WIN)],
            axis=0)
        s_loc = lax.dot_general(kwin, qa, nt, preferred_element_type=F32) + bias
        s_ctx = lax.dot_general(kc, qa, nt, preferred_element_type=F32)
        nkeys = nk + s_ctx.shape[0]
        s_scr[a, :nk, :] = s_loc
        s_scr[a, nk:nkeys, :] = s_ctx
        m = jnp.maximum(jnp.max(s_loc, axis=0, keepdims=True), jnp.max(s_ctx, axis=0, keepdims=True))
        s_scr[a, nkeys:, :] = jnp.broadcast_to(m, (s_scr.shape[1] - nkeys, m.shape[1]))


def _natten_softmax_pv(r_first, s_scr, p_scr, vt_ref, vct_ref, rows):
    nk = NA_WIN * GRID_W
    vct = vct_ref[...]
    nkeys = nk + vct.shape[1]
    nkb = nkeys // GRID_W
    ws = _natten_window(r_first, rows)
    vt_win = vt_ref[:, pl.ds(pl.multiple_of(ws * GRID_W, LANES), nk)]
    outs = []
    for a in range(2):
        linv = []
        for g in range(NA_QR // 2):
            cols = slice(g * LANES, (g + 1) * LANES)
            m = s_scr[a, nkeys:nkeys + 1, cols]
            ls = None
            for t in range(nkb):
                p = jnp.exp(s_scr[a, t * GRID_W:(t + 1) * GRID_W, cols] - m)
                p_scr[a, t * GRID_W:(t + 1) * GRID_W, cols] = p.astype(BF16)
                ls = p if ls is None else ls + p
            linv.append(1.0 / jnp.sum(ls, axis=0, keepdims=True))
        hd = slice(a * NA_HEAD_DIM, (a + 1) * NA_HEAD_DIM)
        o_t = (jnp.dot(vt_win[hd, :], p_scr[a, :nk, :], preferred_element_type=F32)
               + jnp.dot(vct[hd, :], p_scr[a, nk:nkeys, :], preferred_element_type=F32))
        outs.append(o_t * jnp.concatenate(linv, axis=-1))
    return jnp.concatenate(outs, axis=0).T


def _natten_body(q0_ref, q_ref, qn_ref, k_ref, vt_ref, kc_ref, vct_ref, e_ref, o_ref,
                 sa_scr, sb_scr, pa_scr, pb_scr, *, rows, nrb):
    nq = NA_QR * GRID_W
    i = pl.program_id(2)
    b0 = NA_BLOCKS * i
    bufs = ((sa_scr, pa_scr), (sb_scr, pb_scr))

    @pl.when(i == 0)
    def _():
        _natten_scores(q0_ref[...], 0, k_ref, kc_ref, e_ref, sa_scr, rows)

    for k in range(NA_BLOCKS):
        s_cur, p_cur = bufs[k % 2]
        s_nxt = bufs[(k + 1) % 2][0]
        if k + 1 < NA_BLOCKS:
            q_nxt, b_nxt = q_ref[(k + 1) * nq:(k + 2) * nq, :], b0 + k + 1
        else:
            q_nxt, b_nxt = qn_ref[...], jnp.minimum(b0 + NA_BLOCKS, nrb - 1)
        _natten_scores(q_nxt, b_nxt * NA_QR, k_ref, kc_ref, e_ref, s_nxt, rows)
        o_ref[k * nq:(k + 1) * nq, :] = _natten_softmax_pv(
            (b0 + k) * NA_QR, s_cur, p_cur, vt_ref, vct_ref, rows).astype(o_ref.dtype)


def _natten_call(proj_l, vt_l, proj_c, vt_c, e2, batch, seq, lc):
    d = e2.shape[0] * NA_HEAD_DIM
    npair = d // LANES
    rows = seq // GRID_W
    nrb = rows // NA_QR
    nq = NA_QR * GRID_W
    nkeys = NA_WIN * GRID_W + lc
    body = functools.partial(_natten_body, rows=rows, nrb=nrb)
    nst = nrb // NA_BLOCKS
    return pl.pallas_call(
        body,
        grid=(batch, npair, nst),
        in_specs=[pl.BlockSpec((nq, LANES), lambda b, p, i: (b * nrb, p)),
                  pl.BlockSpec((NA_BLOCKS * nq, LANES), lambda b, p, i: (b * nst + i, p)),
                  pl.BlockSpec((nq, LANES),
                               lambda b, p, i: (b * nrb + jnp.minimum(NA_BLOCKS * (i + 1), nrb - 1), p)),
                  pl.BlockSpec((seq, LANES), lambda b, p, i: (b, npair + p)),
                  pl.BlockSpec((LANES, seq), lambda b, p, i: (b * npair + p, 0)),
                  pl.BlockSpec((lc, LANES), lambda b, p, i: (b, npair + p)),
                  pl.BlockSpec((LANES, lc), lambda b, p, i: (b * npair + p, 0)),
                  pl.BlockSpec((2, e2.shape[1], GRID_W, LANES), lambda b, p, i: (p, 0, 0, 0))],
        out_specs=pl.BlockSpec((NA_BLOCKS * nq, LANES), lambda b, p, i: (b * nst + i, p)),
        out_shape=jax.ShapeDtypeStruct((batch * seq, d), BF16),
        scratch_shapes=[pltpu.VMEM((2, nkeys + 8, nq), F32), pltpu.VMEM((2, nkeys + 8, nq), F32),
                        pltpu.VMEM((2, nkeys, nq), BF16), pltpu.VMEM((2, nkeys, nq), BF16)],
        compiler_params=_cparams(("parallel", "parallel", "arbitrary")),
        name="natten",
    )(proj_l, proj_l, proj_l, proj_l, vt_l, proj_c, vt_c, e2)


def _ctxattn_body(q_ref, k_ref, vt_ref, o_ref):
    q = q_ref[...]
    k = k_ref[...]
    vt = vt_ref[...]
    lane = lax.broadcasted_iota(jnp.int32, (1, LANES), 1)
    lo = lane < NA_HEAD_DIM
    scale = NA_HEAD_DIM ** -0.5
    nt = (((1,), (1,)), ((), ()))
    outs = []
    for a in range(2):
        sel = lo if a == 0 else jnp.logical_not(lo)
        qa = jnp.where(sel, q, jnp.zeros_like(q)) * jnp.asarray(scale, BF16)
        s = lax.dot_general(qa, k, nt, preferred_element_type=F32)
        m = jnp.max(s, axis=-1, keepdims=True)
        p = jnp.exp(s - m)
        l = jnp.sum(p, axis=-1, keepdims=True)
        outs.append(lax.dot_general(p.astype(BF16), vt, nt, preferred_element_type=F32) / l)
    o_ref[...] = jnp.where(lo, outs[0], outs[1]).astype(o_ref.dtype)


def _ctxattn_call(proj_c, vt_c, batch, lc, d):
    npair = d // LANES
    return pl.pallas_call(
        _ctxattn_body,
        grid=(batch, npair),
        in_specs=[pl.BlockSpec((lc, LANES), lambda b, p: (b, p)),
                  pl.BlockSpec((lc, LANES), lambda b, p: (b, npair + p)),
                  pl.BlockSpec((LANES, lc), lambda b, p: (b * npair + p, 0))],
        out_specs=pl.BlockSpec((lc, LANES), lambda b, p: (b, p)),
        out_shape=jax.ShapeDtypeStruct((batch * lc, d), BF16),
        compiler_params=_cparams(("parallel", "parallel")),
        name="ctx_attn",
    )(proj_c, proj_c, vt_c)


def _dwconv_silu_body(xp_ref, x_ref, xn_ref, w_ref, b_ref, o_ref, *, tiles_per_seq):
    i = pl.program_id(0)
    k = w_ref.shape[0]
    half = k // 2
    tb = x_ref.shape[0]
    hr = xp_ref.shape[0]
    first = (i % tiles_per_seq) == 0
    last = (i % tiles_per_seq) == tiles_per_seq - 1
    prev = jnp.where(first, 0.0, xp_ref[...].astype(F32))
    nxt = jnp.where(last, 0.0, xn_ref[...].astype(F32))
    ext = jnp.concatenate([prev, x_ref[...].astype(F32), nxt], axis=0)
    acc = jnp.zeros((tb, x_ref.shape[1]), F32) + b_ref[...]
    for t in range(k):
        off = hr - half + t
        acc = acc + ext[off:off + tb, :] * w_ref[pl.ds(t, 1), :]
    o_ref[...] = _silu(acc).astype(o_ref.dtype)


def _dwconv_silu_call(proj, col0, width, w, b, seq, *, tb, tc):
    t = proj.shape[0]
    hr = BF16_SUBLANES
    nrt = t // tb
    cb0 = col0 // tc
    body = functools.partial(_dwconv_silu_body, tiles_per_seq=seq // tb)
    return pl.pallas_call(
        body,
        grid=(nrt, width // tc),
        in_specs=[pl.BlockSpec((hr, tc), lambda i, j: (jnp.maximum(i * (tb // hr) - 1, 0), cb0 + j)),
                  pl.BlockSpec((tb, tc), lambda i, j: (i, cb0 + j)),
                  pl.BlockSpec((hr, tc), lambda i, j: (jnp.minimum((i + 1) * (tb // hr), t // hr - 1), cb0 + j)),
                  pl.BlockSpec((w.shape[0], tc), lambda i, j: (0, j)),
                  pl.BlockSpec((1, tc), lambda i, j: (0, j))],
        out_specs=pl.BlockSpec((tb, tc), lambda i, j: (i, j)),
        out_shape=jax.ShapeDtypeStruct((t, width), BF16),
        compiler_params=_cparams(("parallel", "parallel")),
        name="dwconv_silu",
    )(proj, proj, proj, w, b.reshape(1, width))


def _ssd_prep_body(dt_ref, dtb_ref, alog_ref, dtt_ref, cum_ref, cumt_ref, *, chunks):
    q = SSM_CHUNK
    fwd = pl.program_id(1) == 0
    ii = lax.broadcasted_iota(jnp.int32, (q, q), 0)
    jj = lax.broadcasted_iota(jnp.int32, (q, q), 1)
    tri = ((jj - ii) * jnp.where(fwd, 1, -1) <= 0).astype(F32)
    a = -jnp.exp(alog_ref[0])
    for c in range(chunks):
        rws = slice(c * q, (c + 1) * q)
        dtv = _softplus(dt_ref[rws, :] + dtb_ref[0])
        cum = jnp.dot(tri, dtv * a, precision=HIGHEST, preferred_element_type=F32)
        dtt_ref[0, rws, :] = dtv.T
        cum_ref[0, rws, :] = cum
        cumt_ref[0, rws, :] = cum.T


def _ssd_prep_call(dt, dtb, alog, batch, seq):
    nc = seq // SSM_CHUNK
    chunks = min(SSD_PREP_CHUNKS, nc)
    rows = chunks * SSM_CHUNK
    t = batch * seq
    out = jax.ShapeDtypeStruct((2, t, LANES), F32)
    ospec = pl.BlockSpec((1, rows, LANES), lambda i, d: (d, i, 0))
    return pl.pallas_call(
        functools.partial(_ssd_prep_body, chunks=chunks),
        grid=(t // rows, 2),
        in_specs=[pl.BlockSpec((rows, LANES), lambda i, d: (i, d)),
                  pl.BlockSpec((1, 1, LANES), lambda i, d: (d, 0, 0)),
                  pl.BlockSpec((1, 1, LANES), lambda i, d: (d, 0, 0))],
        out_specs=[ospec, ospec, ospec],
        out_shape=[out, out, out],
        compiler_params=_cparams(("parallel", "parallel")),
        name="ssd_prep",
    )(dt, dtb, alog)


def _ssd_body(x_ref, b_ref, c_ref, dtt_ref, cum_ref, cumt_ref, h0_ref, y_ref, hT_ref, st_scr, *, nsteps):
    q = SSM_CHUNK
    dr = pl.program_id(1)
    s = pl.program_id(2)

    @pl.when(s == 0)
    def _():
        st_scr[...] = h0_ref[0, 0]

    fwd = dr == 0
    cum = cum_ref[0]
    cum_t = cumt_ref[0]
    dt_t = dtt_ref[0]
    ii = lax.broadcasted_iota(jnp.int32, (q, q), 0)
    jj = lax.broadcasted_iota(jnp.int32, (q, q), 1)
    tri = (jj - ii) * jnp.where(fwd, 1, -1) <= 0
    tot = jnp.where(fwd, cum[q - 1:q, :], cum[0:1, :])
    tot_c = jnp.where(fwd, cum_t[:, q - 1:q], cum_t[:, 0:1])
    upd_t = dt_t * jnp.exp(tot_c - cum_t)
    lane = lax.broadcasted_iota(jnp.int32, (1, LANES), 1)
    lo = lane < SSM_HEAD_DIM
    heads_per_group = SSM_HEADS // SSM_GROUPS
    for g in range(SSM_GROUPS):
        bg = b_ref[:, g * SSM_STATE:(g + 1) * SSM_STATE]
        cg = c_ref[:, g * SSM_STATE:(g + 1) * SSM_STATE]
        cb = lax.dot_general(cg, bg, (((1,), (1,)), ((), ())), preferred_element_type=F32)
        bg_t = bg.astype(F32).T
        for hp in range(g * heads_per_group // 2, (g + 1) * heads_per_group // 2):
            ha, hb = 2 * hp, 2 * hp + 1
            xb = x_ref[:, hp * LANES:(hp + 1) * LANES]
            cca = jnp.broadcast_to(cum[:, ha:ha + 1], (q, q))
            ccb = jnp.broadcast_to(cum[:, hb:hb + 1], (q, q))
            l_a = jnp.exp(jnp.where(tri, cca - cum_t[ha:ha + 1, :], NEG))
            l_b = jnp.exp(jnp.where(tri, ccb - cum_t[hb:hb + 1, :], NEG))
            m_a = (cb * l_a * dt_t[ha:ha + 1, :]).astype(BF16)
            m_b = (cb * l_b * dt_t[hb:hb + 1, :]).astype(BF16)
            y_intra = jnp.where(lo, jnp.dot(m_a, xb, preferred_element_type=F32),
                                jnp.dot(m_b, xb, preferred_element_type=F32))
            ccp = jnp.where(lo, cca, ccb)
            st = st_scr[hp]
            y_inter = jnp.dot(cg, st.astype(BF16), preferred_element_type=F32) * jnp.exp(ccp)
            y_ref[0, :, hp * LANES:(hp + 1) * LANES] = y_intra + y_inter
            totp = jnp.where(lo, jnp.broadcast_to(tot[:, ha:ha + 1], (1, LANES)),
                             jnp.broadcast_to(tot[:, hb:hb + 1], (1, LANES)))
            bw_a = (bg_t * upd_t[ha:ha + 1, :]).astype(BF16)
            bw_b = (bg_t * upd_t[hb:hb + 1, :]).astype(BF16)
            st_scr[hp] = st * jnp.exp(totp) + jnp.where(lo, jnp.dot(bw_a, xb, preferred_element_type=F32),
                                                        jnp.dot(bw_b, xb, preferred_element_type=F32))

    @pl.when(s == nsteps - 1)
    def _():
        hT_ref[0, 0] = st_scr[...]


def _ssd_call(xbc, dt, dtb, alog, h0, batch, seq):
    inner = SSM_HEADS * SSM_HEAD_DIM
    gn = SSM_GROUPS * SSM_STATE
    nc = seq // SSM_CHUNK
    npair = SSM_HEADS // 2

    def rblk(b, d, s):
        return b * nc + jnp.where(d == 0, s, nc - 1 - s)

    dt_t, cum, cum_t = _ssd_prep_call(dt, dtb, alog, batch, seq)
    pspec = pl.BlockSpec((1, SSM_CHUNK, LANES), lambda b, d, s: (d, rblk(b, d, s), 0))
    body = functools.partial(_ssd_body, nsteps=nc)
    return pl.pallas_call(
        body,
        grid=(batch, 2, nc),
        in_specs=[pl.BlockSpec((SSM_CHUNK, inner), lambda b, d, s: (rblk(b, d, s), 0)),
                  pl.BlockSpec((SSM_CHUNK, gn), lambda b, d, s: (rblk(b, d, s), inner // gn)),
                  pl.BlockSpec((SSM_CHUNK, gn), lambda b, d, s: (rblk(b, d, s), inner // gn + 1)),
                  pspec, pspec, pspec,
                  pl.BlockSpec((1, 1, npair, SSM_STATE, LANES), lambda b, d, s: (b, d, 0, 0, 0))],
        out_specs=[pl.BlockSpec((1, SSM_CHUNK, inner), lambda b, d, s: (d, rblk(b, d, s), 0)),
                   pl.BlockSpec((1, 1, npair, SSM_STATE, LANES), lambda b, d, s: (b, d, 0, 0, 0))],
        out_shape=[jax.ShapeDtypeStruct((2, batch * seq, inner), F32),
                   jax.ShapeDtypeStruct((batch, 2, npair, SSM_STATE, LANES), F32)],
        scratch_shapes=[pltpu.VMEM((npair, SSM_STATE, LANES), F32)],
        compiler_params=_cparams(("parallel", "parallel", "arbitrary")),
        name="ssd",
    )(xbc, xbc, xbc, dt_t, cum, cum_t, h0)


def _evenout_body(attn_ref, y_ref, xs_ref, z_ref, h_ref, gate_ref, dsk_ref, nw_ref, w_ref, lg_ref, lb_ref,
                  o_ref, ssm_scr, *, row_base, tiles_per_row, alpha):
    i = pl.program_id(0)
    rid = row_base + i // tiles_per_row
    inner = xs_ref.shape[1]
    gw = inner // SSM_GROUPS
    z = z_ref[...].astype(F32)
    y = y_ref[0] + y_ref[1] + xs_ref[...].astype(F32) * dsk_ref[...]
    u = y * _silu(z)
    for g in range(SSM_GROUPS):
        ug = u[:, g * gw:(g + 1) * gw]
        ms = jnp.mean(ug * ug, axis=-1, keepdims=True)
        ssm_scr[:, g * gw:(g + 1) * gw] = (ug * lax.rsqrt(ms + RMS_EPS) * nw_ref[:, g * gw:(g + 1) * gw]).astype(BF16)
    d_attn = attn_ref.shape[1]
    acc = jnp.dot(attn_ref[...], w_ref[:d_attn, :], preferred_element_type=F32)
    acc = acc + jnp.dot(ssm_scr[...], w_ref[d_attn:, :], preferred_element_type=F32)
    o_ref[...] = _ln_residual(h_ref[...], acc, _mod_row(gate_ref, rid), lg_ref[0], lb_ref[0], alpha)


def _evenout_call(attn, y2, xbc, proj, z_cb, h, mods, layer, dskip_row, norm_w, w_out, ln_g, ln_b, ln_idx,
                  *, tm, row_base, tiles_per_row, alpha):
    t, d = h.shape
    inner = y2.shape[2]
    body = functools.partial(_evenout_body, row_base=row_base, tiles_per_row=tiles_per_row, alpha=alpha)
    return pl.pallas_call(
        body,
        grid=(t // tm,),
        in_specs=[pl.BlockSpec((tm, d), lambda i: (i, 0)),
                  pl.BlockSpec((2, tm, inner), lambda i: (0, i, 0)),
                  pl.BlockSpec((tm, inner), lambda i: (i, 0)),
                  pl.BlockSpec((tm, inner), lambda i: (i, z_cb)),
                  pl.BlockSpec((tm, d), lambda i: (i, 0)),
                  pl.BlockSpec((1, 8, d), lambda i: (layer, 0, 2)),
                  pl.BlockSpec((1, inner), lambda i: (0, 0)),
                  pl.BlockSpec((1, inner), lambda i: (0, 0)),
                  pl.BlockSpec(w_out.shape, lambda i: (0, 0)),
                  pl.BlockSpec((1, 1, d), lambda i: (ln_idx, 0, 0)),
                  pl.BlockSpec((1, 1, d), lambda i: (ln_idx, 0, 0))],
        out_specs=pl.BlockSpec((tm, d), lambda i: (i, 0)),
        out_shape=jax.ShapeDtypeStruct((t, d), F32),
        scratch_shapes=[pltpu.VMEM((tm, inner), BF16)],
        compiler_params=_cparams(("parallel",)),
        name="even_out",
    )(attn, y2, xbc, proj, h, mods, dskip_row, norm_w, w_out, ln_g, ln_b)


def _hidden_chunks(f):
    half = (f // 2 + MXU_DIM - 1) // MXU_DIM * MXU_DIM
    return ((0, half), (half, f - half))


def _swiglu(u, w1_ref, w3_ref, w2_ref):
    acc = None
    for f0, fc in _hidden_chunks(w1_ref.shape[1]):
        a = jnp.dot(u, w1_ref[:, f0:f0 + fc], preferred_element_type=F32)
        b = jnp.dot(u, w3_ref[:, f0:f0 + fc], preferred_element_type=F32)
        t = (_silu(a) * b).astype(BF16)
        y = jnp.dot(t, w2_ref[f0:f0 + fc, :], preferred_element_type=F32)
        acc = y if acc is None else acc + y
    return acc


def _ffn_body(h_ref, sh_ref, sc_ref, gate_ref, w1_ref, w3_ref, w2_ref, lg_ref, lb_ref, o_ref,
              *, row_base, tiles_per_row, alpha):
    i = pl.program_id(0)
    rid = row_base + i // tiles_per_row
    h = h_ref[...]
    u = (h * (1.0 + _mod_row(sc_ref, rid)) + _mod_row(sh_ref, rid)).astype(BF16)
    y = _swiglu(u, w1_ref, w3_ref, w2_ref)
    o_ref[...] = _ln_residual(h, y, _mod_row(gate_ref, rid), lg_ref[0], lb_ref[0], alpha)


def _ffn_call(h, mods, layer, w1c, w3c, w2c, ln_g, ln_b, ln_idx, *, tm, row_base, tiles_per_row, alpha):
    t, d = h.shape
    body = functools.partial(_ffn_body, row_base=row_base, tiles_per_row=tiles_per_row, alpha=alpha)
    wspec = lambda w: pl.BlockSpec(w.shape, lambda i: (0, 0), pipeline_mode=pl.Buffered(1))
    return pl.pallas_call(
        body,
        grid=(t // tm,),
        in_specs=[pl.BlockSpec((tm, d), lambda i: (i, 0)),
                  pl.BlockSpec((1, 8, d), lambda i: (layer, 0, 3)),
                  pl.BlockSpec((1, 8, d), lambda i: (layer, 0, 4)),
                  pl.BlockSpec((1, 8, d), lambda i: (layer, 0, 5)),
                  wspec(w1c), wspec(w3c), wspec(w2c),
                  pl.BlockSpec((1, 1, d), lambda i: (ln_idx, 0, 0)),
                  pl.BlockSpec((1, 1, d), lambda i: (ln_idx, 0, 0))],
        out_specs=pl.BlockSpec((tm, d), lambda i: (i, 0)),
        out_shape=jax.ShapeDtypeStruct((t, d), F32),
        compiler_params=_cparams(("parallel",)),
        name="ffn",
    )(h, mods, mods, mods, w1c, w3c, w2c, ln_g, ln_b)


def _scout_body(pb_ref, pcp_ref, pc_ref, pcn_ref, php_ref, ph_ref, phn_ref, h_ref, gate_ref, cw_ref, w_ref,
                lg_ref, lb_ref, o_ref, *, row_base, tiles_per_row, tiles_per_seq, alpha):
    i = pl.program_id(0)
    rid = row_base + i // tiles_per_row
    tm = ph_ref.shape[0]
    hr = php_ref.shape[0]
    first = (i % tiles_per_seq) == 0
    last = (i % tiles_per_seq) == tiles_per_seq - 1
    prev = jnp.where(first, 0.0, pcp_ref[...].astype(F32) * php_ref[...].astype(F32))
    cur = pc_ref[...].astype(F32) * ph_ref[...].astype(F32)
    nxt = jnp.where(last, 0.0, pcn_ref[...].astype(F32) * phn_ref[...].astype(F32))
    ext = jnp.concatenate([prev, cur, nxt], axis=0)
    acc = jnp.zeros_like(cur)
    for t in range(SC_CONV):
        off = hr - SC_CONV // 2 + t
        acc = acc + ext[off:off + tm, :] * cw_ref[pl.ds(t, 1), :]
    a = (pb_ref[...].astype(F32) * acc).astype(BF16)
    y = jnp.dot(a, w_ref[...], preferred_element_type=F32)
    o_ref[...] = _ln_residual(h_ref[...], y, _mod_row(gate_ref, rid), lg_ref[0], lb_ref[0], alpha)


def _scout_call(proj, h, mods, layer, conv_w, w_out, ln_g, ln_b, ln_idx, seq,
                *, tm, row_base, tiles_per_row, alpha):
    t, d = h.shape
    hr = BF16_SUBLANES
    r = tm // hr
    nh = t // hr
    body = functools.partial(_scout_body, row_base=row_base, tiles_per_row=tiles_per_row,
                             tiles_per_seq=seq // tm, alpha=alpha)
    prev_map = lambda cb: (lambda i: (jnp.maximum(i * r - 1, 0), cb))
    next_map = lambda cb: (lambda i: (jnp.minimum((i + 1) * r, nh - 1), cb))
    return pl.pallas_call(
        body,
        grid=(t // tm,),
        in_specs=[pl.BlockSpec((tm, d), lambda i: (i, 0)),
                  pl.BlockSpec((hr, d), prev_map(1)),
                  pl.BlockSpec((tm, d), lambda i: (i, 1)),
                  pl.BlockSpec((hr, d), next_map(1)),
                  pl.BlockSpec((hr, d), prev_map(2)),
                  pl.BlockSpec((tm, d), lambda i: (i, 2)),
                  pl.BlockSpec((hr, d), next_map(2)),
                  pl.BlockSpec((tm, d), lambda i: (i, 0)),
                  pl.BlockSpec((1, 8, d), lambda i: (layer, 0, 2)),
                  pl.BlockSpec(conv_w.shape, lambda i: (0, 0)),
                  pl.BlockSpec(w_out.shape, lambda i: (0, 0)),
                  pl.BlockSpec((1, 1, d), lambda i: (ln_idx, 0, 0)),
                  pl.BlockSpec((1, 1, d), lambda i: (ln_idx, 0, 0))],
        out_specs=pl.BlockSpec((tm, d), lambda i: (i, 0)),
        out_shape=jax.ShapeDtypeStruct((t, d), F32),
        compiler_params=_cparams(("parallel",)),
        name="shortconv_out",
    )(proj, proj, proj, proj, proj, proj, proj, h, mods, conv_w, w_out, ln_g, ln_b)


def _router_body(h_ref, sh_ref, sc_ref, wr_ref, u_ref, meta_ref, cnt_ref, cnt_scr, *, row_base, tiles_per_row):
    i = pl.program_id(0)
    rid = row_base + i // tiles_per_row
    tm = h_ref.shape[0]

    @pl.when(i == 0)
    def _():
        cnt_scr[...] = jnp.zeros_like(cnt_scr)

    u = h_ref[...] * (1.0 + _mod_row(sc_ref, rid)) + _mod_row(sh_ref, rid)
    u_ref[...] = u
    logits = jnp.dot(u, wr_ref[...], precision=HIGHEST, preferred_element_type=F32)
    lane = lax.broadcasted_iota(jnp.int32, (tm, LANES), 1).astype(F32)
    lg = jnp.where(lane < N_EXPERTS, logits, NEG)
    v1 = jnp.max(lg, axis=-1, keepdims=True)
    e1 = jnp.min(jnp.where(lg == v1, lane, float(LANES)), axis=-1, keepdims=True)
    lg2 = jnp.where(lane == e1, 2 * NEG, lg)
    v2 = jnp.max(lg2, axis=-1, keepdims=True)
    e2 = jnp.min(jnp.where(lg2 == v2, lane, float(LANES)), axis=-1, keepdims=True)
    g2 = 1.0 / (1.0 + jnp.exp(v1 - v2))
    g1 = 1.0 - g2
    oh1 = (lane == e1).astype(F32)
    oh2 = (lane == e2).astype(F32)
    both = (oh1 + oh2).astype(BF16)
    ii = lax.broadcasted_iota(jnp.int32, (tm, tm), 0)
    jj = lax.broadcasted_iota(jnp.int32, (tm, tm), 1)
    strict = (jj < ii).astype(BF16)
    before = jnp.dot(strict, both, preferred_element_type=F32) + cnt_scr[...]
    p1 = jnp.sum(before * oh1, axis=-1, keepdims=True)
    p2 = jnp.sum(before * oh2, axis=-1, keepdims=True)
    cnt_scr[...] = cnt_scr[...] + jnp.sum(oh1 + oh2, axis=0, keepdims=True)
    meta = jnp.where(lane == 0, e1,
           jnp.where(lane == 1, e2,
           jnp.where(lane == 2, p1,
           jnp.where(lane == 3, p2,
           jnp.where(lane == 4, g1,
           jnp.where(lane == 5, g2, 0.0))))))
    meta_ref[...] = meta
    cnt_ref[...] = jnp.broadcast_to(cnt_scr[...], cnt_ref.shape)


def _router_call(h, mods, layer, wr_pad, *, tm, row_base, tiles_per_row):
    t, d = h.shape
    body = functools.partial(_router_body, row_base=row_base, tiles_per_row=tiles_per_row)
    return pl.pallas_call(
        body,
        grid=(t // tm,),
        in_specs=[pl.BlockSpec((tm, d), lambda i: (i, 0)),
                  pl.BlockSpec((1, 8, d), lambda i: (layer, 0, 3)),
                  pl.BlockSpec((1, 8, d), lambda i: (layer, 0, 4)),
                  pl.BlockSpec(wr_pad.shape, lambda i: (0, 0))],
        out_specs=[pl.BlockSpec((tm, d), lambda i: (i, 0)),
                   pl.BlockSpec((tm, LANES), lambda i: (i, 0)),
                   pl.BlockSpec((8, LANES), lambda i: (0, 0))],
        out_shape=[jax.ShapeDtypeStruct((t, d), F32),
                   jax.ShapeDtypeStruct((t, LANES), F32),
                   jax.ShapeDtypeStruct((8, LANES), F32)],
        scratch_shapes=[pltpu.VMEM((1, LANES), F32)],
        compiler_params=_cparams(("arbitrary",)),
        name="moe_router",
    )(h, mods, mods, wr_pad)


def _dispatch_body(tv_ref, dest_ref, u_ref, rows_hbm, zbuf, sem, zsem):
    i = pl.program_id(0)
    tm = u_ref.shape[0]
    tile = zbuf.shape[0]

    @pl.when(i == 0)
    def _():
        zbuf[...] = jnp.zeros_like(zbuf)

        def fill_copy(t):
            return pltpu.make_async_copy(zbuf, rows_hbm.at[pl.ds(t * tile, tile)], zsem.at[0])

        def fill(t, c):
            @pl.when(tv_ref[t] < tile)
            def _():
                fill_copy(t).start()
            return c

        def fill_wait(t, c):
            @pl.when(tv_ref[t] < tile)
            def _():
                fill_copy(t).wait()
            return c

        lax.fori_loop(0, tv_ref.shape[0], fill, 0)
        lax.fori_loop(0, tv_ref.shape[0], fill_wait, 0)

    def row_copy(r, k):
        return pltpu.make_async_copy(u_ref.at[pl.ds(r, 1)], rows_hbm.at[pl.ds(dest_ref[0, k, r], 1)], sem.at[k])

    def issue(r, c):
        for k in range(TOP_K):
            row_copy(r, k).start()
        return c

    def drain(r, c):
        for k in range(TOP_K):
            row_copy(r, k).wait()
        return c

    lax.fori_loop(0, tm, issue, 0, unroll=8)
    lax.fori_loop(0, tm, drain, 0, unroll=8)


def _dispatch_call(u, dest_t, n_rows, tile_valid, *, tm):
    t, d = u.shape
    grid_spec = pltpu.PrefetchScalarGridSpec(
        num_scalar_prefetch=1,
        grid=(t // tm,),
        in_specs=[pl.BlockSpec((1, TOP_K, tm), lambda i, tv: (i, 0, 0), memory_space=pltpu.SMEM),
                  pl.BlockSpec((tm, d), lambda i, tv: (i, 0))],
        out_specs=pl.BlockSpec(memory_space=pl.ANY),
        scratch_shapes=[pltpu.VMEM((MOE_TILE, d), F32),
                        pltpu.SemaphoreType.DMA((TOP_K,)),
                        pltpu.SemaphoreType.DMA((1,))],
    )
    return pl.pallas_call(
        _dispatch_body,
        grid_spec=grid_spec,
        out_shape=jax.ShapeDtypeStruct((n_rows, d), F32),
        compiler_params=_cparams(("arbitrary",)),
        name="moe_dispatch",
    )(tile_valid, dest_t, u)


def _moe_body(te_ref, tv_ref, x_ref, w1_hbm, w3_hbm, w2_hbm, o_ref, w1s, w3s, w2s, stage, wsem, *, layer):
    t = pl.program_id(0)
    e = te_ref[t]
    e_prev = te_ref[jnp.maximum(t - 1, 0)]

    @pl.when(jnp.logical_or(t == 0, e != e_prev))
    def _():
        nslot, cs, _ = stage.shape
        copies = []
        for w_hbm, ws in ((w1_hbm, w1s), (w3_hbm, w3s), (w2_hbm, w2s)):
            for r0 in range(0, ws.shape[0], cs):
                for c0 in range(0, ws.shape[1], cs):
                    copies.append((w_hbm.at[layer, e, pl.ds(r0, cs), pl.ds(c0, cs)],
                                   ws.at[pl.ds(r0, cs), pl.ds(c0, cs)]))

        def dma(n):
            return pltpu.make_async_copy(copies[n][0], stage.at[n % nslot], wsem.at[n % nslot])

        for n in range(nslot - 1):
            dma(n).start()
        for n in range(len(copies)):
            if n + nslot - 1 < len(copies):
                dma(n + nslot - 1).start()
            dma(n).wait()
            copies[n][1][...] = stage[n % nslot].astype(BF16)

    @pl.when(tv_ref[t] > 0)
    def _():
        o_ref[...] = _swiglu(x_ref[...].astype(BF16), w1s, w3s, w2s)

    @pl.when(tv_ref[t] == 0)
    def _():
        o_ref[...] = jnp.zeros_like(o_ref)


def _moe_call(rows_in, tile_expert, tile_valid, w1, w3, w2, wlayer):
    n_rows, d = rows_in.shape
    tr = MOE_TILE
    w1c, w3c, w2c = (jax.ShapeDtypeStruct(w.shape[1:], BF16) for w in (w1, w3, w2))
    grid_spec = pltpu.PrefetchScalarGridSpec(
        num_scalar_prefetch=2,
        grid=(n_rows // tr,),
        in_specs=[pl.BlockSpec((tr, d), lambda i, te, tv: (i, 0)),
                  pl.BlockSpec(memory_space=pl.ANY),
                  pl.BlockSpec(memory_space=pl.ANY),
                  pl.BlockSpec(memory_space=pl.ANY)],
        out_specs=pl.BlockSpec((tr, d), lambda i, te, tv: (i, 0)),
        scratch_shapes=[pltpu.VMEM(w1c.shape[1:], BF16),
                        pltpu.VMEM(w3c.shape[1:], BF16),
                        pltpu.VMEM(w2c.shape[1:], BF16),
                        pltpu.VMEM((MOE_STAGE_SLOTS, MOE_STAGE, MOE_STAGE), F32),
                        pltpu.SemaphoreType.DMA((MOE_STAGE_SLOTS,))],
    )
    return pl.pallas_call(
        functools.partial(_moe_body, layer=wlayer),
        grid_spec=grid_spec,
        out_shape=jax.ShapeDtypeStruct((n_rows, d), F32),
        compiler_params=_cparams(("arbitrary",)),
        name="moe_experts",
    )(tile_expert, tile_valid, rows_in, w1, w3, w2)


def _combine_body(dest_ref, rows_hbm, meta_ref, h_ref, gate_ref, lg_ref, lb_ref, o_ref, rbuf, gsem,
                  *, row_base, tiles_per_row, alpha):
    i = pl.program_id(0)
    rid = row_base + i // tiles_per_row
    tm = h_ref.shape[0]

    def row_copy(r, k):
        return pltpu.make_async_copy(rows_hbm.at[pl.ds(dest_ref[0, k, r], 1)], rbuf.at[k, pl.ds(r, 1)], gsem.at[k])

    def issue(r, c):
        for k in range(TOP_K):
            row_copy(r, k).start()
        return c

    def drain(r, c):
        for k in range(TOP_K):
            row_copy(r, k).wait()
        return c

    lax.fori_loop(0, tm, issue, 0, unroll=8)
    lax.fori_loop(0, tm, drain, 0, unroll=8)
    meta = meta_ref[...]
    y = meta[:, 4:5] * rbuf[0] + meta[:, 5:6] * rbuf[1]
    o_ref[...] = _ln_residual(h_ref[...], y, _mod_row(gate_ref, rid), lg_ref[0], lb_ref[0], alpha)


def _combine_call(dest_t, rows_out, meta, h, mods, layer, ln_g, ln_b, ln_idx, *, tm, row_base, tiles_per_row, alpha):
    t, d = h.shape
    body = functools.partial(_combine_body, row_base=row_base, tiles_per_row=tiles_per_row, alpha=alpha)
    return pl.pallas_call(
        body,
        grid=(t // tm,),
        in_specs=[pl.BlockSpec((1, TOP_K, tm), lambda i: (i, 0, 0), memory_space=pltpu.SMEM),
                  pl.BlockSpec(memory_space=pl.ANY),
                  pl.BlockSpec((tm, LANES), lambda i: (i, 0)),
                  pl.BlockSpec((tm, d), lambda i: (i, 0)),
                  pl.BlockSpec((1, 8, d), lambda i: (layer, 0, 5)),
                  pl.BlockSpec((1, 1, d), lambda i: (ln_idx, 0, 0)),
                  pl.BlockSpec((1, 1, d), lambda i: (ln_idx, 0, 0))],
        out_specs=pl.BlockSpec((tm, d), lambda i: (i, 0)),
        out_shape=jax.ShapeDtypeStruct((t, d), F32),
        scratch_shapes=[pltpu.VMEM((TOP_K, tm, d), F32),
                        pltpu.SemaphoreType.DMA((TOP_K,))],
        compiler_params=_cparams(("arbitrary",)),
        name="moe_combine",
    )(dest_t, rows_out, meta, h, mods, ln_g, ln_b)


def _moe_layer(h, mods, layer, wr_pad, w1, w3, w2, wlayer, ln_g, ln_b, ln_idx,
               *, tm, row_base, tiles_per_row, alpha):
    t, d = h.shape
    u, meta, cnt = _router_call(h, mods, layer, wr_pad, tm=tm, row_base=row_base, tiles_per_row=tiles_per_row)
    n_assign = t * TOP_K
    n_tiles = n_assign // MOE_TILE + N_EXPERTS
    counts = cnt[0, :N_EXPERTS].astype(jnp.int32)
    padded = (counts + MOE_TILE - 1) // MOE_TILE * MOE_TILE
    pad_ends = jnp.cumsum(padded)
    pad_starts = pad_ends - padded
    top_e = meta[:, 0:TOP_K].astype(jnp.int32)
    rank = meta[:, 2:2 + TOP_K].astype(jnp.int32)
    dest = pad_starts[top_e] + rank
    dest_t = jnp.swapaxes(dest.reshape(t // tm, tm, TOP_K), 1, 2)
    tile_start = jnp.arange(n_tiles, dtype=jnp.int32) * MOE_TILE
    tile_expert = jnp.minimum(jnp.sum(tile_start[:, None] >= pad_ends[None, :], axis=1),
                              N_EXPERTS - 1).astype(jnp.int32)
    seg_end = jnp.where(tile_start < pad_ends[-1], (pad_starts + counts)[tile_expert], 0)
    tile_valid = jnp.clip(seg_end - tile_start, 0, MOE_TILE).astype(jnp.int32)
    rows_in = _dispatch_call(u, dest_t, n_tiles * MOE_TILE, tile_valid, tm=tm)
    rows_out = _moe_call(rows_in, tile_expert, tile_valid, w1, w3, w2, wlayer)
    return _combine_call(dest_t, rows_out, meta, h, mods, layer, ln_g, ln_b, ln_idx,
                         tm=tm, row_base=row_base, tiles_per_row=tiles_per_row, alpha=alpha)


def kernel(x, c, ctx, c_ctx, ada_w, ada_b, ln_g, ln_b, even_w_in, na_rpb, ssm_conv_w, ssm_conv_b, ssm_dt_bias,
           ssm_a_log, ssm_d, ssm_norm_w, even_w_out, ffn_w1, ffn_w3, ffn_w2, sc_w_in, sc_conv_w, sc_w_out,
           moe_router, moe_w1, moe_w3, moe_w2):
    batch, seq, d = x.shape
    lc = ctx.shape[1]
    depth = ada_w.shape[0]
    alpha = (2 * depth) ** 0.25
    inner = SSM_HEADS * SSM_HEAD_DIM
    gn = SSM_GROUPS * SSM_STATE
    n_main = 3 * d + inner + inner + 2 * gn
    assert batch + 1 <= 8

    cvec = jnp.zeros((8, d), F32).at[:batch].set(c).at[batch].set(c_ctx)
    mods = _mods_call(cvec, ada_w, ada_b)
    lng = ln_g.reshape(depth * 2, 1, d)
    lnb = ln_b.reshape(depth * 2, 1, d)

    h_lat = x.reshape(batch * seq, d)
    h_ctx = ctx.reshape(batch * lc, d)
    lat = dict(tm=TM_MLP, row_base=0, tiles_per_row=seq // TM_MLP, alpha=alpha)
    cx = dict(tm=lc, row_base=batch, tiles_per_row=1 << 20, alpha=alpha)

    for i in range(depth):
        j = i // 2
        ctx_live = any(m % 2 == 0 for m in range(i + 1, depth))
        if i % 2 == 0:
            w_in = even_w_in[j]
            w_main = jnp.concatenate([w_in[:, :2 * d], w_in[:, 3 * d:n_main]], axis=1).astype(BF16)
            w_vt = w_in[:, 2 * d:3 * d].T.astype(BF16)
            w_dt = jnp.zeros((d, 2 * LANES), F32)
            w_dt = w_dt.at[:, :SSM_HEADS].set(w_in[:, n_main:n_main + SSM_HEADS])
            w_dt = w_dt.at[:, LANES:LANES + SSM_HEADS].set(w_in[:, n_main + SSM_HEADS:]).astype(BF16)
            pad16 = ((0, 0), (0, 0), (0, LANES - SSM_HEADS))
            dtb = jnp.pad(ssm_dt_bias[j][:, None, :], pad16)
            alog = jnp.pad(ssm_a_log[j][:, None, :], pad16)
            dskip = jnp.repeat(ssm_d[j], SSM_HEAD_DIM)[None, :]
            norm_w = ssm_norm_w[j][None, :]
            w_out = even_w_out[j].astype(BF16)
            e2 = _rpb_table(na_rpb[j])

            proj_l, vt_l, dt_l = _inproj_call(h_lat, mods, i, w_main, w_vt, w_dt, seq, tm=TM_PROJ, row_base=0,
                                              tiles_per_row=seq // TM_PROJ, name="even_in_lat")
            proj_c, vt_c, dt_c = _inproj_call(h_ctx, mods, i, w_main, w_vt, w_dt, lc, tm=lc, row_base=batch,
                                              tiles_per_row=1 << 20, name="even_in_ctx")

            attn_l = _natten_call(proj_l, vt_l, proj_c, vt_c, e2, batch, seq, lc)
            xbc_col = 2 * d + inner
            xbc_l = _dwconv_silu_call(proj_l, xbc_col, inner + 2 * gn, ssm_conv_w[j], ssm_conv_b[j], seq,
                                      tb=512, tc=512)
            xbc_c = _dwconv_silu_call(proj_c, xbc_col, inner + 2 * gn, ssm_conv_w[j], ssm_conv_b[j], lc,
                                      tb=lc, tc=512)
            h0 = jnp.zeros((batch, 2, SSM_HEADS // 2, SSM_STATE, LANES), F32)
            y_c, h_c = _ssd_call(xbc_c, dt_c, dtb, alog, h0, batch, lc)
            y_l, _ = _ssd_call(xbc_l, dt_l, dtb, alog, h_c, batch, seq)

            z_cb = 2 * d // inner
            h_lat = _evenout_call(attn_l, y_l, xbc_l, proj_l, z_cb, h_lat, mods, i, dskip, norm_w, w_out,
                                  lng, lnb, 2 * i, **lat)
            w1c = ffn_w1[j].astype(BF16)
            w3c = ffn_w3[j].astype(BF16)
            w2c = ffn_w2[j].astype(BF16)
            h_lat = _ffn_call(h_lat, mods, i, w1c, w3c, w2c, lng, lnb, 2 * i + 1, **lat)
            if ctx_live:
                attn_c = _ctxattn_call(proj_c, vt_c, batch, lc, d)
                h_ctx = _evenout_call(attn_c, y_c, xbc_c, proj_c, z_cb, h_ctx, mods, i, dskip, norm_w, w_out,
                                      lng, lnb, 2 * i, **cx)
                h_ctx = _ffn_call(h_ctx, mods, i, w1c, w3c, w2c, lng, lnb, 2 * i + 1, **cx)
        else:
            w_in = sc_w_in[j].astype(BF16)
            w_out = sc_w_out[j].astype(BF16)
            wr_pad = jnp.pad(moe_router[j], ((0, 0), (0, LANES - N_EXPERTS)))
            proj_l = _inproj_call(h_lat, mods, i, w_in, None, None, seq, tm=TM_PROJ, row_base=0,
                                  tiles_per_row=seq // TM_PROJ, name="odd_in_lat")
            h_lat = _scout_call(proj_l, h_lat, mods, i, sc_conv_w[j], w_out, lng, lnb, 2 * i, seq, **lat)
            h_lat = _moe_layer(h_lat, mods, i, wr_pad, moe_w1, moe_w3, moe_w2, j, lng, lnb, 2 * i + 1, **lat)
            if ctx_live:
                proj_c = _inproj_call(h_ctx, mods, i, w_in, None, None, lc, tm=lc, row_base=batch,
                                      tiles_per_row=1 << 20, name="odd_in_ctx")
                h_ctx = _scout_call(proj_c, h_ctx, mods, i, sc_conv_w[j], w_out, lng, lnb, 2 * i, lc, **cx)
                h_ctx = _moe_layer(h_ctx, mods, i, wr_pad, moe_w1, moe_w3, moe_w2, j, lng, lnb, 2 * i + 1, **cx)
    return h_lat.reshape(batch, seq, d)
```

```python
import functools

import numpy as np
import jax
import jax.numpy as jnp
from jax import lax
from jax.experimental import pallas as pl
from jax.experimental.pallas import tpu as pltpu

F32 = jnp.float32
BF16 = jnp.bfloat16
HIGHEST = lax.Precision.HIGHEST

GRID_W = 64
NA_HEAD_DIM = 64
NA_ROWS = 8
NA_COLS = 16
SSM_HEAD_DIM = 64
SSM_HEADS = 16
SSM_GROUPS = 4
SSM_STATE = 128
SSM_CONV = 5
SSM_CHUNK = 128
SC_CONV = 3
N_EXPERTS = 8
TOP_K = 2
LN_EPS = 1e-5
RMS_EPS = 1e-5

LANES = 128
MXU_DIM = 256
BF16_SUBLANES = 16
NEG = -1e30
VMEM_LIMIT = 56 * 1024 * 1024

TM_PROJ = 512
TN_PROJ = 1024
TM_MLP = 512
NA_QR = 4
NA_WIN = NA_QR + NA_ROWS
NA_DMIN = 1 - NA_QR
NA_ND = 2 * (NA_QR + NA_ROWS - 1)
NA_BLOCKS = 8
SSD_PREP_CHUNKS = 16
MOE_TILE = 512
MOE_STAGE = 512
MOE_STAGE_SLOTS = 6


def _cparams(sem, vmem=None):
    return pltpu.CompilerParams(dimension_semantics=sem, vmem_limit_bytes=vmem or VMEM_LIMIT)


def _silu(x):
    return x * jax.nn.sigmoid(x)


def _softplus(x):
    return jnp.maximum(x, 0.0) + jnp.log(1.0 + jnp.exp(-jnp.abs(x)))


def _ln_residual(h, y, gate, g, b, alpha):
    v = alpha * h + gate * y
    mu = jnp.mean(v, axis=-1, keepdims=True)
    d = v - mu
    var = jnp.mean(d * d, axis=-1, keepdims=True)
    return d * lax.rsqrt(var + LN_EPS) * g + b


def _mod_row(ref, rid):
    return ref[0, pl.ds(rid, 1), :]


def _mods_body(c_ref, w_ref, b_ref, o_ref):
    s = _silu(c_ref[...])
    o_ref[0] = jnp.dot(s, w_ref[0], precision=HIGHEST, preferred_element_type=F32) + b_ref[0]


def _mods_call(cvec, ada_w, ada_b):
    depth, d, n = ada_w.shape
    tn = 1024
    return pl.pallas_call(
        _mods_body,
        grid=(depth, n // tn),
        in_specs=[pl.BlockSpec((8, d), lambda l, j: (0, 0)),
                  pl.BlockSpec((1, d, tn), lambda l, j: (l, 0, j)),
                  pl.BlockSpec((1, 1, tn), lambda l, j: (l, 0, j))],
        out_specs=pl.BlockSpec((1, 8, tn), lambda l, j: (l, 0, j)),
        out_shape=jax.ShapeDtypeStruct((depth, 8, n), F32),
        compiler_params=_cparams(("parallel", "parallel")),
        name="mods",
    )(cvec, ada_w, ada_b.reshape(depth, 1, n))


def _inproj_body(*refs, row_base, tiles_per_row, tn, even):
    if even:
        h_ref, sh_ref, sc_ref, w_ref, wk_ref, wvt_ref, wdt_ref, o_ref, ok_ref, ovt_ref, odt_ref = refs
    else:
        h_ref, sh_ref, sc_ref, w_ref, o_ref = refs
    rid = row_base + pl.program_id(0) // tiles_per_row
    u = (h_ref[...] * (1.0 + _mod_row(sc_ref, rid)) + _mod_row(sh_ref, rid)).astype(BF16)
    for n0 in range(0, w_ref.shape[1], tn):
        o_ref[:, n0:n0 + tn] = jnp.dot(u, w_ref[:, n0:n0 + tn], preferred_element_type=F32).astype(o_ref.dtype)
    if even:
        k = jnp.dot(u, wk_ref[...], preferred_element_type=F32)
        for p in range(ok_ref.shape[0]):
            ok_ref[p] = k[:, p * LANES:(p + 1) * LANES].astype(ok_ref.dtype)
        ovt_ref[...] = lax.dot_general(wvt_ref[...], u, (((1,), (1,)), ((), ())),
                                       preferred_element_type=F32).astype(ovt_ref.dtype)
        odt_ref[...] = jnp.dot(u, wdt_ref[...], preferred_element_type=F32)


def _inproj_call(h, mods, layer, w, w_k, w_vt, w_dt, seq, *, tm, row_base, tiles_per_row, name):
    t, d = h.shape
    n = w.shape[1]
    even = w_vt is not None
    body = functools.partial(_inproj_body, row_base=row_base, tiles_per_row=tiles_per_row, tn=TN_PROJ, even=even)
    resident = lambda a: pl.BlockSpec(a.shape, lambda i: (0, 0), pipeline_mode=pl.Buffered(1))
    in_specs = [pl.BlockSpec((tm, d), lambda i: (i, 0)),
                pl.BlockSpec((1, 8, d), lambda i: (layer, 0, 0)),
                pl.BlockSpec((1, 8, d), lambda i: (layer, 0, 1)),
                resident(w)]
    out_specs = [pl.BlockSpec((tm, n), lambda i: (i, 0))]
    out_shape = [jax.ShapeDtypeStruct((t, n), BF16)]
    args = [h, mods, mods, w]
    if even:
        dv = w_vt.shape[0]
        nkp = w_k.shape[1] // LANES
        tps = seq // tm
        in_specs += [resident(w_k), resident(w_vt), resident(w_dt)]
        out_specs += [pl.BlockSpec((nkp, tm, LANES), lambda i: (0, i, 0)),
                      pl.BlockSpec((dv, tm), lambda i: (i // tps, i % tps)),
                      pl.BlockSpec((tm, w_dt.shape[1]), lambda i: (i, 0))]
        out_shape += [jax.ShapeDtypeStruct((nkp, t, LANES), BF16),
                      jax.ShapeDtypeStruct((t // seq * dv, seq), BF16),
                      jax.ShapeDtypeStruct((t, w_dt.shape[1]), F32)]
        args += [w_k, w_vt, w_dt]
    outs = pl.pallas_call(
        body,
        grid=(t // tm,),
        in_specs=in_specs,
        out_specs=out_specs,
        out_shape=out_shape,
        compiler_params=_cparams(("parallel",)),
        name=name,
    )(*args)
    return tuple(outs) if even else outs[0]


def _rpb_onehot():
    qc = np.arange(GRID_W)[:, None]
    kc = np.arange(GRID_W)[None, :]
    c0 = np.clip(qc - NA_COLS // 2, 0, GRID_W - NA_COLS)
    inside = (kc >= c0) & (kc < c0 + NA_COLS)
    dc = kc - qc + NA_COLS - 1
    oh = np.zeros((LANES, GRID_W, GRID_W), np.float32)
    for d in range(2 * NA_COLS - 1):
        oh[d] = ((dc == d) & inside).astype(np.float32)
    mask = np.where(inside, 0.0, NEG).astype(np.float32)
    return oh.reshape(LANES, GRID_W * GRID_W), mask.reshape(1, GRID_W * GRID_W)


def _rpb_body(r_ref, oh_ref, m_ref, o_ref):
    o_ref[...] = jnp.dot(r_ref[...], oh_ref[...], precision=HIGHEST,
                         preferred_element_type=F32) + m_ref[...]


def _rpb_table(rpb):
    h = rpb.shape[0]
    nd = 2 * NA_ROWS - 1
    oh, mask = _rpb_onehot()
    r2 = jnp.pad(rpb.reshape(h * nd, 2 * NA_COLS - 1), ((0, 0), (0, LANES - 2 * NA_COLS + 1)))
    e = pl.pallas_call(
        _rpb_body,
        out_shape=jax.ShapeDtypeStruct((h * nd, GRID_W * GRID_W), F32),
        name="rpb_table",
    )(r2, jnp.asarray(oh), jnp.asarray(mask))
    e = jnp.swapaxes(e.reshape(h, nd, GRID_W, GRID_W), 2, 3)
    e = jnp.pad(e, ((0, 0), (1 - NA_DMIN, NA_DMIN + NA_ND - nd), (0, 0), (0, 0)),
                constant_values=NEG)
    first, second = e[:, 1:], e[:, :-1]
    neg = jnp.full_like(first, NEG)
    return jnp.concatenate([jnp.concatenate([first, second], axis=-1),
                            jnp.concatenate([first, neg], axis=-1),
                            jnp.concatenate([neg, second], axis=-1),
                            jnp.concatenate([neg, neg], axis=-1)], axis=1)


def _natten_window(r_first, rows):
    return jnp.clip(r_first - NA_ROWS // 2, 0, rows - NA_WIN)


def _natten_scores(q, r_first, k_ref, kc_ref, e_ref, s_scr, rows):
    nk = NA_WIN * GRID_W
    ws = _natten_window(r_first, rows)
    kwin = k_ref[0, pl.ds(pl.multiple_of(ws * GRID_W, LANES), nk), :]
    kc = kc_ref[0]
    idx = []
    for g in range(NA_QR // 2):
        r = r_first + 2 * g
        r0a = jnp.clip(r - NA_ROWS // 2, 0, rows - NA_ROWS)
        r0b = jnp.clip(r + 1 - NA_ROWS // 2, 0, rows - NA_ROWS)
        col = []
        for t in range(NA_WIN):
            kr = ws + t
            d = kr - r + (NA_ROWS - 1)
            out_first = jnp.logical_or(kr < r0a, kr >= r0a + NA_ROWS).astype(jnp.int32)
            out_second = jnp.logical_or(kr < r0b, kr >= r0b + NA_ROWS).astype(jnp.int32)
            col.append((2 * out_first + out_second) * NA_ND + d - NA_DMIN)
        idx.append(col)
    lo = lax.broadcasted_iota(jnp.int32, (1, LANES), 1) < NA_HEAD_DIM
    scale = jnp.asarray(NA_HEAD_DIM ** -0.5, BF16)
    nt = (((1,), (1,)), ((), ()))
    for a in range(2):
        sel = lo if a == 0 else jnp.logical_not(lo)
        qa = jnp.where(sel, q, jnp.zeros_like(q)) * scale
        bias = jnp.concatenate(
            [jnp.concatenate([e_ref[a, idx[g][t]] for g in range(NA_QR // 2)], axis=1) for t in range(NA_WIN)],
            axis=0)
        s_loc = lax.dot_general(kwin, qa, nt, preferred_element_type=F32) + bias
        s_ctx = lax.dot_general(kc, qa, nt, preferred_element_type=F32)
        nkeys = nk + s_ctx.shape[0]
        s_scr[a, :nk, :] = s_loc
        s_scr[a, nk:nkeys, :] = s_ctx
        m = jnp.maximum(jnp.max(s_loc, axis=0, keepdims=True), jnp.max(s_ctx, axis=0, keepdims=True))
        s_scr[a, nkeys:, :] = jnp.broadcast_to(m, (s_scr.shape[1] - nkeys, m.shape[1]))


def _natten_softmax_pv(r_first, s_scr, p_scr, vt_ref, vct_ref, rows):
    nk = NA_WIN * GRID_W
    vct = vct_ref[...]
    nkeys = nk + vct.shape[1]
    nkb = nkeys // GRID_W
    ws = _natten_window(r_first, rows)
    vt_win = vt_ref[:, pl.ds(pl.multiple_of(ws * GRID_W, LANES), nk)]
    outs = []
    for a in range(2):
        linv = []
        for g in range(NA_QR // 2):
            cols = slice(g * LANES, (g + 1) * LANES)
            m = s_scr[a, nkeys:nkeys + 1, cols]
            ls = None
            for t in range(nkb):
                p = jnp.exp(s_scr[a, t * GRID_W:(t + 1) * GRID_W, cols] - m)
                p_scr[a, t * GRID_W:(t + 1) * GRID_W, cols] = p.astype(BF16)
                ls = p if ls is None else ls + p
            linv.append(1.0 / jnp.sum(ls, axis=0, keepdims=True))
        hd = slice(a * NA_HEAD_DIM, (a + 1) * NA_HEAD_DIM)
        o_t = (jnp.dot(vt_win[hd, :], p_scr[a, :nk, :], preferred_element_type=F32)
               + jnp.dot(vct[hd, :], p_scr[a, nk:nkeys, :], preferred_element_type=F32))
        outs.append(o_t * jnp.concatenate(linv, axis=-1))
    return jnp.concatenate(outs, axis=0).T


def _natten_body(q0_ref, q_ref, qn_ref, k_ref, vt_ref, kc_ref, vct_ref, e_ref, o_ref,
                 sa_scr, sb_scr, pa_scr, pb_scr, *, rows, nrb):
    nq = NA_QR * GRID_W
    i = pl.program_id(2)
    b0 = NA_BLOCKS * i
    bufs = ((sa_scr, pa_scr), (sb_scr, pb_scr))

    @pl.when(i == 0)
    def _():
        _natten_scores(q0_ref[...], 0, k_ref, kc_ref, e_ref, sa_scr, rows)

    for k in range(NA_BLOCKS):
        s_cur, p_cur = bufs[k % 2]
        s_nxt = bufs[(k + 1) % 2][0]
        if k + 1 < NA_BLOCKS:
            q_nxt, b_nxt = q_ref[(k + 1) * nq:(k + 2) * nq, :], b0 + k + 1
        else:
            q_nxt, b_nxt = qn_ref[...], jnp.minimum(b0 + NA_BLOCKS, nrb - 1)
        _natten_scores(q_nxt, b_nxt * NA_QR, k_ref, kc_ref, e_ref, s_nxt, rows)
        o_ref[k * nq:(k + 1) * nq, :] = _natten_softmax_pv(
            (b0 + k) * NA_QR, s_cur, p_cur, vt_ref, vct_ref, rows).astype(o_ref.dtype)


def _natten_call(proj_l, k_l, vt_l, k_c, vt_c, e2, batch, seq, lc):
    d = e2.shape[0] * NA_HEAD_DIM
    npair = d // LANES
    rows = seq // GRID_W
    nrb = rows // NA_QR
    nq = NA_QR * GRID_W
    nkeys = NA_WIN * GRID_W + lc
    body = functools.partial(_natten_body, rows=rows, nrb=nrb)
    nst = nrb // NA_BLOCKS
    return pl.pallas_call(
        body,
        grid=(batch, npair, nst),
        in_specs=[pl.BlockSpec((nq, LANES), lambda b, p, i: (b * nrb, p)),
                  pl.BlockSpec((NA_BLOCKS * nq, LANES), lambda b, p, i: (b * nst + i, p)),
                  pl.BlockSpec((nq, LANES),
                               lambda b, p, i: (b * nrb + jnp.minimum(NA_BLOCKS * (i + 1), nrb - 1), p)),
                  pl.BlockSpec((1, seq, LANES), lambda b, p, i: (p, b, 0)),
                  pl.BlockSpec((LANES, seq), lambda b, p, i: (b * npair + p, 0)),
                  pl.BlockSpec((1, lc, LANES), lambda b, p, i: (p, b, 0)),
                  pl.BlockSpec((LANES, lc), lambda b, p, i: (b * npair + p, 0)),
                  pl.BlockSpec((2, e2.shape[1], GRID_W, LANES), lambda b, p, i: (p, 0, 0, 0))],
        out_specs=pl.BlockSpec((NA_BLOCKS * nq, LANES), lambda b, p, i: (b * nst + i, p)),
        out_shape=jax.ShapeDtypeStruct((batch * seq, d), BF16),
        scratch_shapes=[pltpu.VMEM((2, nkeys + 8, nq), F32), pltpu.VMEM((2, nkeys + 8, nq), F32),
                        pltpu.VMEM((2, nkeys, nq), BF16), pltpu.VMEM((2, nkeys, nq), BF16)],
        compiler_params=_cparams(("parallel", "parallel", "arbitrary")),
        name="natten",
    )(proj_l, proj_l, proj_l, k_l, vt_l, k_c, vt_c, e2)


def _ctxattn_body(q_ref, k_ref, vt_ref, o_ref):
    q = q_ref[...]
    k = k_ref[0]
    vt = vt_ref[...]
    lane = lax.broadcasted_iota(jnp.int32, (1, LANES), 1)
    lo = lane < NA_HEAD_DIM
    scale = NA_HEAD_DIM ** -0.5
    nt = (((1,), (1,)), ((), ()))
    outs = []
    for a in range(2):
        sel = lo if a == 0 else jnp.logical_not(lo)
        qa = jnp.where(sel, q, jnp.zeros_like(q)) * jnp.asarray(scale, BF16)
        s = lax.dot_general(qa, k, nt, preferred_element_type=F32)
        m = jnp.max(s, axis=-1, keepdims=True)
        p = jnp.exp(s - m)
        l = jnp.sum(p, axis=-1, keepdims=True)
        outs.append(lax.dot_general(p.astype(BF16), vt, nt, preferred_element_type=F32) / l)
    o_ref[...] = jnp.where(lo, outs[0], outs[1]).astype(o_ref.dtype)


def _ctxattn_call(proj_c, k_c, vt_c, batch, lc, d):
    npair = d // LANES
    return pl.pallas_call(
        _ctxattn_body,
        grid=(batch, npair),
        in_specs=[pl.BlockSpec((lc, LANES), lambda b, p: (b, p)),
                  pl.BlockSpec((1, lc, LANES), lambda b, p: (p, b, 0)),
                  pl.BlockSpec((LANES, lc), lambda b, p: (b * npair + p, 0))],
        out_specs=pl.BlockSpec((lc, LANES), lambda b, p: (b, p)),
        out_shape=jax.ShapeDtypeStruct((batch * lc, d), BF16),
        compiler_params=_cparams(("parallel", "parallel")),
        name="ctx_attn",
    )(proj_c, k_c, vt_c)


def _dwconv_silu_body(xp_ref, x_ref, xn_ref, w_ref, b_ref, o_ref, *, tiles_per_seq):
    i = pl.program_id(0)
    k = w_ref.shape[0]
    half = k // 2
    tb = x_ref.shape[0]
    hr = xp_ref.shape[0]
    first = (i % tiles_per_seq) == 0
    last = (i % tiles_per_seq) == tiles_per_seq - 1
    prev = jnp.where(first, 0.0, xp_ref[...].astype(F32))
    nxt = jnp.where(last, 0.0, xn_ref[...].astype(F32))
    ext = jnp.concatenate([prev, x_ref[...].astype(F32), nxt], axis=0)
    acc = jnp.zeros((tb, x_ref.shape[1]), F32) + b_ref[...]
    for t in range(k):
        off = hr - half + t
        acc = acc + ext[off:off + tb, :] * w_ref[pl.ds(t, 1), :]
    o_ref[...] = _silu(acc).astype(o_ref.dtype)


def _dwconv_silu_call(proj, col0, width, w, b, seq, *, tb, tc):
    t = proj.shape[0]
    hr = BF16_SUBLANES
    nrt = t // tb
    cb0 = col0 // tc
    body = functools.partial(_dwconv_silu_body, tiles_per_seq=seq // tb)
    return pl.pallas_call(
        body,
        grid=(nrt, width // tc),
        in_specs=[pl.BlockSpec((hr, tc), lambda i, j: (jnp.maximum(i * (tb // hr) - 1, 0), cb0 + j)),
                  pl.BlockSpec((tb, tc), lambda i, j: (i, cb0 + j)),
                  pl.BlockSpec((hr, tc), lambda i, j: (jnp.minimum((i + 1) * (tb // hr), t // hr - 1), cb0 + j)),
                  pl.BlockSpec((w.shape[0], tc), lambda i, j: (0, j)),
                  pl.BlockSpec((1, tc), lambda i, j: (0, j))],
        out_specs=pl.BlockSpec((tb, tc), lambda i, j: (i, j)),
        out_shape=jax.ShapeDtypeStruct((t, width), BF16),
        compiler_params=_cparams(("parallel", "parallel")),
        name="dwconv_silu",
    )(proj, proj, proj, w, b.reshape(1, width))


def _ssd_prep_body(dt_ref, dtb_ref, alog_ref, dtt_ref, cum_ref, cumt_ref, *, chunks):
    q = SSM_CHUNK
    fwd = pl.program_id(1) == 0
    ii = lax.broadcasted_iota(jnp.int32, (q, q), 0)
    jj = lax.broadcasted_iota(jnp.int32, (q, q), 1)
    tri = ((jj - ii) * jnp.where(fwd, 1, -1) <= 0).astype(F32)
    a = -jnp.exp(alog_ref[0])
    for c in range(chunks):
        rws = slice(c * q, (c + 1) * q)
        dtv = _softplus(dt_ref[rws, :] + dtb_ref[0])
        cum = jnp.dot(tri, dtv * a, precision=HIGHEST, preferred_element_type=F32)
        dtt_ref[0, rws, :] = dtv.T
        cum_ref[0, rws, :] = cum
        cumt_ref[0, rws, :] = cum.T


def _ssd_prep_call(dt, dtb, alog, batch, seq):
    nc = seq // SSM_CHUNK
    chunks = min(SSD_PREP_CHUNKS, nc)
    rows = chunks * SSM_CHUNK
    t = batch * seq
    out = jax.ShapeDtypeStruct((2, t, LANES), F32)
    ospec = pl.BlockSpec((1, rows, LANES), lambda i, d: (d, i, 0))
    return pl.pallas_call(
        functools.partial(_ssd_prep_body, chunks=chunks),
        grid=(t // rows, 2),
        in_specs=[pl.BlockSpec((rows, LANES), lambda i, d: (i, d)),
                  pl.BlockSpec((1, 1, LANES), lambda i, d: (d, 0, 0)),
                  pl.BlockSpec((1, 1, LANES), lambda i, d: (d, 0, 0))],
        out_specs=[ospec, ospec, ospec],
        out_shape=[out, out, out],
        compiler_params=_cparams(("parallel", "parallel")),
        name="ssd_prep",
    )(dt, dtb, alog)


def _ssd_body(x_ref, b_ref, c_ref, dtt_ref, cum_ref, cumt_ref, h0_ref, y_ref, hT_ref, st_scr, *, nsteps):
    q = SSM_CHUNK
    dr = pl.program_id(1)
    s = pl.program_id(2)

    @pl.when(s == 0)
    def _():
        st_scr[...] = h0_ref[0, 0]

    fwd = dr == 0
    cum = cum_ref[0]
    cum_t = cumt_ref[0]
    dt_t = dtt_ref[0]
    ii = lax.broadcasted_iota(jnp.int32, (q, q), 0)
    jj = lax.broadcasted_iota(jnp.int32, (q, q), 1)
    tri = (jj - ii) * jnp.where(fwd, 1, -1) <= 0
    tot = jnp.where(fwd, cum[q - 1:q, :], cum[0:1, :])
    tot_c = jnp.where(fwd, cum_t[:, q - 1:q], cum_t[:, 0:1])
    upd_t = dt_t * jnp.exp(tot_c - cum_t)
    lane = lax.broadcasted_iota(jnp.int32, (1, LANES), 1)
    lo = lane < SSM_HEAD_DIM
    heads_per_group = SSM_HEADS // SSM_GROUPS
    for g in range(SSM_GROUPS):
        bg = b_ref[:, g * SSM_STATE:(g + 1) * SSM_STATE]
        cg = c_ref[:, g * SSM_STATE:(g + 1) * SSM_STATE]
        cb = lax.dot_general(cg, bg, (((1,), (1,)), ((), ())), preferred_element_type=F32)
        bg_t = bg.astype(F32).T
        for hp in range(g * heads_per_group // 2, (g + 1) * heads_per_group // 2):
            ha, hb = 2 * hp, 2 * hp + 1
            xb = x_ref[:, hp * LANES:(hp + 1) * LANES]
            cca = jnp.broadcast_to(cum[:, ha:ha + 1], (q, q))
            ccb = jnp.broadcast_to(cum[:, hb:hb + 1], (q, q))
            l_a = jnp.exp(jnp.where(tri, cca - cum_t[ha:ha + 1, :], NEG))
            l_b = jnp.exp(jnp.where(tri, ccb - cum_t[hb:hb + 1, :], NEG))
            m_a = (cb * l_a * dt_t[ha:ha + 1, :]).astype(BF16)
            m_b = (cb * l_b * dt_t[hb:hb + 1, :]).astype(BF16)
            y_intra = jnp.where(lo, jnp.dot(m_a, xb, preferred_element_type=F32),
                                jnp.dot(m_b, xb, preferred_element_type=F32))
            ccp = jnp.where(lo, cca, ccb)
            st = st_scr[hp]
            y_inter = jnp.dot(cg, st.astype(BF16), preferred_element_type=F32) * jnp.exp(ccp)
            y_ref[0, :, hp * LANES:(hp + 1) * LANES] = y_intra + y_inter
            totp = jnp.where(lo, jnp.broadcast_to(tot[:, ha:ha + 1], (1, LANES)),
                             jnp.broadcast_to(tot[:, hb:hb + 1], (1, LANES)))
            bw_a = (bg_t * upd_t[ha:ha + 1, :]).astype(BF16)
            bw_b = (bg_t * upd_t[hb:hb + 1, :]).astype(BF16)
            st_scr[hp] = st * jnp.exp(totp) + jnp.where(lo, jnp.dot(bw_a, xb, preferred_element_type=F32),
                                                        jnp.dot(bw_b, xb, preferred_element_type=F32))

    @pl.when(s == nsteps - 1)
    def _():
        hT_ref[0, 0] = st_scr[...]


def _ssd_call(xbc, dt, dtb, alog, h0, batch, seq):
    inner = SSM_HEADS * SSM_HEAD_DIM
    gn = SSM_GROUPS * SSM_STATE
    nc = seq // SSM_CHUNK
    npair = SSM_HEADS // 2

    def rblk(b, d, s):
        return b * nc + jnp.where(d == 0, s, nc - 1 - s)

    dt_t, cum, cum_t = _ssd_prep_call(dt, dtb, alog, batch, seq)
    pspec = pl.BlockSpec((1, SSM_CHUNK, LANES), lambda b, d, s: (d, rblk(b, d, s), 0))
    body = functools.partial(_ssd_body, nsteps=nc)
    return pl.pallas_call(
        body,
        grid=(batch, 2, nc),
        in_specs=[pl.BlockSpec((SSM_CHUNK, inner), lambda b, d, s: (rblk(b, d, s), 0)),
                  pl.BlockSpec((SSM_CHUNK, gn), lambda b, d, s: (rblk(b, d, s), inner // gn)),
                  pl.BlockSpec((SSM_CHUNK, gn), lambda b, d, s: (rblk(b, d, s), inner // gn + 1)),
                  pspec, pspec, pspec,
                  pl.BlockSpec((1, 1, npair, SSM_STATE, LANES), lambda b, d, s: (b, d, 0, 0, 0))],
        out_specs=[pl.BlockSpec((1, SSM_CHUNK, inner), lambda b, d, s: (d, rblk(b, d, s), 0)),
                   pl.BlockSpec((1, 1, npair, SSM_STATE, LANES), lambda b, d, s: (b, d, 0, 0, 0))],
        out_shape=[jax.ShapeDtypeStruct((2, batch * seq, inner), F32),
                   jax.ShapeDtypeStruct((batch, 2, npair, SSM_STATE, LANES), F32)],
        scratch_shapes=[pltpu.VMEM((npair, SSM_STATE, LANES), F32)],
        compiler_params=_cparams(("parallel", "parallel", "arbitrary")),
        name="ssd",
    )(xbc, xbc, xbc, dt_t, cum, cum_t, h0)


def _evenout_body(attn_ref, y_ref, xs_ref, z_ref, h_ref, gate_ref, dsk_ref, nw_ref, w_ref, lg_ref, lb_ref,
                  o_ref, ssm_scr, *, row_base, tiles_per_row, alpha):
    i = pl.program_id(0)
    rid = row_base + i // tiles_per_row
    inner = xs_ref.shape[1]
    gw = inner // SSM_GROUPS
    z = z_ref[...].astype(F32)
    y = y_ref[0] + y_ref[1] + xs_ref[...].astype(F32) * dsk_ref[...]
    u = y * _silu(z)
    for g in range(SSM_GROUPS):
        ug = u[:, g * gw:(g + 1) * gw]
        ms = jnp.mean(ug * ug, axis=-1, keepdims=True)
        ssm_scr[:, g * gw:(g + 1) * gw] = (ug * lax.rsqrt(ms + RMS_EPS) * nw_ref[:, g * gw:(g + 1) * gw]).astype(BF16)
    d_attn = attn_ref.shape[1]
    acc = jnp.dot(attn_ref[...], w_ref[:d_attn, :], preferred_element_type=F32)
    acc = acc + jnp.dot(ssm_scr[...], w_ref[d_attn:, :], preferred_element_type=F32)
    o_ref[...] = _ln_residual(h_ref[...], acc, _mod_row(gate_ref, rid), lg_ref[0], lb_ref[0], alpha)


def _evenout_call(attn, y2, xbc, proj, z_cb, h, mods, layer, dskip_row, norm_w, w_out, ln_g, ln_b, ln_idx,
                  *, tm, row_base, tiles_per_row, alpha):
    t, d = h.shape
    inner = y2.shape[2]
    body = functools.partial(_evenout_body, row_base=row_base, tiles_per_row=tiles_per_row, alpha=alpha)
    return pl.pallas_call(
        body,
        grid=(t // tm,),
        in_specs=[pl.BlockSpec((tm, d), lambda i: (i, 0)),
                  pl.BlockSpec((2, tm, inner), lambda i: (0, i, 0)),
                  pl.BlockSpec((tm, inner), lambda i: (i, 0)),
                  pl.BlockSpec((tm, inner), lambda i: (i, z_cb)),
                  pl.BlockSpec((tm, d), lambda i: (i, 0)),
                  pl.BlockSpec((1, 8, d), lambda i: (layer, 0, 2)),
                  pl.BlockSpec((1, inner), lambda i: (0, 0)),
                  pl.BlockSpec((1, inner), lambda i: (0, 0)),
                  pl.BlockSpec(w_out.shape, lambda i: (0, 0)),
                  pl.BlockSpec((1, 1, d), lambda i: (ln_idx, 0, 0)),
                  pl.BlockSpec((1, 1, d), lambda i: (ln_idx, 0, 0))],
        out_specs=pl.BlockSpec((tm, d), lambda i: (i, 0)),
        out_shape=jax.ShapeDtypeStruct((t, d), F32),
        scratch_shapes=[pltpu.VMEM((tm, inner), BF16)],
        compiler_params=_cparams(("parallel",)),
        name="even_out",
    )(attn, y2, xbc, proj, h, mods, dskip_row, norm_w, w_out, ln_g, ln_b)


def _hidden_chunks(f):
    half = (f // 2 + MXU_DIM - 1) // MXU_DIM * MXU_DIM
    return ((0, half), (half, f - half))


def _swiglu(u, w1_ref, w3_ref, w2_ref):
    acc = None
    for f0, fc in _hidden_chunks(w1_ref.shape[1]):
        a = jnp.dot(u, w1_ref[:, f0:f0 + fc], preferred_element_type=F32)
        b = jnp.dot(u, w3_ref[:, f0:f0 + fc], preferred_element_type=F32)
        t = (_silu(a) * b).astype(BF16)
        y = jnp.dot(t, w2_ref[f0:f0 + fc, :], preferred_element_type=F32)
        acc = y if acc is None else acc + y
    return acc


def _ffn_body(h_ref, sh_ref, sc_ref, gate_ref, w1_ref, w3_ref, w2_ref, lg_ref, lb_ref, o_ref,
              *, row_base, tiles_per_row, alpha):
    i = pl.program_id(0)
    rid = row_base + i // tiles_per_row
    h = h_ref[...]
    u = (h * (1.0 + _mod_row(sc_ref, rid)) + _mod_row(sh_ref, rid)).astype(BF16)
    y = _swiglu(u, w1_ref, w3_ref, w2_ref)
    o_ref[...] = _ln_residual(h, y, _mod_row(gate_ref, rid), lg_ref[0], lb_ref[0], alpha)


def _ffn_call(h, mods, layer, w1c, w3c, w2c, ln_g, ln_b, ln_idx, *, tm, row_base, tiles_per_row, alpha):
    t, d = h.shape
    body = functools.partial(_ffn_body, row_base=row_base, tiles_per_row=tiles_per_row, alpha=alpha)
    wspec = lambda w: pl.BlockSpec(w.shape, lambda i: (0, 0), pipeline_mode=pl.Buffered(1))
    return pl.pallas_call(
        body,
        grid=(t // tm,),
        in_specs=[pl.BlockSpec((tm, d), lambda i: (i, 0)),
                  pl.BlockSpec((1, 8, d), lambda i: (layer, 0, 3)),
                  pl.BlockSpec((1, 8, d), lambda i: (layer, 0, 4)),
                  pl.BlockSpec((1, 8, d), lambda i: (layer, 0, 5)),
                  wspec(w1c), wspec(w3c), wspec(w2c),
                  pl.BlockSpec((1, 1, d), lambda i: (ln_idx, 0, 0)),
                  pl.BlockSpec((1, 1, d), lambda i: (ln_idx, 0, 0))],
        out_specs=pl.BlockSpec((tm, d), lambda i: (i, 0)),
        out_shape=jax.ShapeDtypeStruct((t, d), F32),
        compiler_params=_cparams(("parallel",)),
        name="ffn",
    )(h, mods, mods, mods, w1c, w3c, w2c, ln_g, ln_b)


def _scout_body(pb_ref, pcp_ref, pc_ref, pcn_ref, php_ref, ph_ref, phn_ref, h_ref, gate_ref, cw_ref, w_ref,
                lg_ref, lb_ref, o_ref, *, row_base, tiles_per_row, tiles_per_seq, alpha):
    i = pl.program_id(0)
    rid = row_base + i // tiles_per_row
    tm = ph_ref.shape[0]
    hr = php_ref.shape[0]
    first = (i % tiles_per_seq) == 0
    last = (i % tiles_per_seq) == tiles_per_seq - 1
    prev = jnp.where(first, 0.0, pcp_ref[...].astype(F32) * php_ref[...].astype(F32))
    cur = pc_ref[...].astype(F32) * ph_ref[...].astype(F32)
    nxt = jnp.where(last, 0.0, pcn_ref[...].astype(F32) * phn_ref[...].astype(F32))
    ext = jnp.concatenate([prev, cur, nxt], axis=0)
    acc = jnp.zeros_like(cur)
    for t in range(SC_CONV):
        off = hr - SC_CONV // 2 + t
        acc = acc + ext[off:off + tm, :] * cw_ref[pl.ds(t, 1), :]
    a = (pb_ref[...].astype(F32) * acc).astype(BF16)
    y = jnp.dot(a, w_ref[...], preferred_element_type=F32)
    o_ref[...] = _ln_residual(h_ref[...], y, _mod_row(gate_ref, rid), lg_ref[0], lb_ref[0], alpha)


def _scout_call(proj, h, mods, layer, conv_w, w_out, ln_g, ln_b, ln_idx, seq,
                *, tm, row_base, tiles_per_row, alpha):
    t, d = h.shape
    hr = BF16_SUBLANES
    r = tm // hr
    nh = t // hr
    body = functools.partial(_scout_body, row_base=row_base, tiles_per_row=tiles_per_row,
                             tiles_per_seq=seq // tm, alpha=alpha)
    prev_map = lambda cb: (lambda i: (jnp.maximum(i * r - 1, 0), cb))
    next_map = lambda cb: (lambda i: (jnp.minimum((i + 1) * r, nh - 1), cb))
    return pl.pallas_call(
        body,
        grid=(t // tm,),
        in_specs=[pl.BlockSpec((tm, d), lambda i: (i, 0)),
                  pl.BlockSpec((hr, d), prev_map(1)),
                  pl.BlockSpec((tm, d), lambda i: (i, 1)),
                  pl.BlockSpec((hr, d), next_map(1)),
                  pl.BlockSpec((hr, d), prev_map(2)),
                  pl.BlockSpec((tm, d), lambda i: (i, 2)),
                  pl.BlockSpec((hr, d), next_map(2)),
                  pl.BlockSpec((tm, d), lambda i: (i, 0)),
                  pl.BlockSpec((1, 8, d), lambda i: (layer, 0, 2)),
                  pl.BlockSpec(conv_w.shape, lambda i: (0, 0)),
                  pl.BlockSpec(w_out.shape, lambda i: (0, 0)),
                  pl.BlockSpec((1, 1, d), lambda i: (ln_idx, 0, 0)),
                  pl.BlockSpec((1, 1, d), lambda i: (ln_idx, 0, 0))],
        out_specs=pl.BlockSpec((tm, d), lambda i: (i, 0)),
        out_shape=jax.ShapeDtypeStruct((t, d), F32),
        compiler_params=_cparams(("parallel",)),
        name="shortconv_out",
    )(proj, proj, proj, proj, proj, proj, proj, h, mods, conv_w, w_out, ln_g, ln_b)


def _router_body(h_ref, sh_ref, sc_ref, wr_ref, u_ref, meta_ref, cnt_ref, cnt_scr, *, row_base, tiles_per_row):
    i = pl.program_id(0)
    rid = row_base + i // tiles_per_row
    tm = h_ref.shape[0]

    @pl.when(i == 0)
    def _():
        cnt_scr[...] = jnp.zeros_like(cnt_scr)

    u = h_ref[...] * (1.0 + _mod_row(sc_ref, rid)) + _mod_row(sh_ref, rid)
    u_ref[...] = u
    logits = jnp.dot(u, wr_ref[...], precision=HIGHEST, preferred_element_type=F32)
    lane = lax.broadcasted_iota(jnp.int32, (tm, LANES), 1).astype(F32)
    lg = jnp.where(lane < N_EXPERTS, logits, NEG)
    v1 = jnp.max(lg, axis=-1, keepdims=True)
    e1 = jnp.min(jnp.where(lg == v1, lane, float(LANES)), axis=-1, keepdims=True)
    lg2 = jnp.where(lane == e1, 2 * NEG, lg)
    v2 = jnp.max(lg2, axis=-1, keepdims=True)
    e2 = jnp.min(jnp.where(lg2 == v2, lane, float(LANES)), axis=-1, keepdims=True)
    g2 = 1.0 / (1.0 + jnp.exp(v1 - v2))
    g1 = 1.0 - g2
    oh1 = (lane == e1).astype(F32)
    oh2 = (lane == e2).astype(F32)
    both = (oh1 + oh2).astype(BF16)
    ii = lax.broadcasted_iota(jnp.int32, (tm, tm), 0)
    jj = lax.broadcasted_iota(jnp.int32, (tm, tm), 1)
    strict = (jj < ii).astype(BF16)
    before = jnp.dot(strict, both, preferred_element_type=F32) + cnt_scr[...]
    p1 = jnp.sum(before * oh1, axis=-1, keepdims=True)
    p2 = jnp.sum(before * oh2, axis=-1, keepdims=True)
    cnt_scr[...] = cnt_scr[...] + jnp.sum(oh1 + oh2, axis=0, keepdims=True)
    meta = jnp.where(lane == 0, e1,
           jnp.where(lane == 1, e2,
           jnp.where(lane == 2, p1,
           jnp.where(lane == 3, p2,
           jnp.where(lane == 4, g1,
           jnp.where(lane == 5, g2, 0.0))))))
    meta_ref[...] = meta
    cnt_ref[...] = jnp.broadcast_to(cnt_scr[...], cnt_ref.shape)


def _router_call(h, mods, layer, wr_pad, *, tm, row_base, tiles_per_row):
    t, d = h.shape
    body = functools.partial(_router_body, row_base=row_base, tiles_per_row=tiles_per_row)
    return pl.pallas_call(
        body,
        grid=(t // tm,),
        in_specs=[pl.BlockSpec((tm, d), lambda i: (i, 0)),
                  pl.BlockSpec((1, 8, d), lambda i: (layer, 0, 3)),
                  pl.BlockSpec((1, 8, d), lambda i: (layer, 0, 4)),
                  pl.BlockSpec(wr_pad.shape, lambda i: (0, 0))],
        out_specs=[pl.BlockSpec((tm, d), lambda i: (i, 0)),
                   pl.BlockSpec((tm, LANES), lambda i: (i, 0)),
                   pl.BlockSpec((8, LANES), lambda i: (0, 0))],
        out_shape=[jax.ShapeDtypeStruct((t, d), F32),
                   jax.ShapeDtypeStruct((t, LANES), F32),
                   jax.ShapeDtypeStruct((8, LANES), F32)],
        scratch_shapes=[pltpu.VMEM((1, LANES), F32)],
        compiler_params=_cparams(("arbitrary",)),
        name="moe_router",
    )(h, mods, mods, wr_pad)


def _dispatch_body(tv_ref, dest_ref, u_ref, rows_hbm, zbuf, sem, zsem):
    i = pl.program_id(0)
    tm = u_ref.shape[0]
    tile = zbuf.shape[0]

    @pl.when(i == 0)
    def _():
        zbuf[...] = jnp.zeros_like(zbuf)

        def fill_copy(t):
            return pltpu.make_async_copy(zbuf, rows_hbm.at[pl.ds(t * tile, tile)], zsem.at[0])

        def fill(t, c):
            @pl.when(tv_ref[t] < tile)
            def _():
                fill_copy(t).start()
            return c

        def fill_wait(t, c):
            @pl.when(tv_ref[t] < tile)
            def _():
                fill_copy(t).wait()
            return c

        lax.fori_loop(0, tv_ref.shape[0], fill, 0)
        lax.fori_loop(0, tv_ref.shape[0], fill_wait, 0)

    def row_copy(r, k):
        return pltpu.make_async_copy(u_ref.at[pl.ds(r, 1)], rows_hbm.at[pl.ds(dest_ref[0, k, r], 1)], sem.at[k])

    def issue(r, c):
        for k in range(TOP_K):
            row_copy(r, k).start()
        return c

    def drain(r, c):
        for k in range(TOP_K):
            row_copy(r, k).wait()
        return c

    lax.fori_loop(0, tm, issue, 0, unroll=8)
    lax.fori_loop(0, tm, drain, 0, unroll=8)


def _dispatch_call(u, dest_t, n_rows, tile_valid, *, tm):
    t, d = u.shape
    grid_spec = pltpu.PrefetchScalarGridSpec(
        num_scalar_prefetch=1,
        grid=(t // tm,),
        in_specs=[pl.BlockSpec((1, TOP_K, tm), lambda i, tv: (i, 0, 0), memory_space=pltpu.SMEM),
                  pl.BlockSpec((tm, d), lambda i, tv: (i, 0))],
        out_specs=pl.BlockSpec(memory_space=pl.ANY),
        scratch_shapes=[pltpu.VMEM((MOE_TILE, d), F32),
                        pltpu.SemaphoreType.DMA((TOP_K,)),
                        pltpu.SemaphoreType.DMA((1,))],
    )
    return pl.pallas_call(
        _dispatch_body,
        grid_spec=grid_spec,
        out_shape=jax.ShapeDtypeStruct((n_rows, d), F32),
        compiler_params=_cparams(("arbitrary",)),
        name="moe_dispatch",
    )(tile_valid, dest_t, u)


def _moe_body(te_ref, tv_ref, x_ref, w1_hbm, w3_hbm, w2_hbm, o_ref, w1s, w3s, w2s, stage, wsem, *, layer):
    t = pl.program_id(0)
    e = te_ref[t]
    e_prev = te_ref[jnp.maximum(t - 1, 0)]

    @pl.when(jnp.logical_or(t == 0, e != e_prev))
    def _():
        nslot, cs, _ = stage.shape
        copies = []
        for w_hbm, ws in ((w1_hbm, w1s), (w3_hbm, w3s), (w2_hbm, w2s)):
            for r0 in range(0, ws.shape[0], cs):
                for c0 in range(0, ws.shape[1], cs):
                    copies.append((w_hbm.at[layer, e, pl.ds(r0, cs), pl.ds(c0, cs)],
                                   ws.at[pl.ds(r0, cs), pl.ds(c0, cs)]))

        def dma(n):
            return pltpu.make_async_copy(copies[n][0], stage.at[n % nslot], wsem.at[n % nslot])

        for n in range(nslot - 1):
            dma(n).start()
        for n in range(len(copies)):
            if n + nslot - 1 < len(copies):
                dma(n + nslot - 1).start()
            dma(n).wait()
            copies[n][1][...] = stage[n % nslot].astype(BF16)

    @pl.when(tv_ref[t] > 0)
    def _():
        o_ref[...] = _swiglu(x_ref[...].astype(BF16), w1s, w3s, w2s)

    @pl.when(tv_ref[t] == 0)
    def _():
        o_ref[...] = jnp.zeros_like(o_ref)


def _moe_call(rows_in, tile_expert, tile_valid, w1, w3, w2, wlayer):
    n_rows, d = rows_in.shape
    tr = MOE_TILE
    w1c, w3c, w2c = (jax.ShapeDtypeStruct(w.shape[1:], BF16) for w in (w1, w3, w2))
    grid_spec = pltpu.PrefetchScalarGridSpec(
        num_scalar_prefetch=2,
        grid=(n_rows // tr,),
        in_specs=[pl.BlockSpec((tr, d), lambda i, te, tv: (i, 0)),
                  pl.BlockSpec(memory_space=pl.ANY),
                  pl.BlockSpec(memory_space=pl.ANY),
                  pl.BlockSpec(memory_space=pl.ANY)],
        out_specs=pl.BlockSpec((tr, d), lambda i, te, tv: (i, 0)),
        scratch_shapes=[pltpu.VMEM(w1c.shape[1:], BF16),
                        pltpu.VMEM(w3c.shape[1:], BF16),
                        pltpu.VMEM(w2c.shape[1:], BF16),
                        pltpu.VMEM((MOE_STAGE_SLOTS, MOE_STAGE, MOE_STAGE), F32),
                        pltpu.SemaphoreType.DMA((MOE_STAGE_SLOTS,))],
    )
    return pl.pallas_call(
        functools.partial(_moe_body, layer=wlayer),
        grid_spec=grid_spec,
        out_shape=jax.ShapeDtypeStruct((n_rows, d), F32),
        compiler_params=_cparams(("arbitrary",)),
        name="moe_experts",
    )(tile_expert, tile_valid, rows_in, w1, w3, w2)


def _combine_body(dest_ref, rows_hbm, meta_ref, h_ref, gate_ref, lg_ref, lb_ref, o_ref, rbuf, gsem,
                  *, row_base, tiles_per_row, alpha):
    i = pl.program_id(0)
    rid = row_base + i // tiles_per_row
    tm = h_ref.shape[0]

    def row_copy(r, k):
        return pltpu.make_async_copy(rows_hbm.at[pl.ds(dest_ref[0, k, r], 1)], rbuf.at[k, pl.ds(r, 1)], gsem.at[k])

    def issue(r, c):
        for k in range(TOP_K):
            row_copy(r, k).start()
        return c

    def drain(r, c):
        for k in range(TOP_K):
            row_copy(r, k).wait()
        return c

    lax.fori_loop(0, tm, issue, 0, unroll=8)
    lax.fori_loop(0, tm, drain, 0, unroll=8)
    meta = meta_ref[...]
    y = meta[:, 4:5] * rbuf[0] + meta[:, 5:6] * rbuf[1]
    o_ref[...] = _ln_residual(h_ref[...], y, _mod_row(gate_ref, rid), lg_ref[0], lb_ref[0], alpha)


def _combine_call(dest_t, rows_out, meta, h, mods, layer, ln_g, ln_b, ln_idx, *, tm, row_base, tiles_per_row, alpha):
    t, d = h.shape
    body = functools.partial(_combine_body, row_base=row_base, tiles_per_row=tiles_per_row, alpha=alpha)
    return pl.pallas_call(
        body,
        grid=(t // tm,),
        in_specs=[pl.BlockSpec((1, TOP_K, tm), lambda i: (i, 0, 0), memory_space=pltpu.SMEM),
                  pl.BlockSpec(memory_space=pl.ANY),
                  pl.BlockSpec((tm, LANES), lambda i: (i, 0)),
                  pl.BlockSpec((tm, d), lambda i: (i, 0)),
                  pl.BlockSpec((1, 8, d), lambda i: (layer, 0, 5)),
                  pl.BlockSpec((1, 1, d), lambda i: (ln_idx, 0, 0)),
                  pl.BlockSpec((1, 1, d), lambda i: (ln_idx, 0, 0))],
        out_specs=pl.BlockSpec((tm, d), lambda i: (i, 0)),
        out_shape=jax.ShapeDtypeStruct((t, d), F32),
        scratch_shapes=[pltpu.VMEM((TOP_K, tm, d), F32),
                        pltpu.SemaphoreType.DMA((TOP_K,))],
        compiler_params=_cparams(("arbitrary",)),
        name="moe_combine",
    )(dest_t, rows_out, meta, h, mods, ln_g, ln_b)


def _moe_layer(h, mods, layer, wr_pad, w1, w3, w2, wlayer, ln_g, ln_b, ln_idx,
               *, tm, row_base, tiles_per_row, alpha):
    t, d = h.shape
    u, meta, cnt = _router_call(h, mods, layer, wr_pad, tm=tm, row_base=row_base, tiles_per_row=tiles_per_row)
    n_assign = t * TOP_K
    n_tiles = n_assign // MOE_TILE + N_EXPERTS
    counts = cnt[0, :N_EXPERTS].astype(jnp.int32)
    padded = (counts + MOE_TILE - 1) // MOE_TILE * MOE_TILE
    pad_ends = jnp.cumsum(padded)
    pad_starts = pad_ends - padded
    top_e = meta[:, 0:TOP_K].astype(jnp.int32)
    rank = meta[:, 2:2 + TOP_K].astype(jnp.int32)
    dest = pad_starts[top_e] + rank
    dest_t = jnp.swapaxes(dest.reshape(t // tm, tm, TOP_K), 1, 2)
    tile_start = jnp.arange(n_tiles, dtype=jnp.int32) * MOE_TILE
    tile_expert = jnp.minimum(jnp.sum(tile_start[:, None] >= pad_ends[None, :], axis=1),
                              N_EXPERTS - 1).astype(jnp.int32)
    seg_end = jnp.where(tile_start < pad_ends[-1], (pad_starts + counts)[tile_expert], 0)
    tile_valid = jnp.clip(seg_end - tile_start, 0, MOE_TILE).astype(jnp.int32)
    rows_in = _dispatch_call(u, dest_t, n_tiles * MOE_TILE, tile_valid, tm=tm)
    rows_out = _moe_call(rows_in, tile_expert, tile_valid, w1, w3, w2, wlayer)
    return _combine_call(dest_t, rows_out, meta, h, mods, layer, ln_g, ln_b, ln_idx,
                         tm=tm, row_base=row_base, tiles_per_row=tiles_per_row, alpha=alpha)


def kernel(x, c, ctx, c_ctx, ada_w, ada_b, ln_g, ln_b, even_w_in, na_rpb, ssm_conv_w, ssm_conv_b, ssm_dt_bias,
           ssm_a_log, ssm_d, ssm_norm_w, even_w_out, ffn_w1, ffn_w3, ffn_w2, sc_w_in, sc_conv_w, sc_w_out,
           moe_router, moe_w1, moe_w3, moe_w2):
    batch, seq, d = x.shape
    lc = ctx.shape[1]
    depth = ada_w.shape[0]
    alpha = (2 * depth) ** 0.25
    inner = SSM_HEADS * SSM_HEAD_DIM
    gn = SSM_GROUPS * SSM_STATE
    n_main = 3 * d + inner + inner + 2 * gn
    assert batch + 1 <= 8

    cvec = jnp.zeros((8, d), F32).at[:batch].set(c).at[batch].set(c_ctx)
    mods = _mods_call(cvec, ada_w, ada_b)
    lng = ln_g.reshape(depth * 2, 1, d)
    lnb = ln_b.reshape(depth * 2, 1, d)

    h_lat = x.reshape(batch * seq, d)
    h_ctx = ctx.reshape(batch * lc, d)
    lat = dict(tm=TM_MLP, row_base=0, tiles_per_row=seq // TM_MLP, alpha=alpha)
    cx = dict(tm=lc, row_base=batch, tiles_per_row=1 << 20, alpha=alpha)

    for i in range(depth):
        j = i // 2
        ctx_live = any(m % 2 == 0 for m in range(i + 1, depth))
        if i % 2 == 0:
            w_in = even_w_in[j]
            w_main = jnp.concatenate([w_in[:, :d], w_in[:, 3 * d:n_main]], axis=1).astype(BF16)
            w_k = w_in[:, d:2 * d].astype(BF16)
            w_vt = w_in[:, 2 * d:3 * d].T.astype(BF16)
            w_dt = jnp.zeros((d, 2 * LANES), F32)
            w_dt = w_dt.at[:, :SSM_HEADS].set(w_in[:, n_main:n_main + SSM_HEADS])
            w_dt = w_dt.at[:, LANES:LANES + SSM_HEADS].set(w_in[:, n_main + SSM_HEADS:]).astype(BF16)
            pad16 = ((0, 0), (0, 0), (0, LANES - SSM_HEADS))
            dtb = jnp.pad(ssm_dt_bias[j][:, None, :], pad16)
            alog = jnp.pad(ssm_a_log[j][:, None, :], pad16)
            dskip = jnp.repeat(ssm_d[j], SSM_HEAD_DIM)[None, :]
            norm_w = ssm_norm_w[j][None, :]
            w_out = even_w_out[j].astype(BF16)
            e2 = _rpb_table(na_rpb[j])

            proj_l, k_l, vt_l, dt_l = _inproj_call(h_lat, mods, i, w_main, w_k, w_vt, w_dt, seq, tm=TM_PROJ,
                                                   row_base=0, tiles_per_row=seq // TM_PROJ, name="even_in_lat")
            proj_c, k_c, vt_c, dt_c = _inproj_call(h_ctx, mods, i, w_main, w_k, w_vt, w_dt, lc, tm=lc,
                                                   row_base=batch, tiles_per_row=1 << 20, name="even_in_ctx")

            attn_l = _natten_call(proj_l, k_l, vt_l, k_c, vt_c, e2, batch, seq, lc)
            xbc_col = d + inner
            xbc_l = _dwconv_silu_call(proj_l, xbc_col, inner + 2 * gn, ssm_conv_w[j], ssm_conv_b[j], seq,
                                      tb=512, tc=512)
            xbc_c = _dwconv_silu_call(proj_c, xbc_col, inner + 2 * gn, ssm_conv_w[j], ssm_conv_b[j], lc,
                                      tb=lc, tc=512)
            h0 = jnp.zeros((batch, 2, SSM_HEADS // 2, SSM_STATE, LANES), F32)
            y_c, h_c = _ssd_call(xbc_c, dt_c, dtb, alog, h0, batch, lc)
            y_l, _ = _ssd_call(xbc_l, dt_l, dtb, alog, h_c, batch, seq)

            z_cb = d // inner
            h_lat = _evenout_call(attn_l, y_l, xbc_l, proj_l, z_cb, h_lat, mods, i, dskip, norm_w, w_out,
                                  lng, lnb, 2 * i, **lat)
            w1c = ffn_w1[j].astype(BF16)
            w3c = ffn_w3[j].astype(BF16)
            w2c = ffn_w2[j].astype(BF16)
            h_lat = _ffn_call(h_lat, mods, i, w1c, w3c, w2c, lng, lnb, 2 * i + 1, **lat)
            if ctx_live:
                attn_c = _ctxattn_call(proj_c, k_c, vt_c, batch, lc, d)
                h_ctx = _evenout_call(attn_c, y_c, xbc_c, proj_c, z_cb, h_ctx, mods, i, dskip, norm_w, w_out,
                                      lng, lnb, 2 * i, **cx)
                h_ctx = _ffn_call(h_ctx, mods, i, w1c, w3c, w2c, lng, lnb, 2 * i + 1, **cx)
        else:
            w_in = sc_w_in[j].astype(BF16)
            w_out = sc_w_out[j].astype(BF16)
            wr_pad = jnp.pad(moe_router[j], ((0, 0), (0, LANES - N_EXPERTS)))
            proj_l = _inproj_call(h_lat, mods, i, w_in, None, None, None, seq, tm=TM_PROJ, row_base=0,
                                  tiles_per_row=seq // TM_PROJ, name="odd_in_lat")
            h_lat = _scout_call(proj_l, h_lat, mods, i, sc_conv_w[j], w_out, lng, lnb, 2 * i, seq, **lat)
            h_lat = _moe_layer(h_lat, mods, i, wr_pad, moe_w1, moe_w3, moe_w2, j, lng, lnb, 2 * i + 1, **lat)
            if ctx_live:
                proj_c = _inproj_call(h_ctx, mods, i, w_in, None, None, None, lc, tm=lc, row_base=batch,
                                      tiles_per_row=1 << 20, name="odd_in_ctx")
                h_ctx = _scout_call(proj_c, h_ctx, mods, i, sc_conv_w[j], w_out, lng, lnb, 2 * i, lc, **cx)
                h_ctx = _moe_layer(h_ctx, mods, i, wr_pad, moe_w1, moe_w3, moe_w2, j, lng, lnb, 2 * i + 1, **cx)
    return h_lat.reshape(batch, seq, d)
```
